```python
import math
import jax, jax.numpy as jnp
from jax import lax
import numpy as np

D_MODEL = 1024
BATCH = 8
SEQ = 2048
DEPTH = 1
DEC_BATCH = 32
DEC_SEQ = 4
PAST_LEN = 8192
PAGE_SIZE = 128

D_MIX = D_MODEL
D_ATTN = D_MIX // 2
D_CONV = D_MIX - D_ATTN
N_HEADS = 4
HEAD_DIM = D_ATTN // (2 * N_HEADS)
V_DIM = 2 * HEAD_DIM
CONV_WIDTH = 31
D_FF = 4 * D_MODEL
D_PLE = 256
Q_BLOCK = 128
EPS = 1e-6
N_IN = 3 * D_ATTN + 2 * D_CONV

kernel_name = 'hymba_conformer_diffattn_step'


def rms_norm(x, g):
    xf = x.astype(jnp.float32)
    y = xf * lax.rsqrt(jnp.mean(xf * xf, axis=-1, keepdims=True) + EPS)
    return (y * g.astype(jnp.float32)).astype(x.dtype)


def layer_norm(x, g, b):
    xf = x.astype(jnp.float32)
    mu = jnp.mean(xf, axis=-1, keepdims=True)
    var = jnp.mean(jnp.square(xf - mu), axis=-1, keepdims=True)
    y = (xf - mu) * lax.rsqrt(var + EPS) * g.astype(jnp.float32) + b.astype(jnp.float32)
    return y.astype(x.dtype)


def alibi_slopes():
    return jnp.asarray([2.0 ** (-8.0 * (h + 1) / N_HEADS) for h in range(N_HEADS)], dtype=jnp.float32)


def diff_lambda(lq1, lk1, lq2, lk2, lam_init):
    f = jnp.float32
    return (jnp.exp(jnp.sum(lq1.astype(f) * lk1.astype(f)))
            - jnp.exp(jnp.sum(lq2.astype(f) * lk2.astype(f))) + lam_init)


def diff_attend(q, k, v, q_pos, k_pos, lam):
    s = jnp.einsum('bqhcd,bkhcd->bhcqk', q, k).astype(jnp.float32) * (HEAD_DIM ** -0.5)
    dist = (q_pos[:, None] - k_pos[None, :]).astype(jnp.float32)
    s = s - alibi_slopes()[None, :, None, None, None] * dist
    s = jnp.where(k_pos[None, :] <= q_pos[:, None], s, -jnp.inf)
    pr = jax.nn.softmax(s, axis=-1)
    w = pr[:, :, 0] - lam * pr[:, :, 1]
    return jnp.einsum('bhqk,bkhe->bqhe', w.astype(v.dtype), v)


def causal_depthwise_conv(u_ext, w_dw, b_dw):
    y = lax.conv_general_dilated(u_ext, w_dw[:, None, :], window_strides=(1,), padding='VALID',
                                 dimension_numbers=('NWC', 'WIO', 'NWC'), feature_group_count=D_CONV)
    return y + b_dw


def mixer(xn, lp, lam_init, past_k, past_v, conv_prev):
    B, T, _ = xn.shape
    z = xn @ lp['w_in']
    q = z[..., :D_ATTN].reshape(B, T, N_HEADS, 2, HEAD_DIM)
    k = z[..., D_ATTN:2 * D_ATTN].reshape(B, T, N_HEADS, 2, HEAD_DIM)
    v = z[..., 2 * D_ATTN:3 * D_ATTN].reshape(B, T, N_HEADS, V_DIM)
    a, gt = jnp.split(z[..., 3 * D_ATTN:], 2, axis=-1)
    u = a * jax.nn.sigmoid(gt)
    lam = diff_lambda(lp['lambda_q1'], lp['lambda_k1'], lp['lambda_q2'], lp['lambda_k2'], lam_init)
    if past_k is None:
        nb = T // Q_BLOCK
        qb = jnp.moveaxis(q.reshape(B, nb, Q_BLOCK, N_HEADS, 2, HEAD_DIM), 1, 0)
        k_pos = jnp.arange(T)

        def block(args):
            q_blk, i = args
            return diff_attend(q_blk, k, v, i * Q_BLOCK + jnp.arange(Q_BLOCK), k_pos, lam)

        o = lax.map(block, (qb, jnp.arange(nb)))
        o_attn = jnp.moveaxis(o, 0, 1).reshape(B, T, N_HEADS, V_DIM)
        u_ext = jnp.pad(u, ((0, 0), (CONV_WIDTH - 1, 0), (0, 0)))
    else:
        P = past_k.shape[1]
        k_all = jnp.concatenate([past_k, k], axis=1)
        v_all = jnp.concatenate([past_v, v], axis=1)
        o_attn = diff_attend(q, k_all, v_all, P + jnp.arange(T), jnp.arange(P + T), lam)
        u_ext = jnp.concatenate([conv_prev, u], axis=1)
    conv_state = u_ext[:, -(CONV_WIDTH - 1):]
    y_attn = (rms_norm(o_attn, lp['g_subln']) * (1.0 - lam_init)).reshape(B, T, D_ATTN)
    c = causal_depthwise_conv(u_ext, lp['w_dw'], lp['b_dw'])
    y_conv = jax.nn.silu(layer_norm(c, lp['ln_conv_g'], lp['ln_conv_b']))
    out = jnp.concatenate([y_attn, y_conv], axis=-1) @ lp['w_out']
    return out, k, v, conv_state


def decoder_layer(x, pe, lp, lam_init, past_k, past_v, conv_prev):
    m, k, v, conv_state = mixer(rms_norm(x, lp['g_pre_mix']), lp, lam_init, past_k, past_v, conv_prev)
    h = x + rms_norm(m, lp['g_post_mix'])
    f = jnp.square(jax.nn.relu(rms_norm(h, lp['g_pre_ffn']) @ lp['w_ff1'])) @ lp['w_ff2']
    h = h + rms_norm(f, lp['g_post_ffn'])
    h = h + jax.nn.sigmoid(h @ lp['w_ple_gate']) * (pe @ lp['w_ple'])
    return h, k, v, conv_state


def setup_inputs(seed: int = 0) -> dict:
    key = jax.random.key(seed)
    ks = jax.random.split(key, 32)
    f = jnp.float32
    n_pages = PAST_LEN // PAGE_SIZE
    n_pool = (DEC_BATCH * n_pages * 5) // 4

    def nrm(k, shape, scale):
        return jax.random.normal(k, shape, f) * scale

    def gain(k, shape):
        return 1.0 + 0.02 * jax.random.normal(k, shape, f)

    perm = jax.random.permutation(ks[0], n_pool)[: DEC_BATCH * n_pages]
    page_table = perm.reshape(DEC_BATCH, n_pages).astype(jnp.int32)
    return {
        'x_prompt': nrm(ks[1], (BATCH, SEQ, D_MODEL), 1.0),
        'x_sample': nrm(ks[2], (DEC_BATCH, DEC_SEQ, D_MODEL), 1.0),
        'cache_k': nrm(ks[3], (DEPTH, n_pool, PAGE_SIZE, N_HEADS, 2, HEAD_DIM), 1.0),
        'cache_v': nrm(ks[4], (DEPTH, n_pool, PAGE_SIZE, N_HEADS, V_DIM), 1.0),
        'state_conv': nrm(ks[5], (DEPTH, DEC_BATCH, CONV_WIDTH - 1, D_CONV), 0.5),
        'page_table': page_table,
        'p_prompt': nrm(ks[6], (DEPTH, BATCH, SEQ, D_PLE), 1.0),
        'p_sample': nrm(ks[7], (DEPTH, DEC_BATCH, DEC_SEQ, D_PLE), 1.0),
        'w_in': nrm(ks[8], (DEPTH, D_MODEL, N_IN), D_MODEL ** -0.5),
        'w_out': nrm(ks[9], (DEPTH, D_MIX, D_MODEL), D_MIX ** -0.5),
        'lambda_q1': nrm(ks[10], (DEPTH, HEAD_DIM), 0.1),
        'lambda_k1': nrm(ks[11], (DEPTH, HEAD_DIM), 0.1),
        'lambda_q2': nrm(ks[12], (DEPTH, HEAD_DIM), 0.1),
        'lambda_k2': nrm(ks[13], (DEPTH, HEAD_DIM), 0.1),
        'g_subln': gain(ks[14], (DEPTH, V_DIM)),
        'w_dw': nrm(ks[15], (DEPTH, CONV_WIDTH, D_CONV), CONV_WIDTH ** -0.5),
        'b_dw': nrm(ks[16], (DEPTH, D_CONV), 0.02),
        'ln_conv_g': gain(ks[17], (DEPTH, D_CONV)),
        'ln_conv_b': nrm(ks[18], (DEPTH, D_CONV), 0.02),
        'g_pre_mix': gain(ks[19], (DEPTH, D_MODEL)),
        'g_post_mix': gain(ks[20], (DEPTH, D_MODEL)),
        'g_pre_ffn': gain(ks[21], (DEPTH, D_MODEL)),
        'g_post_ffn': gain(ks[22], (DEPTH, D_MODEL)),
        'w_ff1': nrm(ks[23], (DEPTH, D_MODEL, D_FF), D_MODEL ** -0.5),
        'w_ff2': nrm(ks[24], (DEPTH, D_FF, D_MODEL), D_FF ** -0.5),
        'w_ple': nrm(ks[25], (DEPTH, D_PLE, D_MODEL), D_PLE ** -0.5),
        'w_ple_gate': nrm(ks[26], (DEPTH, D_MODEL, D_MODEL), D_MODEL ** -0.5),
    }


def reference(x_prompt, x_sample, cache_k, cache_v, state_conv, page_table, p_prompt, p_sample,
              w_in, w_out, lambda_q1, lambda_k1, lambda_q2, lambda_k2, g_subln, w_dw, b_dw,
              ln_conv_g, ln_conv_b, g_pre_mix, g_post_mix, g_pre_ffn, g_post_ffn,
              w_ff1, w_ff2, w_ple, w_ple_gate):
    hp, hs = x_prompt, x_sample
    DB = x_sample.shape[0]
    kp_l, vp_l, cp_l, ks_l, vs_l, cs_l = [], [], [], [], [], []
    for l in range(DEPTH):
        lam_init = 0.8 - 0.6 * math.exp(-0.3 * l)
        lp = {
            'w_in': w_in[l], 'w_out': w_out[l],
            'lambda_q1': lambda_q1[l], 'lambda_k1': lambda_k1[l],
            'lambda_q2': lambda_q2[l], 'lambda_k2': lambda_k2[l],
            'g_subln': g_subln[l], 'w_dw': w_dw[l], 'b_dw': b_dw[l],
            'ln_conv_g': ln_conv_g[l], 'ln_conv_b': ln_conv_b[l],
            'g_pre_mix': g_pre_mix[l], 'g_post_mix': g_post_mix[l],
            'g_pre_ffn': g_pre_ffn[l], 'g_post_ffn': g_post_ffn[l],
            'w_ff1': w_ff1[l], 'w_ff2': w_ff2[l],
            'w_ple': w_ple[l], 'w_ple_gate': w_ple_gate[l],
        }
        hp, kp, vp, cp = decoder_layer(hp, p_prompt[l], lp, lam_init, None, None, None)
        past_k = cache_k[l][page_table].reshape(DB, -1, N_HEADS, 2, HEAD_DIM)
        past_v = cache_v[l][page_table].reshape(DB, -1, N_HEADS, V_DIM)
        hs, kn, vn, cn = decoder_layer(hs, p_sample[l], lp, lam_init, past_k, past_v, state_conv[l])
        kp_l.append(kp); vp_l.append(vp); cp_l.append(cp)
        ks_l.append(kn); vs_l.append(vn); cs_l.append(cn)
    k_prompt = jnp.stack(kp_l)
    v_prompt = jnp.stack(vp_l)
    conv_prompt = jnp.stack(cp_l)
    k_sample = jnp.stack(ks_l)
    v_sample = jnp.stack(vs_l)
    conv_sample = jnp.stack(cs_l)
    return (hp, hs, k_prompt, v_prompt, conv_prompt, k_sample, v_sample, conv_sample)
```

```python
import functools
import math

import jax
import jax.numpy as jnp
from jax import lax
from jax.experimental import pallas as pl
from jax.experimental.pallas import tpu as pltpu

D_MODEL = 1024
D_ATTN = 512
D_CONV = 512
N_HEADS = 4
HEAD_DIM = 64
V_DIM = 128
CONV_WIDTH = 31
D_FF = 4096
D_PLE = 256
PAGE_SIZE = 128
EPS = 1e-6
N_QKV = 3 * D_ATTN
N_IN = N_QKV + 2 * D_CONV
NEG_BIG = -1e30
ALIBI_SLOPES = tuple(2.0 ** (-8.0 * (h + 1) / N_HEADS) for h in range(N_HEADS))

F32 = jnp.float32
BF16 = jnp.bfloat16

VMEM_LIMIT_BYTES = 56 * 1024 * 1024


def _cparams(semantics):
    return pltpu.CompilerParams(dimension_semantics=semantics, vmem_limit_bytes=VMEM_LIMIT_BYTES)


def _rms(x, g):
    return x * lax.rsqrt(jnp.mean(x * x, axis=-1, keepdims=True) + EPS) * g


def _diff_lambda(lq1, lk1, lq2, lk2, lam_init):
    return (jnp.exp(jnp.sum(lq1 * lk1, axis=-1, keepdims=True))
            - jnp.exp(jnp.sum(lq2 * lk2, axis=-1, keepdims=True)) + lam_init)


def _nt_dot(a, b):
    return lax.dot_general(a, b, (((1,), (1,)), ((), ())), preferred_element_type=F32)


def _inproj_kernel(x_ref, g_ref, w_ref, qkv_ref, u_ref):
    xn = _rms(x_ref[...], g_ref[...]).astype(BF16)
    for c in range(3):
        sl = slice(c * D_ATTN, (c + 1) * D_ATTN)
        qkv_ref[:, sl] = jnp.dot(xn, w_ref[:, sl], preferred_element_type=F32)
    a = jnp.dot(xn, w_ref[:, N_QKV:N_QKV + D_CONV], preferred_element_type=F32)
    gt = jnp.dot(xn, w_ref[:, N_QKV + D_CONV:], preferred_element_type=F32)
    u_ref[...] = a * jax.nn.sigmoid(gt)


def _inproj(x2d, g_pre_mix, w_in16, tm):
    m = x2d.shape[0]
    return pl.pallas_call(
        _inproj_kernel,
        grid=(m // tm,),
        in_specs=[
            pl.BlockSpec((tm, D_MODEL), lambda i: (i, 0)),
            pl.BlockSpec((1, D_MODEL), lambda i: (0, 0)),
            pl.BlockSpec((D_MODEL, N_IN), lambda i: (0, 0)),
        ],
        out_specs=[
            pl.BlockSpec((tm, N_QKV), lambda i: (i, 0)),
            pl.BlockSpec((tm, D_CONV), lambda i: (i, 0)),
        ],
        out_shape=[
            jax.ShapeDtypeStruct((m, N_QKV), F32),
            jax.ShapeDtypeStruct((m, D_CONV), F32),
        ],
        compiler_params=_cparams(("parallel",)),
        name="inproj",
    )(x2d, g_pre_mix, w_in16)


def _attn_prompt_kernel(lq1_ref, lk1_ref, lq2_ref, lk2_ref, gsub_ref, q_ref, k_ref, v_ref, o_ref,
                        m_scr, l_scr, acc_scr, *, tq, tk, lam_init):
    qi = pl.program_id(1)
    ki = pl.program_id(2)

    @pl.when(ki == 0)
    def _():
        m_scr[...] = jnp.full(m_scr.shape, NEG_BIG, F32)
        l_scr[...] = jnp.zeros(l_scr.shape, F32)
        acc_scr[...] = jnp.zeros(acc_scr.shape, F32)

    def step(diagonal):
        lane = lax.broadcasted_iota(jnp.int32, (tq, 2 * HEAD_DIM), 1)
        kcol = lax.broadcasted_iota(jnp.int32, (1, tk), 1)
        koff = (kcol + (ki * tk - qi * tq)).astype(F32)
        if diagonal:
            row = lax.broadcasted_iota(jnp.int32, (2 * tq, tk), 0)
            row = jnp.where(row >= tq, row - tq, row)
            col = lax.broadcasted_iota(jnp.int32, (2 * tq, tk), 1)
            keep = col <= row
        for h in range(N_HEADS):
            hs = slice(h * V_DIM, (h + 1) * V_DIM)
            q = q_ref[:, hs] * (HEAD_DIM ** -0.5)
            q1 = jnp.where(lane < HEAD_DIM, q, 0.0).astype(BF16)
            q2 = jnp.where(lane >= HEAD_DIM, q, 0.0).astype(BF16)
            qs = jnp.concatenate([q1, q2], axis=0)
            s = _nt_dot(qs, k_ref[:, hs].astype(BF16))
            s = s + ALIBI_SLOPES[h] * koff
            if diagonal:
                s = jnp.where(keep, s, NEG_BIG)
            m_prev = m_scr[h]
            m_new = jnp.maximum(m_prev, jnp.max(s, axis=-1, keepdims=True))
            alpha = jnp.exp(m_prev - m_new)
            p = jnp.exp(s - m_new)
            l_scr[h] = alpha * l_scr[h] + jnp.sum(p, axis=-1, keepdims=True)
            acc_scr[h] = alpha * acc_scr[h] + jnp.dot(
                p.astype(BF16), v_ref[:, hs].astype(BF16), preferred_element_type=F32)
            m_scr[h] = m_new

    @pl.when(ki < qi)
    def _():
        step(False)

    @pl.when(ki == qi)
    def _():
        step(True)
        lam = _diff_lambda(lq1_ref[...], lk1_ref[...], lq2_ref[...], lk2_ref[...], lam_init)
        for h in range(N_HEADS):
            o = acc_scr[h] / l_scr[h]
            oh = o[:tq] - lam * o[tq:]
            y = _rms(oh, gsub_ref[...]) * (1.0 - lam_init)
            o_ref[:, h * V_DIM:(h + 1) * V_DIM] = y.astype(o_ref.dtype)


def _attn_prompt(qkv, lam_vecs, g_subln, batch, seq, lam_init, tq=512):
    tk = tq
    nq = seq // tq
    kern = functools.partial(_attn_prompt_kernel, tq=tq, tk=tk, lam_init=lam_init)
    vec = lambda n: pl.BlockSpec((1, n), lambda b, qi, ki: (0, 0))
    return pl.pallas_call(
        kern,
        grid=(batch, nq, nq),
        in_specs=[vec(HEAD_DIM)] * 4 + [
            vec(V_DIM),
            pl.BlockSpec((tq, D_ATTN), lambda b, qi, ki: (b * nq + qi, 0)),
            pl.BlockSpec((tk, D_ATTN), lambda b, qi, ki: (b * nq + jnp.minimum(ki, qi), 1)),
            pl.BlockSpec((tk, D_ATTN), lambda b, qi, ki: (b * nq + jnp.minimum(ki, qi), 2)),
        ],
        out_specs=pl.BlockSpec((tq, D_ATTN), lambda b, qi, ki: (b * nq + qi, 0)),
        out_shape=jax.ShapeDtypeStruct((batch * seq, D_ATTN), BF16),
        scratch_shapes=[
            pltpu.VMEM((N_HEADS, 2 * tq, 1), F32),
            pltpu.VMEM((N_HEADS, 2 * tq, 1), F32),
            pltpu.VMEM((N_HEADS, 2 * tq, V_DIM), F32),
        ],
        compiler_params=_cparams(("parallel", "parallel", "arbitrary")),
        name="attn_prompt",
    )(*lam_vecs, g_subln, qkv, qkv, qkv)


def _attn_paged_kernel(pt_ref, lq1_ref, lk1_ref, lq2_ref, lk2_ref, gsub_ref, q_ref, kn_ref, vn_ref,
                       *rest, pps, n_steps, past_len, dec_seq, lam_init):
    kp_refs = rest[:pps]
    vp_refs = rest[pps:2 * pps]
    o_ref = rest[2 * pps]
    qbd_scr, m_scr, l_scr, acc_scr, kn_scr, vn_scr = rest[2 * pps + 1:]
    j = pl.program_id(1)
    n_rows = dec_seq * 2 * N_HEADS

    row = lax.broadcasted_iota(jnp.int32, (n_rows, 1), 0)
    hc = row % (2 * N_HEADS)
    head = hc // 2
    slope = jnp.zeros((n_rows, 1), F32)
    for h in range(N_HEADS):
        slope = jnp.where(head == h, ALIBI_SLOPES[h], slope)

    @pl.when(j == 0)
    def _():
        q = q_ref[...] * (HEAD_DIM ** -0.5)
        qrep = jnp.concatenate(
            [jnp.broadcast_to(q[t:t + 1, :], (2 * N_HEADS, D_ATTN)) for t in range(dec_seq)], axis=0)
        col = lax.broadcasted_iota(jnp.int32, (n_rows, D_ATTN), 1)
        qbd_scr[...] = jnp.where(col // HEAD_DIM == hc, qrep, 0.0).astype(BF16)
        m_scr[...] = jnp.full(m_scr.shape, NEG_BIG, F32)
        l_scr[...] = jnp.zeros(l_scr.shape, F32)
        acc_scr[...] = jnp.zeros(acc_scr.shape, F32)

    qbd = qbd_scr[...]

    def online_update(s, v_list):
        m_prev = m_scr[...]
        m_new = jnp.maximum(m_prev, jnp.max(s, axis=-1, keepdims=True))
        alpha = jnp.exp(m_prev - m_new)
        p = jnp.exp(s - m_new)
        l_scr[...] = alpha * l_scr[...] + jnp.sum(p, axis=-1, keepdims=True)
        p16 = p.astype(BF16)
        pv = jnp.zeros((n_rows, D_ATTN), F32)
        for i, v in enumerate(v_list):
            pv = pv + jnp.dot(p16[:, i * PAGE_SIZE:(i + 1) * PAGE_SIZE], v, preferred_element_type=F32)
        acc_scr[...] = alpha * acc_scr[...] + pv
        m_scr[...] = m_new

    s = jnp.concatenate([_nt_dot(qbd, kp_refs[i][...].astype(BF16)) for i in range(pps)], axis=-1)
    kcol = lax.broadcasted_iota(jnp.int32, (1, pps * PAGE_SIZE), 1)
    koff = (kcol + (j * (pps * PAGE_SIZE) - past_len)).astype(F32)
    online_update(s + slope * koff, [vp_refs[i][...].astype(BF16) for i in range(pps)])

    @pl.when(j == n_steps - 1)
    def _():
        kn_scr[...] = jnp.zeros(kn_scr.shape, F32)
        vn_scr[...] = jnp.zeros(vn_scr.shape, F32)
        kn_scr[0:dec_seq, :] = kn_ref[...]
        vn_scr[0:dec_seq, :] = vn_ref[...]
        sn = _nt_dot(qbd, kn_scr[...].astype(BF16))
        kidx = lax.broadcasted_iota(jnp.int32, (n_rows, PAGE_SIZE), 1)
        tok = lax.broadcasted_iota(jnp.int32, (n_rows, PAGE_SIZE), 0) // (2 * N_HEADS)
        sn = jnp.where(kidx <= tok, sn + slope * kidx.astype(F32), NEG_BIG)
        online_update(sn, [vn_scr[...].astype(BF16)])

        lam = _diff_lambda(lq1_ref[...], lk1_ref[...], lq2_ref[...], lk2_ref[...], lam_init)
        o = acc_scr[...] / l_scr[...]
        coef = jnp.where(hc % 2 == 0, 1.0, -lam)
        col = lax.broadcasted_iota(jnp.int32, (n_rows, D_ATTN), 1)
        o = jnp.where(col // V_DIM == head, o * coef, 0.0)
        y = jnp.sum(o.reshape(dec_seq, 2 * N_HEADS, D_ATTN), axis=1)
        for h in range(N_HEADS):
            hs = slice(h * V_DIM, (h + 1) * V_DIM)
            o_ref[:, hs] = (_rms(y[:, hs], gsub_ref[...]) * (1.0 - lam_init)).astype(o_ref.dtype)


def _attn_paged(qkv_s, cache_k, cache_v, page_table, lam_vecs, g_subln, dec_batch, dec_seq, lam_init,
                pps=8):
    n_pages = page_table.shape[1]
    n_steps = n_pages // pps
    past_len = n_pages * PAGE_SIZE
    n_rows = dec_seq * 2 * N_HEADS
    qkv3 = qkv_s.reshape(dec_batch, dec_seq, N_QKV)
    kern = functools.partial(_attn_paged_kernel, pps=pps, n_steps=n_steps, past_len=past_len,
                             dec_seq=dec_seq, lam_init=lam_init)
    vec = lambda n: pl.BlockSpec((1, n), lambda b, j, pt: (0, 0))

    def page_spec(i):
        return pl.BlockSpec((None, PAGE_SIZE, D_ATTN), lambda b, j, pt: (pt[b, j * pps + i], 0, 0))

    def new_spec(c):
        return pl.BlockSpec((None, dec_seq, D_ATTN), lambda b, j, pt: (b, 0, c))

    grid_spec = pltpu.PrefetchScalarGridSpec(
        num_scalar_prefetch=1,
        grid=(dec_batch, n_steps),
        in_specs=[vec(HEAD_DIM)] * 4 + [vec(V_DIM), new_spec(0), new_spec(1), new_spec(2)]
        + [page_spec(i) for i in range(pps)] * 2,
        out_specs=pl.BlockSpec((None, dec_seq, D_ATTN), lambda b, j, pt: (b, 0, 0)),
        scratch_shapes=[
            pltpu.VMEM((n_rows, D_ATTN), BF16),
            pltpu.VMEM((n_rows, 1), F32),
            pltpu.VMEM((n_rows, 1), F32),
            pltpu.VMEM((n_rows, D_ATTN), F32),
            pltpu.VMEM((PAGE_SIZE, D_ATTN), F32),
            pltpu.VMEM((PAGE_SIZE, D_ATTN), F32),
        ],
    )
    out = pl.pallas_call(
        kern,
        grid_spec=grid_spec,
        out_shape=jax.ShapeDtypeStruct((dec_batch, dec_seq, D_ATTN), F32),
        compiler_params=_cparams(("parallel", "arbitrary")),
        name="attn_paged",
    )(page_table, *lam_vecs, g_subln, qkv3, qkv3, qkv3, *([cache_k] * pps), *([cache_v] * pps))
    return out.reshape(dec_batch * dec_seq, D_ATTN)


CONV_HALO = 32
CONV_ROW_CHUNK = 16


def _ln_swish(c, g, b):
    mu = jnp.mean(c, axis=-1, keepdims=True)
    d = c - mu
    var = jnp.mean(d * d, axis=-1, keepdims=True)
    y = d * lax.rsqrt(var + EPS) * g + b
    return y * jax.nn.sigmoid(y)


def _conv_prompt_kernel(prev_ref, cur_ref, w_ref, b_ref, g_ref, beta_ref, o_ref, ext_scr, *, tm):
    i = pl.program_id(1)
    ext_scr[0:CONV_HALO, :] = jnp.where(i == 0, 0.0, prev_ref[...])
    ext_scr[CONV_HALO:CONV_HALO + tm, :] = cur_ref[...]
    first = CONV_HALO - (CONV_WIDTH - 1)
    for r in range(0, tm, CONV_ROW_CHUNK):
        acc = jnp.broadcast_to(b_ref[...], (CONV_ROW_CHUNK, D_CONV))
        for j in range(CONV_WIDTH):
            acc = acc + w_ref[j:j + 1, :] * ext_scr[r + first + j:r + first + j + CONV_ROW_CHUNK, :]
        o_ref[r:r + CONV_ROW_CHUNK, :] = _ln_swish(acc, g_ref[...], beta_ref[...]).astype(o_ref.dtype)


def _conv_prompt(u2d, w_dw, b_dw, ln_g, ln_b, batch, seq, tm=128):
    nt = seq // tm
    halo_per_tile = tm // CONV_HALO
    kern = functools.partial(_conv_prompt_kernel, tm=tm)
    vec = pl.BlockSpec((1, D_CONV), lambda b, i: (0, 0))
    return pl.pallas_call(
        kern,
        grid=(batch, nt),
        in_specs=[
            pl.BlockSpec((CONV_HALO, D_CONV),
                         lambda b, i: (jnp.maximum((b * nt + i) * halo_per_tile - 1, 0), 0)),
            pl.BlockSpec((tm, D_CONV), lambda b, i: (b * nt + i, 0)),
            pl.BlockSpec((CONV_WIDTH, D_CONV), lambda b, i: (0, 0)),
            vec, vec, vec,
        ],
        out_specs=pl.BlockSpec((tm, D_CONV), lambda b, i: (b * nt + i, 0)),
        out_shape=jax.ShapeDtypeStruct((batch * seq, D_CONV), BF16),
        scratch_shapes=[pltpu.VMEM((CONV_HALO + tm, D_CONV), F32)],
        compiler_params=_cparams(("parallel", "parallel")),
        name="conv_prompt",
    )(u2d, u2d, w_dw, b_dw, ln_g, ln_b)


def _conv_sample_kernel(state_ref, u_ref, w_ref, b_ref, g_ref, beta_ref, o_ref, st_ref, ext_scr, *, dec_seq):
    n_state = CONV_WIDTH - 1
    ext_scr[0:n_state, :] = state_ref[...]
    ext_scr[n_state:n_state + dec_seq, :] = u_ref[...]
    w = w_ref[...]
    rows = []
    for t in range(dec_seq):
        rows.append(jnp.sum(w * ext_scr[t:t + CONV_WIDTH, :], axis=0, keepdims=True))
    c = jnp.concatenate(rows, axis=0) + b_ref[...]
    o_ref[...] = _ln_swish(c, g_ref[...], beta_ref[...]).astype(o_ref.dtype)
    st_ref[...] = ext_scr[dec_seq:dec_seq + n_state, :]


def _conv_sample(state, u_s, w_dw, b_dw, ln_g, ln_b, dec_batch, dec_seq):
    n_state = CONV_WIDTH - 1
    u3 = u_s.reshape(dec_batch, dec_seq, D_CONV)
    kern = functools.partial(_conv_sample_kernel, dec_seq=dec_seq)
    vec = pl.BlockSpec((1, D_CONV), lambda b: (0, 0))
    y, st = pl.pallas_call(
        kern,
        grid=(dec_batch,),
        in_specs=[
            pl.BlockSpec((None, n_state, D_CONV), lambda b: (b, 0, 0)),
            pl.BlockSpec((None, dec_seq, D_CONV), lambda b: (b, 0, 0)),
            pl.BlockSpec((CONV_WIDTH, D_CONV), lambda b: (0, 0)),
            vec, vec, vec,
        ],
        out_specs=[
            pl.BlockSpec((None, dec_seq, D_CONV), lambda b: (b, 0, 0)),
            pl.BlockSpec((None, n_state, D_CONV), lambda b: (b, 0, 0)),
        ],
        out_shape=[
            jax.ShapeDtypeStruct((dec_batch, dec_seq, D_CONV), F32),
            jax.ShapeDtypeStruct((dec_batch, n_state, D_CONV), F32),
        ],
        scratch_shapes=[pltpu.VMEM((n_state + dec_seq + 6, D_CONV), F32)],
        compiler_params=_cparams(("parallel",)),
        name="conv_sample",
    )(state, u3, w_dw, b_dw, ln_g, ln_b)
    return y.reshape(dec_batch * dec_seq, D_CONV), st


def _outproj_kernel(ya_ref, yc_ref, x_ref, w_ref, gpost_ref, gpre_ref, h_ref, hn_ref):
    m = (jnp.dot(ya_ref[...].astype(BF16), w_ref[0:D_ATTN, :], preferred_element_type=F32)
         + jnp.dot(yc_ref[...].astype(BF16), w_ref[D_ATTN:, :], preferred_element_type=F32))
    h = x_ref[...] + _rms(m, gpost_ref[...])
    h_ref[...] = h
    hn_ref[...] = _rms(h, gpre_ref[...]).astype(BF16)


def _outproj(y_attn, y_conv, x2d, w_out16, g_post_mix, g_pre_ffn, tm):
    m = x2d.shape[0]
    vec = pl.BlockSpec((1, D_MODEL), lambda i: (0, 0))
    return pl.pallas_call(
        _outproj_kernel,
        grid=(m // tm,),
        in_specs=[
            pl.BlockSpec((tm, D_ATTN), lambda i: (i, 0)),
            pl.BlockSpec((tm, D_CONV), lambda i: (i, 0)),
            pl.BlockSpec((tm, D_MODEL), lambda i: (i, 0)),
            pl.BlockSpec((D_MODEL, D_MODEL), lambda i: (0, 0)),
            vec, vec,
        ],
        out_specs=[
            pl.BlockSpec((tm, D_MODEL), lambda i: (i, 0)),
            pl.BlockSpec((tm, D_MODEL), lambda i: (i, 0)),
        ],
        out_shape=[
            jax.ShapeDtypeStruct((m, D_MODEL), F32),
            jax.ShapeDtypeStruct((m, D_MODEL), BF16),
        ],
        compiler_params=_cparams(("parallel",)),
        name="outproj",
    )(y_attn, y_conv, x2d, w_out16, g_post_mix, g_pre_ffn)


def _ffn_kernel(hn_ref, h_ref, pe_ref, w1_ref, w2_ref, gpost_ref, wg_ref, wp_ref, o_ref, acc_scr):
    k = pl.program_id(1)
    a = jnp.dot(hn_ref[...], w1_ref[...], preferred_element_type=F32)
    a = jnp.square(jnp.maximum(a, 0.0)).astype(BF16)
    contrib = jnp.dot(a, w2_ref[...], preferred_element_type=F32)

    @pl.when(k == 0)
    def _():
        acc_scr[...] = contrib

    @pl.when(k > 0)
    def _():
        acc_scr[...] += contrib

    @pl.when(k == pl.num_programs(1) - 1)
    def _():
        h2 = h_ref[...] + _rms(acc_scr[...], gpost_ref[...])
        gate = jax.nn.sigmoid(jnp.dot(h2.astype(BF16), wg_ref[...], preferred_element_type=F32))
        pev = jnp.dot(pe_ref[...].astype(BF16), wp_ref[...], preferred_element_type=F32)
        o_ref[...] = h2 + gate * pev


def _ffn(hn, h, pe2d, w_ff1_16, w_ff2_16, g_post_ffn, w_gate16, w_ple16, tm, tf=512):
    m = h.shape[0]
    return pl.pallas_call(
        _ffn_kernel,
        grid=(m // tm, D_FF // tf),
        in_specs=[
            pl.BlockSpec((tm, D_MODEL), lambda i, k: (i, 0)),
            pl.BlockSpec((tm, D_MODEL), lambda i, k: (i, 0)),
            pl.BlockSpec((tm, D_PLE), lambda i, k: (i, 0)),
            pl.BlockSpec((D_MODEL, tf), lambda i, k: (0, k)),
            pl.BlockSpec((tf, D_MODEL), lambda i, k: (k, 0)),
            pl.BlockSpec((1, D_MODEL), lambda i, k: (0, 0)),
            pl.BlockSpec((D_MODEL, D_MODEL), lambda i, k: (0, 0)),
            pl.BlockSpec((D_PLE, D_MODEL), lambda i, k: (0, 0)),
        ],
        out_specs=pl.BlockSpec((tm, D_MODEL), lambda i, k: (i, 0)),
        out_shape=jax.ShapeDtypeStruct((m, D_MODEL), F32),
        scratch_shapes=[pltpu.VMEM((tm, D_MODEL), F32)],
        compiler_params=_cparams(("parallel", "arbitrary")),
        name="ffn",
    )(hn, h, pe2d, w_ff1_16, w_ff2_16, g_post_ffn, w_gate16, w_ple16)


def _row(v):
    return v.reshape(1, -1)


def kernel(x_prompt, x_sample, cache_k, cache_v, state_conv, page_table, p_prompt, p_sample,
           w_in, w_out, lambda_q1, lambda_k1, lambda_q2, lambda_k2, g_subln, w_dw, b_dw,
           ln_conv_g, ln_conv_b, g_pre_mix, g_post_mix, g_pre_ffn, g_post_ffn,
           w_ff1, w_ff2, w_ple, w_ple_gate):
    depth = w_in.shape[0]
    batch, seq, _ = x_prompt.shape
    dec_batch, dec_seq, _ = x_sample.shape
    n_pool = cache_k.shape[1]
    mp, ms = batch * seq, dec_batch * dec_seq

    hp = x_prompt.reshape(mp, D_MODEL)
    hs = x_sample.reshape(ms, D_MODEL)
    outs = {k: [] for k in ("kp", "vp", "cp", "ks", "vs", "cs")}
    for l in range(depth):
        lam_init = 0.8 - 0.6 * math.exp(-0.3 * l)
        w_in16 = w_in[l].astype(BF16)
        w_out16 = w_out[l].astype(BF16)
        w_ff1_16 = w_ff1[l].astype(BF16)
        w_ff2_16 = w_ff2[l].astype(BF16)
        w_gate16 = w_ple_gate[l].astype(BF16)
        w_ple16 = w_ple[l].astype(BF16)
        lam_vecs = [_row(lambda_q1[l]), _row(lambda_k1[l]), _row(lambda_q2[l]), _row(lambda_k2[l])]
        gsub = _row(g_subln[l])
        conv_args = (w_dw[l], _row(b_dw[l]), _row(ln_conv_g[l]), _row(ln_conv_b[l]))
        ck = cache_k[l].reshape(n_pool, PAGE_SIZE, D_ATTN)
        cv = cache_v[l].reshape(n_pool, PAGE_SIZE, D_ATTN)

        qkv_p, u_p = _inproj(hp, _row(g_pre_mix[l]), w_in16, tm=512)
        ya_p = _attn_prompt(qkv_p, lam_vecs, gsub, batch, seq, lam_init)
        yc_p = _conv_prompt(u_p, *conv_args, batch, seq)
        h_p, hn_p = _outproj(ya_p, yc_p, hp, w_out16, _row(g_post_mix[l]), _row(g_pre_ffn[l]), tm=512)
        hp = _ffn(hn_p, h_p, p_prompt[l].reshape(mp, D_PLE), w_ff1_16, w_ff2_16, _row(g_post_ffn[l]),
                  w_gate16, w_ple16, tm=512)
        outs["kp"].append(qkv_p[:, D_ATTN:2 * D_ATTN].reshape(batch, seq, N_HEADS, 2, HEAD_DIM))
        outs["vp"].append(qkv_p[:, 2 * D_ATTN:].reshape(batch, seq, N_HEADS, V_DIM))
        outs["cp"].append(u_p.reshape(batch, seq, D_CONV)[:, seq - (CONV_WIDTH - 1):])

        qkv_s, u_s = _inproj(hs, _row(g_pre_mix[l]), w_in16, tm=ms)
        ya_s = _attn_paged(qkv_s, ck, cv, page_table, lam_vecs, gsub, dec_batch, dec_seq, lam_init)
        yc_s, st_s = _conv_sample(state_conv[l], u_s, *conv_args, dec_batch, dec_seq)
        h_s, hn_s = _outproj(ya_s, yc_s, hs, w_out16, _row(g_post_mix[l]), _row(g_pre_ffn[l]), tm=ms)
        hs = _ffn(hn_s, h_s, p_sample[l].reshape(ms, D_PLE), w_ff1_16, w_ff2_16, _row(g_post_ffn[l]),
                  w_gate16, w_ple16, tm=ms)
        outs["ks"].append(qkv_s[:, D_ATTN:2 * D_ATTN].reshape(dec_batch, dec_seq, N_HEADS, 2, HEAD_DIM))
        outs["vs"].append(qkv_s[:, 2 * D_ATTN:].reshape(dec_batch, dec_seq, N_HEADS, V_DIM))
        outs["cs"].append(st_s)

    return (hp.reshape(batch, seq, D_MODEL), hs.reshape(dec_batch, dec_seq, D_MODEL),
            jnp.stack(outs["kp"]), jnp.stack(outs["vp"]), jnp.stack(outs["cp"]),
            jnp.stack(outs["ks"]), jnp.stack(outs["vs"]), jnp.stack(outs["cs"]))
```

```python
import functools
import math

import jax
import jax.numpy as jnp
from jax import lax
from jax.experimental import pallas as pl
from jax.experimental.pallas import tpu as pltpu

D_MODEL = 1024
D_ATTN = 512
D_CONV = 512
N_HEADS = 4
HEAD_DIM = 64
V_DIM = 128
CONV_WIDTH = 31
D_FF = 4096
D_PLE = 256
PAGE_SIZE = 128
EPS = 1e-6
N_QKV = 3 * D_ATTN
N_IN = N_QKV + 2 * D_CONV
NEG_BIG = -1e30
ALIBI_SLOPES = tuple(2.0 ** (-8.0 * (h + 1) / N_HEADS) for h in range(N_HEADS))

F32 = jnp.float32
BF16 = jnp.bfloat16

VMEM_LIMIT_BYTES = 56 * 1024 * 1024


def _cparams(semantics):
    return pltpu.CompilerParams(dimension_semantics=semantics, vmem_limit_bytes=VMEM_LIMIT_BYTES)


def _rms(x, g):
    return x * lax.rsqrt(jnp.mean(x * x, axis=-1, keepdims=True) + EPS) * g


def _diff_lambda(lq1, lk1, lq2, lk2, lam_init):
    return (jnp.exp(jnp.sum(lq1 * lk1, axis=-1, keepdims=True))
            - jnp.exp(jnp.sum(lq2 * lk2, axis=-1, keepdims=True)) + lam_init)


def _nt_dot(a, b):
    return lax.dot_general(a, b, (((1,), (1,)), ((), ())), preferred_element_type=F32)


def _inproj_kernel(x_ref, g_ref, w_ref, wkt_ref, q_ref, kt_ref, vi_ref, v16_ref, u_ref, *, tm):
    xn = _rms(x_ref[...], g_ref[...]).astype(BF16)
    q = jnp.dot(xn, w_ref[:, 0:D_ATTN], preferred_element_type=F32)
    q_ref[...] = (q * (HEAD_DIM ** -0.5)).astype(BF16)
    kt_ref[...] = _nt_dot(wkt_ref[...], xn)
    v = jnp.dot(xn, w_ref[:, 2 * D_ATTN:N_QKV], preferred_element_type=F32)
    v16_ref[...] = v.astype(BF16)
    for h in range(N_HEADS):
        vi_ref[pl.ds(h, tm, stride=N_HEADS), :] = v[:, h * V_DIM:(h + 1) * V_DIM]
    a = jnp.dot(xn, w_ref[:, N_QKV:N_QKV + D_CONV], preferred_element_type=F32)
    gt = jnp.dot(xn, w_ref[:, N_QKV + D_CONV:], preferred_element_type=F32)
    u_ref[...] = a * jax.nn.sigmoid(gt)


def _inproj(x2d, g_pre_mix, w_in16, w_kt16, n_seq, tm):
    m = x2d.shape[0]
    seq = m // n_seq
    nt = seq // tm
    return pl.pallas_call(
        functools.partial(_inproj_kernel, tm=tm),
        grid=(m // tm,),
        in_specs=[
            pl.BlockSpec((tm, D_MODEL), lambda i: (i, 0)),
            pl.BlockSpec((1, D_MODEL), lambda i: (0, 0)),
            pl.BlockSpec((D_MODEL, N_IN), lambda i: (0, 0)),
            pl.BlockSpec((D_ATTN, D_MODEL), lambda i: (0, 0)),
        ],
        out_specs=[
            pl.BlockSpec((tm, D_ATTN), lambda i: (i, 0)),
            pl.BlockSpec((None, D_ATTN, tm), lambda i: (i // nt, 0, i % nt)),
            pl.BlockSpec((tm * N_HEADS, V_DIM), lambda i: (i, 0)),
            pl.BlockSpec((tm, D_ATTN), lambda i: (i, 0)),
            pl.BlockSpec((tm, D_CONV), lambda i: (i, 0)),
        ],
        out_shape=[
            jax.ShapeDtypeStruct((m, D_ATTN), BF16),
            jax.ShapeDtypeStruct((n_seq, D_ATTN, seq), F32),
            jax.ShapeDtypeStruct((m * N_HEADS, V_DIM), F32),
            jax.ShapeDtypeStruct((m, D_ATTN), BF16),
            jax.ShapeDtypeStruct((m, D_CONV), F32),
        ],
        compiler_params=_cparams(("parallel",)),
        name="inproj",
    )(x2d, g_pre_mix, w_in16, w_kt16)


def _attn_prompt_kernel(lq1_ref, lk1_ref, lq2_ref, lk2_ref, gsub_ref, q_ref, kt_ref, v_ref, o_ref,
                        m_scr, l_scr, acc_scr, *, tq, tk, lam_init):
    qi = pl.program_id(1)
    ki = pl.program_id(2)

    @pl.when(ki == 0)
    def _():
        m_scr[...] = jnp.full(m_scr.shape, NEG_BIG, F32)
        l_scr[...] = jnp.zeros(l_scr.shape, F32)
        acc_scr[...] = jnp.zeros(acc_scr.shape, F32)

    def step(diagonal):
        lane = lax.broadcasted_iota(jnp.int32, (tq, 2 * HEAD_DIM), 1)
        kcol = lax.broadcasted_iota(jnp.int32, (1, tk), 1)
        koff = (kcol + (ki * tk - qi * tq)).astype(F32)
        if diagonal:
            row = lax.broadcasted_iota(jnp.int32, (2 * tq, tk), 0)
            row = jnp.where(row >= tq, row - tq, row)
            col = lax.broadcasted_iota(jnp.int32, (2 * tq, tk), 1)
            keep = col <= row
        for h in range(N_HEADS):
            hs = slice(h * V_DIM, (h + 1) * V_DIM)
            q = q_ref[:, hs]
            zero = jnp.zeros_like(q)
            qs = jnp.concatenate([jnp.where(lane < HEAD_DIM, q, zero),
                                  jnp.where(lane >= HEAD_DIM, q, zero)], axis=0)
            s = jnp.dot(qs, kt_ref[hs, :].astype(BF16), preferred_element_type=F32)
            s = s + ALIBI_SLOPES[h] * koff
            if diagonal:
                s = jnp.where(keep, s, NEG_BIG)
            m_prev = m_scr[h]
            m_new = jnp.maximum(m_prev, jnp.max(s, axis=-1, keepdims=True))
            alpha = jnp.exp(m_prev - m_new)
            p = jnp.exp(s - m_new)
            l_scr[h] = alpha * l_scr[h] + jnp.sum(p, axis=-1, keepdims=True)
            acc_scr[h] = alpha * acc_scr[h] + jnp.dot(
                p.astype(BF16), v_ref[:, hs], preferred_element_type=F32)
            m_scr[h] = m_new

    @pl.when(ki < qi)
    def _():
        step(False)

    @pl.when(ki == qi)
    def _():
        step(True)
        lam = _diff_lambda(lq1_ref[...], lk1_ref[...], lq2_ref[...], lk2_ref[...], lam_init)
        for h in range(N_HEADS):
            o = acc_scr[h] / l_scr[h]
            oh = o[:tq] - lam * o[tq:]
            y = _rms(oh, gsub_ref[...]) * (1.0 - lam_init)
            o_ref[:, h * V_DIM:(h + 1) * V_DIM] = y.astype(o_ref.dtype)


def _attn_prompt(q16, kt, v16, lam_vecs, g_subln, batch, seq, lam_init, tq=512):
    tk = tq
    nq = seq // tq
    kern = functools.partial(_attn_prompt_kernel, tq=tq, tk=tk, lam_init=lam_init)
    vec = lambda n: pl.BlockSpec((1, n), lambda b, qi, ki: (0, 0))
    return pl.pallas_call(
        kern,
        grid=(batch, nq, nq),
        in_specs=[vec(HEAD_DIM)] * 4 + [
            vec(V_DIM),
            pl.BlockSpec((tq, D_ATTN), lambda b, qi, ki: (b * nq + qi, 0)),
            pl.BlockSpec((None, D_ATTN, tk), lambda b, qi, ki: (b, 0, jnp.minimum(ki, qi))),
            pl.BlockSpec((tk, D_ATTN), lambda b, qi, ki: (b * nq + jnp.minimum(ki, qi), 0)),
        ],
        out_specs=pl.BlockSpec((tq, D_ATTN), lambda b, qi, ki: (b * nq + qi, 0)),
        out_shape=jax.ShapeDtypeStruct((batch * seq, D_ATTN), BF16),
        scratch_shapes=[
            pltpu.VMEM((N_HEADS, 2 * tq, 1), F32),
            pltpu.VMEM((N_HEADS, 2 * tq, 1), F32),
            pltpu.VMEM((N_HEADS, 2 * tq, V_DIM), F32),
        ],
        compiler_params=_cparams(("parallel", "parallel", "arbitrary")),
        name="attn_prompt",
    )(*lam_vecs, g_subln, q16, kt, v16)


def _attn_paged_kernel(pt_ref, lq1_ref, lk1_ref, lq2_ref, lk2_ref, gsub_ref, q_ref, ktn_ref, vin_ref,
                       *rest, pps, n_steps, past_len, dec_seq, lam_init):
    ktp_refs = rest[:pps]
    vip_refs = rest[pps:2 * pps]
    o_ref = rest[2 * pps]
    q2_scr, qbd_scr, m_scr, l_scr, acc_scr = rest[2 * pps + 1:]
    b = pl.program_id(0)
    j = pl.program_id(1)
    rph = 2 * dec_seq
    n_rows = N_HEADS * rph

    row = lax.broadcasted_iota(jnp.int32, (n_rows, 1), 0)
    head = row // rph
    comp = (row % rph) // dec_seq
    tok = row % dec_seq
    slope = jnp.zeros((n_rows, 1), F32)
    for h in range(N_HEADS):
        slope = jnp.where(head == h, ALIBI_SLOPES[h], slope)

    @pl.when(j == 0)
    def _():
        q = q_ref[...]
        q2_scr[0:dec_seq, :] = q
        q2_scr[dec_seq:rph, :] = q
        qrep = jnp.concatenate([q2_scr[...]] * N_HEADS, axis=0)
        col = lax.broadcasted_iota(jnp.int32, (n_rows, D_ATTN), 1)
        qbd_scr[...] = jnp.where(col // HEAD_DIM == head * 2 + comp, qrep, 0.0).astype(BF16)
        m_scr[...] = jnp.full(m_scr.shape, NEG_BIG, F32)
        l_scr[...] = jnp.zeros(l_scr.shape, F32)
        acc_scr[...] = jnp.zeros(acc_scr.shape, F32)

    qbd = qbd_scr[...]

    def online_update(s, v_refs, n_slots):
        m_prev = m_scr[...]
        m_new = jnp.maximum(m_prev, jnp.max(s, axis=-1, keepdims=True))
        alpha = jnp.exp(m_prev - m_new)
        p = jnp.exp(s - m_new)
        l_scr[...] = alpha * l_scr[...] + jnp.sum(p, axis=-1, keepdims=True)
        p16 = p.astype(BF16)
        for h in range(N_HEADS):
            rs = slice(h * rph, (h + 1) * rph)
            vh = jnp.concatenate([r[pl.ds(h, n_slots, stride=N_HEADS), :] for r in v_refs], axis=0)
            pv = jnp.dot(p16[rs, :], vh.astype(BF16), preferred_element_type=F32)
            acc_scr[rs, :] = alpha[rs] * acc_scr[rs, :] + pv
        m_scr[...] = m_new

    s = jnp.concatenate(
        [jnp.dot(qbd, ktp_refs[i][...].astype(BF16), preferred_element_type=F32) for i in range(pps)], axis=-1)
    kcol = lax.broadcasted_iota(jnp.int32, (1, pps * PAGE_SIZE), 1)
    koff = (kcol + (j * (pps * PAGE_SIZE) - past_len)).astype(F32)
    online_update(s + slope * koff, vip_refs, PAGE_SIZE)

    @pl.when(j == n_steps - 1)
    def _():
        n_new = ktn_ref.shape[1]
        sn = jnp.dot(qbd, ktn_ref[...].astype(BF16), preferred_element_type=F32)
        c = lax.broadcasted_iota(jnp.int32, (n_rows, n_new), 1)
        ctok = c % dec_seq
        valid = jnp.logical_and(c // dec_seq == b, ctok <= tok)
        sn = jnp.where(valid, sn + slope * ctok.astype(F32), NEG_BIG)
        online_update(sn, [vin_ref], n_new)

        lam = _diff_lambda(lq1_ref[...], lk1_ref[...], lq2_ref[...], lk2_ref[...], lam_init)
        o = acc_scr[...] / l_scr[...]
        for h in range(N_HEADS):
            oh = o[h * rph:h * rph + dec_seq] - lam * o[h * rph + dec_seq:(h + 1) * rph]
            o_ref[:, h * V_DIM:(h + 1) * V_DIM] = _rms(oh, gsub_ref[...]) * (1.0 - lam_init)


def _attn_paged(q_s, kt_new, vi_new, kt_pool, vi_pool, page_table, lam_vecs, g_subln, dec_batch, dec_seq,
                lam_init, pps=8):
    n_pages = page_table.shape[1]
    n_steps = n_pages // pps
    past_len = n_pages * PAGE_SIZE
    rph = 2 * dec_seq
    n_rows = N_HEADS * rph
    n_new = dec_batch * dec_seq
    q3 = q_s.astype(F32).reshape(dec_batch, dec_seq, D_ATTN)
    kern = functools.partial(_attn_paged_kernel, pps=pps, n_steps=n_steps, past_len=past_len,
                             dec_seq=dec_seq, lam_init=lam_init)
    vec = lambda n: pl.BlockSpec((1, n), lambda b, j, pt: (0, 0))

    def kpage_spec(i):
        return pl.BlockSpec((None, D_ATTN, PAGE_SIZE), lambda b, j, pt: (pt[b, j * pps + i], 0, 0))

    def vpage_spec(i):
        return pl.BlockSpec((None, PAGE_SIZE * N_HEADS, V_DIM), lambda b, j, pt: (pt[b, j * pps + i], 0, 0))

    grid_spec = pltpu.PrefetchScalarGridSpec(
        num_scalar_prefetch=1,
        grid=(dec_batch, n_steps),
        in_specs=[vec(HEAD_DIM)] * 4 + [
            vec(V_DIM),
            pl.BlockSpec((None, dec_seq, D_ATTN), lambda b, j, pt: (b, 0, 0)),
            pl.BlockSpec((D_ATTN, n_new), lambda b, j, pt: (0, 0)),
            pl.BlockSpec((n_new * N_HEADS, V_DIM), lambda b, j, pt: (0, 0)),
        ] + [kpage_spec(i) for i in range(pps)] + [vpage_spec(i) for i in range(pps)],
        out_specs=pl.BlockSpec((None, dec_seq, D_ATTN), lambda b, j, pt: (b, 0, 0)),
        scratch_shapes=[
            pltpu.VMEM((rph, D_ATTN), F32),
            pltpu.VMEM((n_rows, D_ATTN), BF16),
            pltpu.VMEM((n_rows, 1), F32),
            pltpu.VMEM((n_rows, 1), F32),
            pltpu.VMEM((n_rows, V_DIM), F32),
        ],
    )
    out = pl.pallas_call(
        kern,
        grid_spec=grid_spec,
        out_shape=jax.ShapeDtypeStruct((dec_batch, dec_seq, D_ATTN), F32),
        compiler_params=_cparams(("parallel", "arbitrary")),
        name="attn_paged",
    )(page_table, *lam_vecs, g_subln, q3, kt_new, vi_new, *([kt_pool] * pps), *([vi_pool] * pps))
    return out.reshape(dec_batch * dec_seq, D_ATTN)


CONV_HALO = 32
CONV_ROW_CHUNK = 16


def _ln_swish(c, g, b):
    mu = jnp.mean(c, axis=-1, keepdims=True)
    d = c - mu
    var = jnp.mean(d * d, axis=-1, keepdims=True)
    y = d * lax.rsqrt(var + EPS) * g + b
    return y * jax.nn.sigmoid(y)


def _conv_prompt_kernel(prev_ref, cur_ref, w_ref, b_ref, g_ref, beta_ref, o_ref, ext_scr, *, tm):
    i = pl.program_id(1)
    ext_scr[0:CONV_HALO, :] = jnp.where(i == 0, 0.0, prev_ref[...])
    ext_scr[CONV_HALO:CONV_HALO + tm, :] = cur_ref[...]
    first = CONV_HALO - (CONV_WIDTH - 1)
    for r in range(0, tm, CONV_ROW_CHUNK):
        acc = jnp.broadcast_to(b_ref[...], (CONV_ROW_CHUNK, D_CONV))
        for j in range(CONV_WIDTH):
            acc = acc + w_ref[j:j + 1, :] * ext_scr[r + first + j:r + first + j + CONV_ROW_CHUNK, :]
        o_ref[r:r + CONV_ROW_CHUNK, :] = _ln_swish(acc, g_ref[...], beta_ref[...]).astype(o_ref.dtype)


def _conv_prompt(u2d, w_dw, b_dw, ln_g, ln_b, batch, seq, tm=128):
    nt = seq // tm
    halo_per_tile = tm // CONV_HALO
    kern = functools.partial(_conv_prompt_kernel, tm=tm)
    vec = pl.BlockSpec((1, D_CONV), lambda b, i: (0, 0))
    return pl.pallas_call(
        kern,
        grid=(batch, nt),
        in_specs=[
            pl.BlockSpec((CONV_HALO, D_CONV),
                         lambda b, i: (jnp.maximum((b * nt + i) * halo_per_tile - 1, 0), 0)),
            pl.BlockSpec((tm, D_CONV), lambda b, i: (b * nt + i, 0)),
            pl.BlockSpec((CONV_WIDTH, D_CONV), lambda b, i: (0, 0)),
            vec, vec, vec,
        ],
        out_specs=pl.BlockSpec((tm, D_CONV), lambda b, i: (b * nt + i, 0)),
        out_shape=jax.ShapeDtypeStruct((batch * seq, D_CONV), BF16),
        scratch_shapes=[pltpu.VMEM((CONV_HALO + tm, D_CONV), F32)],
        compiler_params=_cparams(("parallel", "parallel")),
        name="conv_prompt",
    )(u2d, u2d, w_dw, b_dw, ln_g, ln_b)


def _conv_sample_kernel(state_ref, u_ref, w_ref, b_ref, g_ref, beta_ref, o_ref, st_ref, *, dec_batch, dec_seq):
    n_state = CONV_WIDTH - 1

    def ext(i):
        return state_ref[i] if i < n_state else u_ref[i - n_state]

    for t in range(dec_seq):
        acc = jnp.broadcast_to(b_ref[...], (dec_batch, D_CONV))
        for j in range(CONV_WIDTH):
            acc = acc + w_ref[j:j + 1, :] * ext(t + j)
        o_ref[t] = _ln_swish(acc, g_ref[...], beta_ref[...])
    for i in range(n_state):
        st_ref[i] = ext(i + dec_seq)


def _conv_sample(state_tm, u_s, w_dw, b_dw, ln_g, ln_b, dec_batch, dec_seq):
    kern = functools.partial(_conv_sample_kernel, dec_batch=dec_batch, dec_seq=dec_seq)
    u_tm = jnp.transpose(u_s.reshape(dec_batch, dec_seq, D_CONV), (1, 0, 2))
    y_tm, st_tm = pl.pallas_call(
        kern,
        out_shape=[
            jax.ShapeDtypeStruct((dec_seq, dec_batch, D_CONV), F32),
            jax.ShapeDtypeStruct(state_tm.shape, F32),
        ],
        compiler_params=pltpu.CompilerParams(vmem_limit_bytes=VMEM_LIMIT_BYTES),
        name="conv_sample",
    )(state_tm, u_tm, w_dw, b_dw, ln_g, ln_b)
    return jnp.transpose(y_tm, (1, 0, 2)).reshape(dec_batch * dec_seq, D_CONV), st_tm


def _outproj_kernel(ya_ref, yc_ref, x_ref, w_ref, gpost_ref, gpre_ref, h_ref, hn_ref):
    m = (jnp.dot(ya_ref[...].astype(BF16), w_ref[0:D_ATTN, :], preferred_element_type=F32)
         + jnp.dot(yc_ref[...].astype(BF16), w_ref[D_ATTN:, :], preferred_element_type=F32))
    h = x_ref[...] + _rms(m, gpost_ref[...])
    h_ref[...] = h
    hn_ref[...] = _rms(h, gpre_ref[...]).astype(BF16)


def _outproj(y_attn, y_conv, x2d, w_out16, g_post_mix, g_pre_ffn, tm):
    m = x2d.shape[0]
    vec = pl.BlockSpec((1, D_MODEL), lambda i: (0, 0))
    return pl.pallas_call(
        _outproj_kernel,
        grid=(m // tm,),
        in_specs=[
            pl.BlockSpec((tm, D_ATTN), lambda i: (i, 0)),
            pl.BlockSpec((tm, D_CONV), lambda i: (i, 0)),
            pl.BlockSpec((tm, D_MODEL), lambda i: (i, 0)),
            pl.BlockSpec((D_MODEL, D_MODEL), lambda i: (0, 0)),
            vec, vec,
        ],
        out_specs=[
            pl.BlockSpec((tm, D_MODEL), lambda i: (i, 0)),
            pl.BlockSpec((tm, D_MODEL), lambda i: (i, 0)),
        ],
        out_shape=[
            jax.ShapeDtypeStruct((m, D_MODEL), F32),
            jax.ShapeDtypeStruct((m, D_MODEL), BF16),
        ],
        compiler_params=_cparams(("parallel",)),
        name="outproj",
    )(y_attn, y_conv, x2d, w_out16, g_post_mix, g_pre_ffn)


def _ffn_kernel(hn_ref, h_ref, pe_ref, w1_ref, w2_ref, gpost_ref, wg_ref, wp_ref, o_ref, acc_scr):
    k = pl.program_id(1)
    a = jnp.dot(hn_ref[...], w1_ref[...], preferred_element_type=F32)
    a = jnp.square(jnp.maximum(a, 0.0)).astype(BF16)
    contrib = jnp.dot(a, w2_ref[...], preferred_element_type=F32)

    @pl.when(k == 0)
    def _():
        acc_scr[...] = contrib

    @pl.when(k > 0)
    def _():
        acc_scr[...] += contrib

    @pl.when(k == pl.num_programs(1) - 1)
    def _():
        h2 = h_ref[...] + _rms(acc_scr[...], gpost_ref[...])
        gate = jax.nn.sigmoid(jnp.dot(h2.astype(BF16), wg_ref[...], preferred_element_type=F32))
        pev = jnp.dot(pe_ref[...].astype(BF16), wp_ref[...], preferred_element_type=F32)
        o_ref[...] = h2 + gate * pev


def _ffn(hn, h, pe2d, w_ff1_16, w_ff2_16, g_post_ffn, w_gate16, w_ple16, tm, tf=512):
    m = h.shape[0]
    return pl.pallas_call(
        _ffn_kernel,
        grid=(m // tm, D_FF // tf),
        in_specs=[
            pl.BlockSpec((tm, D_MODEL), lambda i, k: (i, 0)),
            pl.BlockSpec((tm, D_MODEL), lambda i, k: (i, 0)),
            pl.BlockSpec((tm, D_PLE), lambda i, k: (i, 0)),
            pl.BlockSpec((D_MODEL, tf), lambda i, k: (0, k)),
            pl.BlockSpec((tf, D_MODEL), lambda i, k: (k, 0)),
            pl.BlockSpec((1, D_MODEL), lambda i, k: (0, 0)),
            pl.BlockSpec((D_MODEL, D_MODEL), lambda i, k: (0, 0)),
            pl.BlockSpec((D_PLE, D_MODEL), lambda i, k: (0, 0)),
        ],
        out_specs=pl.BlockSpec((tm, D_MODEL), lambda i, k: (i, 0)),
        out_shape=jax.ShapeDtypeStruct((m, D_MODEL), F32),
        scratch_shapes=[pltpu.VMEM((tm, D_MODEL), F32)],
        compiler_params=_cparams(("parallel", "arbitrary")),
        name="ffn",
    )(hn, h, pe2d, w_ff1_16, w_ff2_16, g_post_ffn, w_gate16, w_ple16)


def _row(v):
    return v.reshape(1, -1)


def kernel(x_prompt, x_sample, cache_k, cache_v, state_conv, page_table, p_prompt, p_sample,
           w_in, w_out, lambda_q1, lambda_k1, lambda_q2, lambda_k2, g_subln, w_dw, b_dw,
           ln_conv_g, ln_conv_b, g_pre_mix, g_post_mix, g_pre_ffn, g_post_ffn,
           w_ff1, w_ff2, w_ple, w_ple_gate):
    depth = w_in.shape[0]
    batch, seq, _ = x_prompt.shape
    dec_batch, dec_seq, _ = x_sample.shape
    n_pool = cache_k.shape[1]
    mp, ms = batch * seq, dec_batch * dec_seq
    assert 2 * dec_seq == 8 and cache_k.shape[2] == PAGE_SIZE

    hp = x_prompt.reshape(mp, D_MODEL)
    hs = x_sample.reshape(ms, D_MODEL)
    outs = {k: [] for k in ("kp", "vp", "cp", "ks", "vs", "cs")}
    for l in range(depth):
        lam_init = 0.8 - 0.6 * math.exp(-0.3 * l)
        w_in16 = w_in[l].astype(BF16)
        w_kt16 = w_in[l][:, D_ATTN:2 * D_ATTN].T.astype(BF16)
        w_out16 = w_out[l].astype(BF16)
        w_ff1_16 = w_ff1[l].astype(BF16)
        w_ff2_16 = w_ff2[l].astype(BF16)
        w_gate16 = w_ple_gate[l].astype(BF16)
        w_ple16 = w_ple[l].astype(BF16)
        lam_vecs = [_row(lambda_q1[l]), _row(lambda_k1[l]), _row(lambda_q2[l]), _row(lambda_k2[l])]
        gsub = _row(g_subln[l])
        conv_args = (w_dw[l], _row(b_dw[l]), _row(ln_conv_g[l]), _row(ln_conv_b[l]))
        kt_pool = jnp.transpose(cache_k[l], (0, 2, 3, 4, 1)).reshape(n_pool, D_ATTN, PAGE_SIZE)
        vi_pool = cache_v[l].reshape(n_pool, PAGE_SIZE * N_HEADS, V_DIM)

        q_p, kt_p, vi_p, v16_p, u_p = _inproj(hp, _row(g_pre_mix[l]), w_in16, w_kt16, n_seq=batch, tm=512)
        ya_p = _attn_prompt(q_p, kt_p, v16_p, lam_vecs, gsub, batch, seq, lam_init)
        yc_p = _conv_prompt(u_p, *conv_args, batch, seq)
        h_p, hn_p = _outproj(ya_p, yc_p, hp, w_out16, _row(g_post_mix[l]), _row(g_pre_ffn[l]), tm=512)
        hp = _ffn(hn_p, h_p, p_prompt[l].reshape(mp, D_PLE), w_ff1_16, w_ff2_16, _row(g_post_ffn[l]),
                  w_gate16, w_ple16, tm=512)
        outs["kp"].append(jnp.transpose(kt_p.reshape(batch, N_HEADS, 2, HEAD_DIM, seq), (0, 4, 1, 2, 3)))
        outs["vp"].append(vi_p.reshape(batch, seq, N_HEADS, V_DIM))
        outs["cp"].append(u_p.reshape(batch, seq, D_CONV)[:, seq - (CONV_WIDTH - 1):])

        q_s, kt_s, vi_s, _, u_s = _inproj(hs, _row(g_pre_mix[l]), w_in16, w_kt16, n_seq=1, tm=ms)
        ya_s = _attn_paged(q_s, kt_s[0], vi_s, kt_pool, vi_pool, page_table, lam_vecs, gsub,
                           dec_batch, dec_seq, lam_init)
        yc_s, st_tm = _conv_sample(jnp.transpose(state_conv[l], (1, 0, 2)), u_s, *conv_args,
                                   dec_batch, dec_seq)
        h_s, hn_s = _outproj(ya_s, yc_s, hs, w_out16, _row(g_post_mix[l]), _row(g_pre_ffn[l]), tm=ms)
        hs = _ffn(hn_s, h_s, p_sample[l].reshape(ms, D_PLE), w_ff1_16, w_ff2_16, _row(g_post_ffn[l]),
                  w_gate16, w_ple16, tm=ms)
        outs["ks"].append(jnp.transpose(kt_s.reshape(N_HEADS, 2, HEAD_DIM, dec_batch, dec_seq),
                                        (3, 4, 0, 1, 2)))
        outs["vs"].append(vi_s.reshape(dec_batch, dec_seq, N_HEADS, V_DIM))
        outs["cs"].append(jnp.transpose(st_tm, (1, 0, 2)))

    return (hp.reshape(batch, seq, D_MODEL), hs.reshape(dec_batch, dec_seq, D_MODEL),
            jnp.stack(outs["kp"]), jnp.stack(outs["vp"]), jnp.stack(outs["cp"]),
            jnp.stack(outs["ks"]), jnp.stack(outs["vs"]), jnp.stack(outs["cs"]))
```

```python
import functools
import math

import jax
import jax.numpy as jnp
from jax import lax
from jax.experimental import pallas as pl
from jax.experimental.pallas import tpu as pltpu

D_MODEL = 1024
D_ATTN = 512
D_CONV = 512
N_HEADS = 4
HEAD_DIM = 64
V_DIM = 128
CONV_WIDTH = 31
D_FF = 4096
D_PLE = 256
PAGE_SIZE = 128
EPS = 1e-6
N_QKV = 3 * D_ATTN
N_IN = N_QKV + 2 * D_CONV
NEG_BIG = -1e30
ALIBI_SLOPES = tuple(2.0 ** (-8.0 * (h + 1) / N_HEADS) for h in range(N_HEADS))

F32 = jnp.float32
BF16 = jnp.bfloat16

VMEM_LIMIT_BYTES = 56 * 1024 * 1024


def _cparams(semantics):
    return pltpu.CompilerParams(dimension_semantics=semantics, vmem_limit_bytes=VMEM_LIMIT_BYTES)


def _rms(x, g):
    return x * lax.rsqrt(jnp.mean(x * x, axis=-1, keepdims=True) + EPS) * g


def _diff_lambda(lq1, lk1, lq2, lk2, lam_init):
    return (jnp.exp(jnp.sum(lq1 * lk1, axis=-1, keepdims=True))
            - jnp.exp(jnp.sum(lq2 * lk2, axis=-1, keepdims=True)) + lam_init)


def _nt_dot(a, b):
    return lax.dot_general(a, b, (((1,), (1,)), ((), ())), preferred_element_type=F32)


def _inproj_kernel(x_ref, g_ref, w_ref, wkt_ref, q_ref, kt_ref, vi_ref, v16_ref, u_ref, *, tm):
    xn = _rms(x_ref[...], g_ref[...]).astype(BF16)
    q = jnp.dot(xn, w_ref[:, 0:D_ATTN], preferred_element_type=F32)
    q_ref[...] = (q * (HEAD_DIM ** -0.5)).astype(BF16)
    kt_ref[...] = _nt_dot(wkt_ref[...], xn)
    v = jnp.dot(xn, w_ref[:, 2 * D_ATTN:N_QKV], preferred_element_type=F32)
    v16_ref[...] = v.astype(BF16)
    for h in range(N_HEADS):
        vi_ref[pl.ds(h, tm, stride=N_HEADS), :] = v[:, h * V_DIM:(h + 1) * V_DIM]
    a = jnp.dot(xn, w_ref[:, N_QKV:N_QKV + D_CONV], preferred_element_type=F32)
    gt = jnp.dot(xn, w_ref[:, N_QKV + D_CONV:], preferred_element_type=F32)
    u_ref[...] = a * jax.nn.sigmoid(gt)


def _inproj(x2d, g_pre_mix, w_in16, w_kt16, n_seq, tm):
    m = x2d.shape[0]
    seq = m // n_seq
    nt = seq // tm
    return pl.pallas_call(
        functools.partial(_inproj_kernel, tm=tm),
        grid=(m // tm,),
        in_specs=[
            pl.BlockSpec((tm, D_MODEL), lambda i: (i, 0)),
            pl.BlockSpec((1, D_MODEL), lambda i: (0, 0)),
            pl.BlockSpec((D_MODEL, N_IN), lambda i: (0, 0)),
            pl.BlockSpec((D_ATTN, D_MODEL), lambda i: (0, 0)),
        ],
        out_specs=[
            pl.BlockSpec((tm, D_ATTN), lambda i: (i, 0)),
            pl.BlockSpec((None, D_ATTN, tm), lambda i: (i // nt, 0, i % nt)),
            pl.BlockSpec((tm * N_HEADS, V_DIM), lambda i: (i, 0)),
            pl.BlockSpec((tm, D_ATTN), lambda i: (i, 0)),
            pl.BlockSpec((tm, D_CONV), lambda i: (i, 0)),
        ],
        out_shape=[
            jax.ShapeDtypeStruct((m, D_ATTN), BF16),
            jax.ShapeDtypeStruct((n_seq, D_ATTN, seq), F32),
            jax.ShapeDtypeStruct((m * N_HEADS, V_DIM), F32),
            jax.ShapeDtypeStruct((m, D_ATTN), BF16),
            jax.ShapeDtypeStruct((m, D_CONV), F32),
        ],
        compiler_params=_cparams(("parallel",)),
        name="inproj",
    )(x2d, g_pre_mix, w_in16, w_kt16)


ATTN_TILE = 256
ATTN_ROW_BLOCK = 64
LANES = 128


def _attn_prompt_kernel(lq1_ref, lk1_ref, lq2_ref, lk2_ref, gsub_ref, q_ref, kt_ref, v_ref, o_ref,
                        qs_scr, m_scr, l_scr, acc_scr, *, lam_init):
    t, rb = ATTN_TILE, ATTN_ROW_BLOCK
    n_rb = t // rb
    qi = pl.program_id(1)
    lane = lax.broadcasted_iota(jnp.int32, (rb, 2 * HEAD_DIM), 1)
    kcol = lax.broadcasted_iota(jnp.int32, (1, t), 1)

    for h in range(N_HEADS):
        hs = slice(h * V_DIM, (h + 1) * V_DIM)
        for g in range(n_rb):
            q = q_ref[g * rb:(g + 1) * rb, hs]
            zero = jnp.zeros_like(q)
            qs_scr[h, g] = jnp.concatenate([jnp.where(lane < HEAD_DIM, q, zero),
                                            jnp.where(lane >= HEAD_DIM, q, zero)], axis=0)
    m_scr[...] = jnp.full(m_scr.shape, NEG_BIG, F32)
    l_scr[...] = jnp.zeros(l_scr.shape, F32)
    acc_scr[...] = jnp.zeros(acc_scr.shape, F32)

    def tile(kv, diagonal):
        k0 = pl.multiple_of(kv * t, t)
        koff = (kcol + (kv - qi) * t).astype(F32)
        for h in range(N_HEADS):
            hs = slice(h * V_DIM, (h + 1) * V_DIM)
            kt = kt_ref[hs, pl.ds(k0, t)].astype(BF16)
            vv = v_ref[pl.ds(k0, t), hs]
            bias = ALIBI_SLOPES[h] * koff
            for g in range(n_rb):
                s = jnp.dot(qs_scr[h, g], kt, preferred_element_type=F32) + bias
                if diagonal:
                    row = lax.broadcasted_iota(jnp.int32, (2 * rb, t), 0)
                    row = jnp.where(row >= rb, row - rb, row) + g * rb
                    col = lax.broadcasted_iota(jnp.int32, (2 * rb, t), 1)
                    s = jnp.where(col <= row, s, NEG_BIG)
                m_prev = m_scr[h, g]
                m_new = jnp.maximum(m_prev, jnp.max(s, axis=-1, keepdims=True))
                alpha = jnp.exp(m_prev - m_new)
                p = jnp.exp(s - jnp.concatenate([m_new] * (t // LANES), axis=-1))
                l_scr[h, g] = alpha * l_scr[h, g] + jnp.sum(p, axis=-1, keepdims=True)
                acc_scr[h, g] = alpha * acc_scr[h, g] + jnp.dot(p.astype(BF16), vv,
                                                               preferred_element_type=F32)
                m_scr[h, g] = m_new

    def below_diagonal(kv, carry):
        tile(kv, False)
        return carry

    lax.fori_loop(0, qi, below_diagonal, 0)
    tile(qi, True)

    lam = _diff_lambda(lq1_ref[...], lk1_ref[...], lq2_ref[...], lk2_ref[...], lam_init)
    for h in range(N_HEADS):
        for g in range(n_rb):
            o = acc_scr[h, g] / l_scr[h, g]
            oh = o[:rb] - lam * o[rb:]
            y = _rms(oh, gsub_ref[...]) * (1.0 - lam_init)
            o_ref[g * rb:(g + 1) * rb, h * V_DIM:(h + 1) * V_DIM] = y.astype(o_ref.dtype)


def _attn_prompt(q16, kt, v16, lam_vecs, g_subln, batch, seq, lam_init):
    t, rb = ATTN_TILE, ATTN_ROW_BLOCK
    nq = seq // t
    n_rb = t // rb
    kern = functools.partial(_attn_prompt_kernel, lam_init=lam_init)
    vec = lambda n: pl.BlockSpec((1, n), lambda b, qi: (0, 0))
    return pl.pallas_call(
        kern,
        grid=(batch, nq),
        in_specs=[vec(HEAD_DIM)] * 4 + [
            vec(V_DIM),
            pl.BlockSpec((t, D_ATTN), lambda b, qi: (b * nq + qi, 0)),
            pl.BlockSpec((None, D_ATTN, seq), lambda b, qi: (b, 0, 0)),
            pl.BlockSpec((seq, D_ATTN), lambda b, qi: (b, 0)),
        ],
        out_specs=pl.BlockSpec((t, D_ATTN), lambda b, qi: (b * nq + qi, 0)),
        out_shape=jax.ShapeDtypeStruct((batch * seq, D_ATTN), BF16),
        scratch_shapes=[
            pltpu.VMEM((N_HEADS, n_rb, 2 * rb, 2 * HEAD_DIM), BF16),
            pltpu.VMEM((N_HEADS, n_rb, 2 * rb, LANES), F32),
            pltpu.VMEM((N_HEADS, n_rb, 2 * rb, LANES), F32),
            pltpu.VMEM((N_HEADS, n_rb, 2 * rb, V_DIM), F32),
        ],
        compiler_params=_cparams(("parallel", "arbitrary")),
        name="attn_prompt",
    )(*lam_vecs, g_subln, q16, kt, v16)


def _attn_paged_kernel(pt_ref, lq1_ref, lk1_ref, lq2_ref, lk2_ref, gsub_ref, q_ref, ktn_ref, vin_ref,
                       *rest, pps, n_steps, past_len, dec_seq, lam_init):
    ktp_refs = rest[:pps]
    vip_refs = rest[pps:2 * pps]
    o_ref = rest[2 * pps]
    q2_scr, qbd_scr, m_scr, l_scr, acc_scr = rest[2 * pps + 1:]
    b = pl.program_id(0)
    j = pl.program_id(1)
    rph = 2 * dec_seq
    n_rows = N_HEADS * rph

    row = lax.broadcasted_iota(jnp.int32, (n_rows, 1), 0)
    head = row // rph
    comp = (row % rph) // dec_seq
    tok = row % dec_seq
    slope = jnp.zeros((n_rows, 1), F32)
    for h in range(N_HEADS):
        slope = jnp.where(head == h, ALIBI_SLOPES[h], slope)

    @pl.when(j == 0)
    def _():
        q = q_ref[...]
        q2_scr[0:dec_seq, :] = q
        q2_scr[dec_seq:rph, :] = q
        qrep = jnp.concatenate([q2_scr[...]] * N_HEADS, axis=0)
        col = lax.broadcasted_iota(jnp.int32, (n_rows, D_ATTN), 1)
        qbd_scr[...] = jnp.where(col // HEAD_DIM == head * 2 + comp, qrep, 0.0).astype(BF16)
        m_scr[...] = jnp.full(m_scr.shape, NEG_BIG, F32)
        l_scr[...] = jnp.zeros(l_scr.shape, F32)
        acc_scr[...] = jnp.zeros(acc_scr.shape, F32)

    qbd = qbd_scr[...]

    def online_update(s, v_refs, n_slots):
        m_prev = m_scr[...]
        m_new = jnp.maximum(m_prev, jnp.max(s, axis=-1, keepdims=True))
        alpha = jnp.exp(m_prev - m_new)
        p = jnp.exp(s - m_new)
        l_scr[...] = alpha * l_scr[...] + jnp.sum(p, axis=-1, keepdims=True)
        p16 = p.astype(BF16)
        for h in range(N_HEADS):
            rs = slice(h * rph, (h + 1) * rph)
            vh = jnp.concatenate([r[pl.ds(h, n_slots, stride=N_HEADS), :] for r in v_refs], axis=0)
            pv = jnp.dot(p16[rs, :], vh.astype(BF16), preferred_element_type=F32)
            acc_scr[rs, :] = alpha[rs] * acc_scr[rs, :] + pv
        m_scr[...] = m_new

    s = jnp.concatenate(
        [jnp.dot(qbd, ktp_refs[i][...].astype(BF16), preferred_element_type=F32) for i in range(pps)], axis=-1)
    kcol = lax.broadcasted_iota(jnp.int32, (1, pps * PAGE_SIZE), 1)
    koff = (kcol + (j * (pps * PAGE_SIZE) - past_len)).astype(F32)
    online_update(s + slope * koff, vip_refs, PAGE_SIZE)

    @pl.when(j == n_steps - 1)
    def _():
        n_new = ktn_ref.shape[1]
        sn = jnp.dot(qbd, ktn_ref[...].astype(BF16), preferred_element_type=F32)
        c = lax.broadcasted_iota(jnp.int32, (n_rows, n_new), 1)
        ctok = c % dec_seq
        valid = jnp.logical_and(c // dec_seq == b, ctok <= tok)
        sn = jnp.where(valid, sn + slope * ctok.astype(F32), NEG_BIG)
        online_update(sn, [vin_ref], n_new)

        lam = _diff_lambda(lq1_ref[...], lk1_ref[...], lq2_ref[...], lk2_ref[...], lam_init)
        o = acc_scr[...] / l_scr[...]
        for h in range(N_HEADS):
            oh = o[h * rph:h * rph + dec_seq] - lam * o[h * rph + dec_seq:(h + 1) * rph]
            o_ref[:, h * V_DIM:(h + 1) * V_DIM] = _rms(oh, gsub_ref[...]) * (1.0 - lam_init)


def _attn_paged(q_s, kt_new, vi_new, kt_pool, vi_pool, page_table, lam_vecs, g_subln, dec_batch, dec_seq,
                lam_init, pps=8):
    n_pages = page_table.shape[1]
    n_steps = n_pages // pps
    past_len = n_pages * PAGE_SIZE
    rph = 2 * dec_seq
    n_rows = N_HEADS * rph
    n_new = dec_batch * dec_seq
    q3 = q_s.astype(F32).reshape(dec_batch, dec_seq, D_ATTN)
    kern = functools.partial(_attn_paged_kernel, pps=pps, n_steps=n_steps, past_len=past_len,
                             dec_seq=dec_seq, lam_init=lam_init)
    vec = lambda n: pl.BlockSpec((1, n), lambda b, j, pt: (0, 0))

    def kpage_spec(i):
        return pl.BlockSpec((None, D_ATTN, PAGE_SIZE), lambda b, j, pt: (pt[b, j * pps + i], 0, 0))

    def vpage_spec(i):
        return pl.BlockSpec((None, PAGE_SIZE * N_HEADS, V_DIM), lambda b, j, pt: (pt[b, j * pps + i], 0, 0))

    grid_spec = pltpu.PrefetchScalarGridSpec(
        num_scalar_prefetch=1,
        grid=(dec_batch, n_steps),
        in_specs=[vec(HEAD_DIM)] * 4 + [
            vec(V_DIM),
            pl.BlockSpec((None, dec_seq, D_ATTN), lambda b, j, pt: (b, 0, 0)),
            pl.BlockSpec((D_ATTN, n_new), lambda b, j, pt: (0, 0)),
            pl.BlockSpec((n_new * N_HEADS, V_DIM), lambda b, j, pt: (0, 0)),
        ] + [kpage_spec(i) for i in range(pps)] + [vpage_spec(i) for i in range(pps)],
        out_specs=pl.BlockSpec((None, dec_seq, D_ATTN), lambda b, j, pt: (b, 0, 0)),
        scratch_shapes=[
            pltpu.VMEM((rph, D_ATTN), F32),
            pltpu.VMEM((n_rows, D_ATTN), BF16),
            pltpu.VMEM((n_rows, 1), F32),
            pltpu.VMEM((n_rows, 1), F32),
            pltpu.VMEM((n_rows, V_DIM), F32),
        ],
    )
    out = pl.pallas_call(
        kern,
        grid_spec=grid_spec,
        out_shape=jax.ShapeDtypeStruct((dec_batch, dec_seq, D_ATTN), F32),
        compiler_params=_cparams(("parallel", "arbitrary")),
        name="attn_paged",
    )(page_table, *lam_vecs, g_subln, q3, kt_new, vi_new, *([kt_pool] * pps), *([vi_pool] * pps))
    return out.reshape(dec_batch * dec_seq, D_ATTN)


SUBLANES = 8
CONV_HALO = 32
CONV_ROW_CHUNK = 16


def _ln_swish(c, g, b):
    mu = jnp.mean(c, axis=-1, keepdims=True)
    d = c - mu
    var = jnp.mean(d * d, axis=-1, keepdims=True)
    y = d * lax.rsqrt(var + EPS) * g + b
    return y * jax.nn.sigmoid(y)


def _conv_prompt_kernel(prev_ref, cur_ref, w_ref, b_ref, g_ref, beta_ref, o_ref, ext_scr, sh_scr, *, tm):
    i = pl.program_id(1)
    n_ext = CONV_HALO + tm
    ext_scr[0:CONV_HALO, :] = jnp.where(i == 0, 0.0, prev_ref[...])
    ext_scr[CONV_HALO:n_ext, :] = cur_ref[...]
    ext = ext_scr[...]
    for r in range(1, SUBLANES):
        sh_scr[r] = pltpu.roll(ext, n_ext - r, axis=0)
    first = CONV_HALO - (CONV_WIDTH - 1)
    for c in range(0, tm, CONV_ROW_CHUNK):
        acc = jnp.broadcast_to(b_ref[...], (CONV_ROW_CHUNK, D_CONV))
        for j in range(CONV_WIDTH):
            r = (first + j) % SUBLANES
            a = c + first + j - r
            src = ext_scr[a:a + CONV_ROW_CHUNK, :] if r == 0 else sh_scr[r, a:a + CONV_ROW_CHUNK, :]
            acc = acc + jnp.concatenate([w_ref[j]] * (CONV_ROW_CHUNK // SUBLANES), axis=0) * src
        o_ref[c:c + CONV_ROW_CHUNK, :] = _ln_swish(acc, g_ref[...], beta_ref[...]).astype(o_ref.dtype)


def _conv_prompt(u2d, w_dw, b_dw, ln_g, ln_b, batch, seq, tm=128):
    nt = seq // tm
    halo_per_tile = tm // CONV_HALO
    kern = functools.partial(_conv_prompt_kernel, tm=tm)
    vec = pl.BlockSpec((1, D_CONV), lambda b, i: (0, 0))
    w_rep = jnp.broadcast_to(w_dw[:, None, :], (CONV_WIDTH, SUBLANES, D_CONV))
    return pl.pallas_call(
        kern,
        grid=(batch, nt),
        in_specs=[
            pl.BlockSpec((CONV_HALO, D_CONV),
                         lambda b, i: (jnp.maximum((b * nt + i) * halo_per_tile - 1, 0), 0)),
            pl.BlockSpec((tm, D_CONV), lambda b, i: (b * nt + i, 0)),
            pl.BlockSpec((CONV_WIDTH, SUBLANES, D_CONV), lambda b, i: (0, 0, 0)),
            vec, vec, vec,
        ],
        out_specs=pl.BlockSpec((tm, D_CONV), lambda b, i: (b * nt + i, 0)),
        out_shape=jax.ShapeDtypeStruct((batch * seq, D_CONV), BF16),
        scratch_shapes=[pltpu.VMEM((CONV_HALO + tm, D_CONV), F32),
                        pltpu.VMEM((SUBLANES, CONV_HALO + tm, D_CONV), F32)],
        compiler_params=_cparams(("parallel", "parallel")),
        name="conv_prompt",
    )(u2d, u2d, w_rep, b_dw, ln_g, ln_b)


def _conv_sample_kernel(state_ref, u_ref, w_ref, b_ref, g_ref, beta_ref, o_ref, st_ref, *, dec_batch, dec_seq):
    n_state = CONV_WIDTH - 1

    def ext(i):
        return state_ref[i] if i < n_state else u_ref[i - n_state]

    for t in range(dec_seq):
        acc = jnp.broadcast_to(b_ref[...], (dec_batch, D_CONV))
        for j in range(CONV_WIDTH):
            acc = acc + w_ref[j:j + 1, :] * ext(t + j)
        o_ref[t] = _ln_swish(acc, g_ref[...], beta_ref[...])
    for i in range(n_state):
        st_ref[i] = ext(i + dec_seq)


def _conv_sample(state_tm, u_s, w_dw, b_dw, ln_g, ln_b, dec_batch, dec_seq):
    kern = functools.partial(_conv_sample_kernel, dec_batch=dec_batch, dec_seq=dec_seq)
    u_tm = jnp.transpose(u_s.reshape(dec_batch, dec_seq, D_CONV), (1, 0, 2))
    y_tm, st_tm = pl.pallas_call(
        kern,
        out_shape=[
            jax.ShapeDtypeStruct((dec_seq, dec_batch, D_CONV), F32),
            jax.ShapeDtypeStruct(state_tm.shape, F32),
        ],
        compiler_params=pltpu.CompilerParams(vmem_limit_bytes=VMEM_LIMIT_BYTES),
        name="conv_sample",
    )(state_tm, u_tm, w_dw, b_dw, ln_g, ln_b)
    return jnp.transpose(y_tm, (1, 0, 2)).reshape(dec_batch * dec_seq, D_CONV), st_tm


def _outproj_kernel(ya_ref, yc_ref, x_ref, w_ref, gpost_ref, gpre_ref, h_ref, hn_ref):
    m = (jnp.dot(ya_ref[...].astype(BF16), w_ref[0:D_ATTN, :], preferred_element_type=F32)
         + jnp.dot(yc_ref[...].astype(BF16), w_ref[D_ATTN:, :], preferred_element_type=F32))
    h = x_ref[...] + _rms(m, gpost_ref[...])
    h_ref[...] = h
    hn_ref[...] = _rms(h, gpre_ref[...]).astype(BF16)


def _outproj(y_attn, y_conv, x2d, w_out16, g_post_mix, g_pre_ffn, tm):
    m = x2d.shape[0]
    vec = pl.BlockSpec((1, D_MODEL), lambda i: (0, 0))
    return pl.pallas_call(
        _outproj_kernel,
        grid=(m // tm,),
        in_specs=[
            pl.BlockSpec((tm, D_ATTN), lambda i: (i, 0)),
            pl.BlockSpec((tm, D_CONV), lambda i: (i, 0)),
            pl.BlockSpec((tm, D_MODEL), lambda i: (i, 0)),
            pl.BlockSpec((D_MODEL, D_MODEL), lambda i: (0, 0)),
            vec, vec,
        ],
        out_specs=[
            pl.BlockSpec((tm, D_MODEL), lambda i: (i, 0)),
            pl.BlockSpec((tm, D_MODEL), lambda i: (i, 0)),
        ],
        out_shape=[
            jax.ShapeDtypeStruct((m, D_MODEL), F32),
            jax.ShapeDtypeStruct((m, D_MODEL), BF16),
        ],
        compiler_params=_cparams(("parallel",)),
        name="outproj",
    )(y_attn, y_conv, x2d, w_out16, g_post_mix, g_pre_ffn)


def _ffn_kernel(hn_ref, h_ref, pe_ref, w1_ref, w2_ref, gpost_ref, wg_ref, wp_ref, o_ref, acc_scr):
    k = pl.program_id(1)
    a = jnp.dot(hn_ref[...], w1_ref[...], preferred_element_type=F32)
    a = jnp.square(jnp.maximum(a, 0.0)).astype(BF16)
    contrib = jnp.dot(a, w2_ref[...], preferred_element_type=F32)

    @pl.when(k == 0)
    def _():
        acc_scr[...] = contrib

    @pl.when(k > 0)
    def _():
        acc_scr[...] += contrib

    @pl.when(k == pl.num_programs(1) - 1)
    def _():
        h2 = h_ref[...] + _rms(acc_scr[...], gpost_ref[...])
        gate = jax.nn.sigmoid(jnp.dot(h2.astype(BF16), wg_ref[...], preferred_element_type=F32))
        pev = jnp.dot(pe_ref[...].astype(BF16), wp_ref[...], preferred_element_type=F32)
        o_ref[...] = h2 + gate * pev


def _ffn(hn, h, pe2d, w_ff1_16, w_ff2_16, g_post_ffn, w_gate16, w_ple16, tm, tf=512):
    m = h.shape[0]
    return pl.pallas_call(
        _ffn_kernel,
        grid=(m // tm, D_FF // tf),
        in_specs=[
            pl.BlockSpec((tm, D_MODEL), lambda i, k: (i, 0)),
            pl.BlockSpec((tm, D_MODEL), lambda i, k: (i, 0)),
            pl.BlockSpec((tm, D_PLE), lambda i, k: (i, 0)),
            pl.BlockSpec((D_MODEL, tf), lambda i, k: (0, k)),
            pl.BlockSpec((tf, D_MODEL), lambda i, k: (k, 0)),
            pl.BlockSpec((1, D_MODEL), lambda i, k: (0, 0)),
            pl.BlockSpec((D_MODEL, D_MODEL), lambda i, k: (0, 0)),
            pl.BlockSpec((D_PLE, D_MODEL), lambda i, k: (0, 0)),
        ],
        out_specs=pl.BlockSpec((tm, D_MODEL), lambda i, k: (i, 0)),
        out_shape=jax.ShapeDtypeStruct((m, D_MODEL), F32),
        scratch_shapes=[pltpu.VMEM((tm, D_MODEL), F32)],
        compiler_params=_cparams(("parallel", "arbitrary")),
        name="ffn",
    )(hn, h, pe2d, w_ff1_16, w_ff2_16, g_post_ffn, w_gate16, w_ple16)


def _row(v):
    return v.reshape(1, -1)


def kernel(x_prompt, x_sample, cache_k, cache_v, state_conv, page_table, p_prompt, p_sample,
           w_in, w_out, lambda_q1, lambda_k1, lambda_q2, lambda_k2, g_subln, w_dw, b_dw,
           ln_conv_g, ln_conv_b, g_pre_mix, g_post_mix, g_pre_ffn, g_post_ffn,
           w_ff1, w_ff2, w_ple, w_ple_gate):
    depth = w_in.shape[0]
    batch, seq, _ = x_prompt.shape
    dec_batch, dec_seq, _ = x_sample.shape
    n_pool = cache_k.shape[1]
    mp, ms = batch * seq, dec_batch * dec_seq
    assert 2 * dec_seq == 8 and cache_k.shape[2] == PAGE_SIZE

    hp = x_prompt.reshape(mp, D_MODEL)
    hs = x_sample.reshape(ms, D_MODEL)
    outs = {k: [] for k in ("kp", "vp", "cp", "ks", "vs", "cs")}
    for l in range(depth):
        lam_init = 0.8 - 0.6 * math.exp(-0.3 * l)
        w_in16 = w_in[l].astype(BF16)
        w_kt16 = w_in[l][:, D_ATTN:2 * D_ATTN].T.astype(BF16)
        w_out16 = w_out[l].astype(BF16)
        w_ff1_16 = w_ff1[l].astype(BF16)
        w_ff2_16 = w_ff2[l].astype(BF16)
        w_gate16 = w_ple_gate[l].astype(BF16)
        w_ple16 = w_ple[l].astype(BF16)
        lam_vecs = [_row(lambda_q1[l]), _row(lambda_k1[l]), _row(lambda_q2[l]), _row(lambda_k2[l])]
        gsub = _row(g_subln[l])
        conv_args = (w_dw[l], _row(b_dw[l]), _row(ln_conv_g[l]), _row(ln_conv_b[l]))
        kt_pool = jnp.transpose(cache_k[l], (0, 2, 3, 4, 1)).reshape(n_pool, D_ATTN, PAGE_SIZE)
        vi_pool = cache_v[l].reshape(n_pool, PAGE_SIZE * N_HEADS, V_DIM)

        q_p, kt_p, vi_p, v16_p, u_p = _inproj(hp, _row(g_pre_mix[l]), w_in16, w_kt16, n_seq=batch, tm=512)
        ya_p = _attn_prompt(q_p, kt_p, v16_p, lam_vecs, gsub, batch, seq, lam_init)
        yc_p = _conv_prompt(u_p, *conv_args, batch, seq)
        h_p, hn_p = _outproj(ya_p, yc_p, hp, w_out16, _row(g_post_mix[l]), _row(g_pre_ffn[l]), tm=512)
        hp = _ffn(hn_p, h_p, p_prompt[l].reshape(mp, D_PLE), w_ff1_16, w_ff2_16, _row(g_post_ffn[l]),
                  w_gate16, w_ple16, tm=512)
        outs["kp"].append(jnp.transpose(kt_p.reshape(batch, N_HEADS, 2, HEAD_DIM, seq), (0, 4, 1, 2, 3)))
        outs["vp"].append(vi_p.reshape(batch, seq, N_HEADS, V_DIM))
        outs["cp"].append(u_p.reshape(batch, seq, D_CONV)[:, seq - (CONV_WIDTH - 1):])

        q_s, kt_s, vi_s, _, u_s = _inproj(hs, _row(g_pre_mix[l]), w_in16, w_kt16, n_seq=1, tm=ms)
        ya_s = _attn_paged(q_s, kt_s[0], vi_s, kt_pool, vi_pool, page_table, lam_vecs, gsub,
                           dec_batch, dec_seq, lam_init)
        yc_s, st_tm = _conv_sample(jnp.transpose(state_conv[l], (1, 0, 2)), u_s, *conv_args,
                                   dec_batch, dec_seq)
        h_s, hn_s = _outproj(ya_s, yc_s, hs, w_out16, _row(g_post_mix[l]), _row(g_pre_ffn[l]), tm=ms)
        hs = _ffn(hn_s, h_s, p_sample[l].reshape(ms, D_PLE), w_ff1_16, w_ff2_16, _row(g_post_ffn[l]),
                  w_gate16, w_ple16, tm=ms)
        outs["ks"].append(jnp.transpose(kt_s.reshape(N_HEADS, 2, HEAD_DIM, dec_batch, dec_seq),
                                        (3, 4, 0, 1, 2)))
        outs["vs"].append(vi_s.reshape(dec_batch, dec_seq, N_HEADS, V_DIM))
        outs["cs"].append(jnp.transpose(st_tm, (1, 0, 2)))

    return (hp.reshape(batch, seq, D_MODEL), hs.reshape(dec_batch, dec_seq, D_MODEL),
            jnp.stack(outs["kp"]), jnp.stack(outs["vp"]), jnp.stack(outs["cp"]),
            jnp.stack(outs["ks"]), jnp.stack(outs["vs"]), jnp.stack(outs["cs"]))
```

```python
import functools
import math

import jax
import jax.numpy as jnp
from jax import lax
from jax.experimental import pallas as pl
from jax.experimental.pallas import tpu as pltpu

D_MODEL = 1024
D_ATTN = 512
D_CONV = 512
N_HEADS = 4
HEAD_DIM = 64
V_DIM = 128
CONV_WIDTH = 31
D_FF = 4096
D_PLE = 256
PAGE_SIZE = 128
EPS = 1e-6
N_QKV = 3 * D_ATTN
N_IN = N_QKV + 2 * D_CONV
NEG_BIG = -1e30
LOG2E = math.log2(math.e)
Q_SCALE = HEAD_DIM ** -0.5 * LOG2E
ALIBI_SLOPES = tuple(2.0 ** (-8.0 * (h + 1) / N_HEADS) * LOG2E for h in range(N_HEADS))

F32 = jnp.float32
BF16 = jnp.bfloat16

VMEM_LIMIT_BYTES = 56 * 1024 * 1024


def _cparams(semantics):
    return pltpu.CompilerParams(dimension_semantics=semantics, vmem_limit_bytes=VMEM_LIMIT_BYTES)


def _rms(x, g):
    return x * lax.rsqrt(jnp.mean(x * x, axis=-1, keepdims=True) + EPS) * g


def _diff_lambda(lq1, lk1, lq2, lk2, lam_init):
    return (jnp.exp(jnp.sum(lq1 * lk1, axis=-1, keepdims=True))
            - jnp.exp(jnp.sum(lq2 * lk2, axis=-1, keepdims=True)) + lam_init)


def _nt_dot(a, b):
    return lax.dot_general(a, b, (((1,), (1,)), ((), ())), preferred_element_type=F32)


def _inproj_kernel(x_ref, g_ref, w_ref, wkt_ref, q_ref, kt_ref, vi_ref, v16_ref, u_ref, *, tm):
    xn = _rms(x_ref[...], g_ref[...]).astype(BF16)
    q = jnp.dot(xn, w_ref[:, 0:D_ATTN], preferred_element_type=F32)
    q_ref[...] = (q * Q_SCALE).astype(BF16)
    kt_ref[...] = _nt_dot(wkt_ref[...], xn)
    v = jnp.dot(xn, w_ref[:, 2 * D_ATTN:N_QKV], preferred_element_type=F32)
    v16_ref[...] = v.astype(BF16)
    for h in range(N_HEADS):
        vi_ref[pl.ds(h, tm, stride=N_HEADS), :] = v[:, h * V_DIM:(h + 1) * V_DIM]
    a = jnp.dot(xn, w_ref[:, N_QKV:N_QKV + D_CONV], preferred_element_type=F32)
    gt = jnp.dot(xn, w_ref[:, N_QKV + D_CONV:], preferred_element_type=F32)
    u_ref[...] = a * jax.nn.sigmoid(gt)


def _inproj(x2d, g_pre_mix, w_in16, w_kt16, n_seq, tm):
    m = x2d.shape[0]
    seq = m // n_seq
    nt = seq // tm
    return pl.pallas_call(
        functools.partial(_inproj_kernel, tm=tm),
        grid=(m // tm,),
        in_specs=[
            pl.BlockSpec((tm, D_MODEL), lambda i: (i, 0)),
            pl.BlockSpec((1, D_MODEL), lambda i: (0, 0)),
            pl.BlockSpec((D_MODEL, N_IN), lambda i: (0, 0)),
            pl.BlockSpec((D_ATTN, D_MODEL), lambda i: (0, 0)),
        ],
        out_specs=[
            pl.BlockSpec((tm, D_ATTN), lambda i: (i, 0)),
            pl.BlockSpec((None, D_ATTN, tm), lambda i: (i // nt, 0, i % nt)),
            pl.BlockSpec((tm * N_HEADS, V_DIM), lambda i: (i, 0)),
            pl.BlockSpec((tm, D_ATTN), lambda i: (i, 0)),
            pl.BlockSpec((tm, D_CONV), lambda i: (i, 0)),
        ],
        out_shape=[
            jax.ShapeDtypeStruct((m, D_ATTN), BF16),
            jax.ShapeDtypeStruct((n_seq, D_ATTN, seq), F32),
            jax.ShapeDtypeStruct((m * N_HEADS, V_DIM), F32),
            jax.ShapeDtypeStruct((m, D_ATTN), BF16),
            jax.ShapeDtypeStruct((m, D_CONV), F32),
        ],
        compiler_params=_cparams(("parallel",)),
        name="inproj",
    )(x2d, g_pre_mix, w_in16, w_kt16)


ATTN_TILE = 256
ATTN_ROW_BLOCK = 64
LANES = 128


def _attn_prompt_kernel(lq1_ref, lk1_ref, lq2_ref, lk2_ref, gsub_ref, q_ref, kt_ref, v_ref, o_ref,
                        qs_scr, m_scr, l_scr, acc_scr, *, lam_init):
    t, rb = ATTN_TILE, ATTN_ROW_BLOCK
    n_rb = t // rb
    qi = pl.program_id(1)
    lane = lax.broadcasted_iota(jnp.int32, (rb, 2 * HEAD_DIM), 1)
    kcol = lax.broadcasted_iota(jnp.int32, (1, t), 1)

    for h in range(N_HEADS):
        hs = slice(h * V_DIM, (h + 1) * V_DIM)
        for g in range(n_rb):
            q = q_ref[g * rb:(g + 1) * rb, hs]
            zero = jnp.zeros_like(q)
            qs_scr[h, g] = jnp.concatenate([jnp.where(lane < HEAD_DIM, q, zero),
                                            jnp.where(lane >= HEAD_DIM, q, zero)], axis=0)
    m_scr[...] = jnp.full(m_scr.shape, NEG_BIG, F32)
    l_scr[...] = jnp.zeros(l_scr.shape, F32)
    acc_scr[...] = jnp.zeros(acc_scr.shape, F32)

    def tile(kv, diagonal):
        k0 = pl.multiple_of(kv * t, t)
        koff = (kcol + (kv - qi) * t).astype(F32)
        for h in range(N_HEADS):
            hs = slice(h * V_DIM, (h + 1) * V_DIM)
            kt = kt_ref[hs, pl.ds(k0, t)].astype(BF16)
            vv = v_ref[pl.ds(k0, t), hs]
            bias = ALIBI_SLOPES[h] * koff
            for g in range(n_rb):
                s = jnp.dot(qs_scr[h, g], kt, preferred_element_type=F32) + bias
                if diagonal:
                    row = lax.broadcasted_iota(jnp.int32, (2 * rb, t), 0)
                    row = jnp.where(row >= rb, row - rb, row) + g * rb
                    col = lax.broadcasted_iota(jnp.int32, (2 * rb, t), 1)
                    s = jnp.where(col <= row, s, NEG_BIG)
                m_prev = m_scr[h, g]
                m_new = jnp.maximum(m_prev, jnp.max(s, axis=-1, keepdims=True))
                alpha = jnp.exp2(m_prev - m_new)
                p = jnp.exp2(s - jnp.concatenate([m_new] * (t // LANES), axis=-1))
                l_scr[h, g] = alpha * l_scr[h, g] + jnp.sum(p, axis=-1, keepdims=True)
                acc_scr[h, g] = alpha * acc_scr[h, g] + jnp.dot(p.astype(BF16), vv,
                                                               preferred_element_type=F32)
                m_scr[h, g] = m_new

    def below_diagonal(kv, carry):
        tile(kv, False)
        return carry

    lax.fori_loop(0, qi, below_diagonal, 0)
    tile(qi, True)

    lam = _diff_lambda(lq1_ref[...], lk1_ref[...], lq2_ref[...], lk2_ref[...], lam_init)
    for h in range(N_HEADS):
        for g in range(n_rb):
            o = acc_scr[h, g] / l_scr[h, g]
            oh = o[:rb] - lam * o[rb:]
            y = _rms(oh, gsub_ref[...]) * (1.0 - lam_init)
            o_ref[g * rb:(g + 1) * rb, h * V_DIM:(h + 1) * V_DIM] = y.astype(o_ref.dtype)


def _attn_prompt(q16, kt, v16, lam_vecs, g_subln, batch, seq, lam_init):
    t, rb = ATTN_TILE, ATTN_ROW_BLOCK
    nq = seq // t
    n_rb = t // rb
    kern = functools.partial(_attn_prompt_kernel, lam_init=lam_init)
    vec = lambda n: pl.BlockSpec((1, n), lambda b, qi: (0, 0))
    return pl.pallas_call(
        kern,
        grid=(batch, nq),
        in_specs=[vec(HEAD_DIM)] * 4 + [
            vec(V_DIM),
            pl.BlockSpec((t, D_ATTN), lambda b, qi: (b * nq + qi, 0)),
            pl.BlockSpec((None, D_ATTN, seq), lambda b, qi: (b, 0, 0)),
            pl.BlockSpec((seq, D_ATTN), lambda b, qi: (b, 0)),
        ],
        out_specs=pl.BlockSpec((t, D_ATTN), lambda b, qi: (b * nq + qi, 0)),
        out_shape=jax.ShapeDtypeStruct((batch * seq, D_ATTN), BF16),
        scratch_shapes=[
            pltpu.VMEM((N_HEADS, n_rb, 2 * rb, 2 * HEAD_DIM), BF16),
            pltpu.VMEM((N_HEADS, n_rb, 2 * rb, LANES), F32),
            pltpu.VMEM((N_HEADS, n_rb, 2 * rb, LANES), F32),
            pltpu.VMEM((N_HEADS, n_rb, 2 * rb, V_DIM), F32),
        ],
        compiler_params=_cparams(("parallel", "arbitrary")),
        name="attn_prompt",
    )(*lam_vecs, g_subln, q16, kt, v16)


def _attn_paged_kernel(pt_ref, lq1_ref, lk1_ref, lq2_ref, lk2_ref, gsub_ref, q_ref, ktn_ref, vin_ref,
                       *rest, pps, n_steps, past_len, dec_seq, lam_init):
    ktp_refs = rest[:pps]
    vip_refs = rest[pps:2 * pps]
    o_ref = rest[2 * pps]
    q2_scr, qbd_scr, m_scr, l_scr, acc_scr = rest[2 * pps + 1:]
    b = pl.program_id(0)
    j = pl.program_id(1)
    rph = 2 * dec_seq
    n_rows = N_HEADS * rph
    ppc = pps // PAGED_CHAINS

    row = lax.broadcasted_iota(jnp.int32, (n_rows, 1), 0)
    head = row // rph
    comp = (row % rph) // dec_seq
    tok = row % dec_seq
    slope = jnp.zeros((n_rows, 1), F32)
    for h in range(N_HEADS):
        slope = jnp.where(head == h, ALIBI_SLOPES[h], slope)

    @pl.when(j == 0)
    def _():
        q = q_ref[...]
        q2_scr[0:dec_seq, :] = q
        q2_scr[dec_seq:rph, :] = q
        qrep = jnp.concatenate([q2_scr[...]] * N_HEADS, axis=0)
        col = lax.broadcasted_iota(jnp.int32, (n_rows, D_ATTN), 1)
        qbd_scr[...] = jnp.where(col // HEAD_DIM == head * 2 + comp, qrep, 0.0).astype(BF16)
        m_scr[...] = jnp.full(m_scr.shape, NEG_BIG, F32)
        l_scr[...] = jnp.zeros(l_scr.shape, F32)
        acc_scr[...] = jnp.zeros(acc_scr.shape, F32)

    qbd = qbd_scr[...]

    def online_update(c, s, v_refs, n_slots):
        m_prev = m_scr[c]
        m_new = jnp.maximum(m_prev, jnp.max(s, axis=-1, keepdims=True))
        alpha = jnp.exp2(m_prev - m_new)
        p = jnp.exp2(s - m_new)
        l_scr[c] = alpha * l_scr[c] + jnp.sum(p, axis=-1, keepdims=True)
        p16 = p.astype(BF16)
        for h in range(N_HEADS):
            rs = slice(h * rph, (h + 1) * rph)
            vh = jnp.concatenate([r[pl.ds(h, n_slots, stride=N_HEADS), :] for r in v_refs], axis=0)
            pv = jnp.dot(p16[rs, :], vh.astype(BF16), preferred_element_type=F32)
            acc_scr[c, rs, :] = alpha[rs] * acc_scr[c, rs, :] + pv
        m_scr[c] = m_new

    kcol = lax.broadcasted_iota(jnp.int32, (1, ppc * PAGE_SIZE), 1)
    for c in range(PAGED_CHAINS):
        pages = range(c * ppc, (c + 1) * ppc)
        s = jnp.concatenate(
            [jnp.dot(qbd, ktp_refs[i][...].astype(BF16), preferred_element_type=F32) for i in pages], axis=-1)
        koff = (kcol + ((j * pps + c * ppc) * PAGE_SIZE - past_len)).astype(F32)
        online_update(c, s + slope * koff, [vip_refs[i] for i in pages], PAGE_SIZE)

    @pl.when(j == n_steps - 1)
    def _():
        n_new = ktn_ref.shape[1]
        sn = jnp.dot(qbd, ktn_ref[...].astype(BF16), preferred_element_type=F32)
        col = lax.broadcasted_iota(jnp.int32, (n_rows, n_new), 1)
        ctok = col % dec_seq
        valid = jnp.logical_and(col // dec_seq == b, ctok <= tok)
        sn = jnp.where(valid, sn + slope * ctok.astype(F32), NEG_BIG)
        online_update(0, sn, [vin_ref], n_new)

        m = m_scr[0]
        for c in range(1, PAGED_CHAINS):
            m = jnp.maximum(m, m_scr[c])
        l = jnp.zeros((n_rows, 1), F32)
        acc = jnp.zeros((n_rows, V_DIM), F32)
        for c in range(PAGED_CHAINS):
            w = jnp.exp2(m_scr[c] - m)
            l = l + w * l_scr[c]
            acc = acc + w * acc_scr[c]

        lam = _diff_lambda(lq1_ref[...], lk1_ref[...], lq2_ref[...], lk2_ref[...], lam_init)
        o = acc / l
        for h in range(N_HEADS):
            oh = o[h * rph:h * rph + dec_seq] - lam * o[h * rph + dec_seq:(h + 1) * rph]
            o_ref[:, h * V_DIM:(h + 1) * V_DIM] = _rms(oh, gsub_ref[...]) * (1.0 - lam_init)


PAGED_PAGES_PER_STEP = 16
PAGED_CHAINS = 2


def _attn_paged(q_s, kt_new, vi_new, kt_pool, vi_pool, page_table, lam_vecs, g_subln, dec_batch, dec_seq,
                lam_init):
    pps = PAGED_PAGES_PER_STEP
    n_pages = page_table.shape[1]
    n_steps = n_pages // pps
    past_len = n_pages * PAGE_SIZE
    rph = 2 * dec_seq
    n_rows = N_HEADS * rph
    n_new = dec_batch * dec_seq
    q3 = q_s.astype(F32).reshape(dec_batch, dec_seq, D_ATTN)
    kern = functools.partial(_attn_paged_kernel, pps=pps, n_steps=n_steps, past_len=past_len,
                             dec_seq=dec_seq, lam_init=lam_init)
    vec = lambda n: pl.BlockSpec((1, n), lambda b, j, pt: (0, 0))

    def kpage_spec(i):
        return pl.BlockSpec((None, D_ATTN, PAGE_SIZE), lambda b, j, pt: (pt[b, j * pps + i], 0, 0))

    def vpage_spec(i):
        return pl.BlockSpec((None, PAGE_SIZE * N_HEADS, V_DIM), lambda b, j, pt: (pt[b, j * pps + i], 0, 0))

    grid_spec = pltpu.PrefetchScalarGridSpec(
        num_scalar_prefetch=1,
        grid=(dec_batch, n_steps),
        in_specs=[vec(HEAD_DIM)] * 4 + [
            vec(V_DIM),
            pl.BlockSpec((None, dec_seq, D_ATTN), lambda b, j, pt: (b, 0, 0)),
            pl.BlockSpec((D_ATTN, n_new), lambda b, j, pt: (0, 0)),
            pl.BlockSpec((n_new * N_HEADS, V_DIM), lambda b, j, pt: (0, 0)),
        ] + [kpage_spec(i) for i in range(pps)] + [vpage_spec(i) for i in range(pps)],
        out_specs=pl.BlockSpec((None, dec_seq, D_ATTN), lambda b, j, pt: (b, 0, 0)),
        scratch_shapes=[
            pltpu.VMEM((rph, D_ATTN), F32),
            pltpu.VMEM((n_rows, D_ATTN), BF16),
            pltpu.VMEM((PAGED_CHAINS, n_rows, 1), F32),
            pltpu.VMEM((PAGED_CHAINS, n_rows, 1), F32),
            pltpu.VMEM((PAGED_CHAINS, n_rows, V_DIM), F32),
        ],
    )
    out = pl.pallas_call(
        kern,
        grid_spec=grid_spec,
        out_shape=jax.ShapeDtypeStruct((dec_batch, dec_seq, D_ATTN), F32),
        compiler_params=_cparams(("parallel", "arbitrary")),
        name="attn_paged",
    )(page_table, *lam_vecs, g_subln, q3, kt_new, vi_new, *([kt_pool] * pps), *([vi_pool] * pps))
    return out.reshape(dec_batch * dec_seq, D_ATTN)


SUBLANES = 8
CONV_HALO = 32
CONV_ROW_CHUNK = 16


def _ln_swish(c, g, b):
    mu = jnp.mean(c, axis=-1, keepdims=True)
    d = c - mu
    var = jnp.mean(d * d, axis=-1, keepdims=True)
    y = d * lax.rsqrt(var + EPS) * g + b
    return y * jax.nn.sigmoid(y)


def _conv_prompt_kernel(prev_ref, cur_ref, w_ref, b_ref, g_ref, beta_ref, o_ref, ext_scr, sh_scr, *, tm):
    i = pl.program_id(1)
    n_ext = CONV_HALO + tm
    ext_scr[0:CONV_HALO, :] = jnp.where(i == 0, 0.0, prev_ref[...])
    ext_scr[CONV_HALO:n_ext, :] = cur_ref[...]
    ext = ext_scr[...]
    for r in range(1, SUBLANES):
        sh_scr[r] = pltpu.roll(ext, n_ext - r, axis=0)
    first = CONV_HALO - (CONV_WIDTH - 1)
    for c in range(0, tm, CONV_ROW_CHUNK):
        acc = jnp.broadcast_to(b_ref[...], (CONV_ROW_CHUNK, D_CONV))
        for j in range(CONV_WIDTH):
            r = (first + j) % SUBLANES
            a = c + first + j - r
            src = ext_scr[a:a + CONV_ROW_CHUNK, :] if r == 0 else sh_scr[r, a:a + CONV_ROW_CHUNK, :]
            acc = acc + jnp.concatenate([w_ref[j]] * (CONV_ROW_CHUNK // SUBLANES), axis=0) * src
        o_ref[c:c + CONV_ROW_CHUNK, :] = _ln_swish(acc, g_ref[...], beta_ref[...]).astype(o_ref.dtype)


def _conv_prompt(u2d, w_dw, b_dw, ln_g, ln_b, batch, seq, tm=128):
    nt = seq // tm
    halo_per_tile = tm // CONV_HALO
    kern = functools.partial(_conv_prompt_kernel, tm=tm)
    vec = pl.BlockSpec((1, D_CONV), lambda b, i: (0, 0))
    w_rep = jnp.broadcast_to(w_dw[:, None, :], (CONV_WIDTH, SUBLANES, D_CONV))
    return pl.pallas_call(
        kern,
        grid=(batch, nt),
        in_specs=[
            pl.BlockSpec((CONV_HALO, D_CONV),
                         lambda b, i: (jnp.maximum((b * nt + i) * halo_per_tile - 1, 0), 0)),
            pl.BlockSpec((tm, D_CONV), lambda b, i: (b * nt + i, 0)),
            pl.BlockSpec((CONV_WIDTH, SUBLANES, D_CONV), lambda b, i: (0, 0, 0)),
            vec, vec, vec,
        ],
        out_specs=pl.BlockSpec((tm, D_CONV), lambda b, i: (b * nt + i, 0)),
        out_shape=jax.ShapeDtypeStruct((batch * seq, D_CONV), BF16),
        scratch_shapes=[pltpu.VMEM((CONV_HALO + tm, D_CONV), F32),
                        pltpu.VMEM((SUBLANES, CONV_HALO + tm, D_CONV), F32)],
        compiler_params=_cparams(("parallel", "parallel")),
        name="conv_prompt",
    )(u2d, u2d, w_rep, b_dw, ln_g, ln_b)


def _conv_sample_kernel(state_ref, u_ref, w_ref, b_ref, g_ref, beta_ref, o_ref, st_ref, *, dec_batch, dec_seq):
    n_state = CONV_WIDTH - 1

    def ext(i):
        return state_ref[i] if i < n_state else u_ref[i - n_state]

    for t in range(dec_seq):
        acc = jnp.broadcast_to(b_ref[...], (dec_batch, D_CONV))
        for j in range(CONV_WIDTH):
            acc = acc + w_ref[j:j + 1, :] * ext(t + j)
        o_ref[t] = _ln_swish(acc, g_ref[...], beta_ref[...])
    for i in range(n_state):
        st_ref[i] = ext(i + dec_seq)


def _conv_sample(state_tm, u_s, w_dw, b_dw, ln_g, ln_b, dec_batch, dec_seq):
    kern = functools.partial(_conv_sample_kernel, dec_batch=dec_batch, dec_seq=dec_seq)
    u_tm = jnp.transpose(u_s.reshape(dec_batch, dec_seq, D_CONV), (1, 0, 2))
    y_tm, st_tm = pl.pallas_call(
        kern,
        out_shape=[
            jax.ShapeDtypeStruct((dec_seq, dec_batch, D_CONV), F32),
            jax.ShapeDtypeStruct(state_tm.shape, F32),
        ],
        compiler_params=pltpu.CompilerParams(vmem_limit_bytes=VMEM_LIMIT_BYTES),
        name="conv_sample",
    )(state_tm, u_tm, w_dw, b_dw, ln_g, ln_b)
    return jnp.transpose(y_tm, (1, 0, 2)).reshape(dec_batch * dec_seq, D_CONV), st_tm


def _post_kernel(ya_ref, yc_ref, x_ref, pe_ref, wo_ref, gpm_ref, gpf_ref, w1_ref, w2_ref, gpost_ref,
                 wg_ref, wp_ref, o_ref, h_scr, hn_scr, acc_scr):
    k = pl.program_id(1)

    @pl.when(k == 0)
    def _():
        m = (jnp.dot(ya_ref[...].astype(BF16), wo_ref[0:D_ATTN, :], preferred_element_type=F32)
             + jnp.dot(yc_ref[...].astype(BF16), wo_ref[D_ATTN:, :], preferred_element_type=F32))
        h = x_ref[...] + _rms(m, gpm_ref[...])
        h_scr[...] = h
        hn_scr[...] = _rms(h, gpf_ref[...]).astype(BF16)

    a = jnp.dot(hn_scr[...], w1_ref[...], preferred_element_type=F32)
    a = jnp.square(jnp.maximum(a, 0.0)).astype(BF16)
    contrib = jnp.dot(a, w2_ref[...], preferred_element_type=F32)

    @pl.when(k == 0)
    def _():
        acc_scr[...] = contrib

    @pl.when(k > 0)
    def _():
        acc_scr[...] += contrib

    @pl.when(k == pl.num_programs(1) - 1)
    def _():
        h2 = h_scr[...] + _rms(acc_scr[...], gpost_ref[...])
        gate = jax.nn.sigmoid(jnp.dot(h2.astype(BF16), wg_ref[...], preferred_element_type=F32))
        pev = jnp.dot(pe_ref[...].astype(BF16), wp_ref[...], preferred_element_type=F32)
        o_ref[...] = h2 + gate * pev


def _post(y_attn, y_conv, x2d, pe2d, w_out16, g_post_mix, g_pre_ffn, w_ff1_16, w_ff2_16, g_post_ffn,
          w_gate16, w_ple16, tm, tf=1024):
    m = x2d.shape[0]
    row_tile = lambda n: pl.BlockSpec((tm, n), lambda i, k: (i, 0))
    vec = pl.BlockSpec((1, D_MODEL), lambda i, k: (0, 0))
    resident = lambda r, c: pl.BlockSpec((r, c), lambda i, k: (0, 0), pipeline_mode=pl.Buffered(1))
    return pl.pallas_call(
        _post_kernel,
        grid=(m // tm, D_FF // tf),
        in_specs=[
            row_tile(D_ATTN), row_tile(D_CONV), row_tile(D_MODEL), row_tile(D_PLE),
            resident(D_MODEL, D_MODEL), vec, vec,
            pl.BlockSpec((D_MODEL, tf), lambda i, k: (0, k)),
            pl.BlockSpec((tf, D_MODEL), lambda i, k: (k, 0)),
            vec,
            resident(D_MODEL, D_MODEL), resident(D_PLE, D_MODEL),
        ],
        out_specs=row_tile(D_MODEL),
        out_shape=jax.ShapeDtypeStruct((m, D_MODEL), F32),
        scratch_shapes=[
            pltpu.VMEM((tm, D_MODEL), F32),
            pltpu.VMEM((tm, D_MODEL), BF16),
            pltpu.VMEM((tm, D_MODEL), F32),
        ],
        compiler_params=_cparams(("parallel", "arbitrary")),
        name="post",
    )(y_attn, y_conv, x2d, pe2d, w_out16, g_post_mix, g_pre_ffn, w_ff1_16, w_ff2_16, g_post_ffn,
      w_gate16, w_ple16)


def _row(v):
    return v.reshape(1, -1)


def kernel(x_prompt, x_sample, cache_k, cache_v, state_conv, page_table, p_prompt, p_sample,
           w_in, w_out, lambda_q1, lambda_k1, lambda_q2, lambda_k2, g_subln, w_dw, b_dw,
           ln_conv_g, ln_conv_b, g_pre_mix, g_post_mix, g_pre_ffn, g_post_ffn,
           w_ff1, w_ff2, w_ple, w_ple_gate):
    depth = w_in.shape[0]
    batch, seq, _ = x_prompt.shape
    dec_batch, dec_seq, _ = x_sample.shape
    n_pool = cache_k.shape[1]
    mp, ms = batch * seq, dec_batch * dec_seq
    assert 2 * dec_seq == 8 and cache_k.shape[2] == PAGE_SIZE

    hp = x_prompt.reshape(mp, D_MODEL)
    hs = x_sample.reshape(ms, D_MODEL)
    outs = {k: [] for k in ("kp", "vp", "cp", "ks", "vs", "cs")}
    for l in range(depth):
        lam_init = 0.8 - 0.6 * math.exp(-0.3 * l)
        w_in16 = w_in[l].astype(BF16)
        w_kt16 = w_in[l][:, D_ATTN:2 * D_ATTN].T.astype(BF16)
        w_out16 = w_out[l].astype(BF16)
        w_ff1_16 = w_ff1[l].astype(BF16)
        w_ff2_16 = w_ff2[l].astype(BF16)
        w_gate16 = w_ple_gate[l].astype(BF16)
        w_ple16 = w_ple[l].astype(BF16)
        lam_vecs = [_row(lambda_q1[l]), _row(lambda_k1[l]), _row(lambda_q2[l]), _row(lambda_k2[l])]
        gsub = _row(g_subln[l])
        conv_args = (w_dw[l], _row(b_dw[l]), _row(ln_conv_g[l]), _row(ln_conv_b[l]))
        kt_pool = jnp.transpose(cache_k[l], (0, 2, 3, 4, 1)).reshape(n_pool, D_ATTN, PAGE_SIZE)
        vi_pool = cache_v[l].reshape(n_pool, PAGE_SIZE * N_HEADS, V_DIM)

        q_p, kt_p, vi_p, v16_p, u_p = _inproj(hp, _row(g_pre_mix[l]), w_in16, w_kt16, n_seq=batch, tm=512)
        ya_p = _attn_prompt(q_p, kt_p, v16_p, lam_vecs, gsub, batch, seq, lam_init)
        yc_p = _conv_prompt(u_p, *conv_args, batch, seq)
        post_w = (w_out16, _row(g_post_mix[l]), _row(g_pre_ffn[l]), w_ff1_16, w_ff2_16, _row(g_post_ffn[l]),
                  w_gate16, w_ple16)
        hp = _post(ya_p, yc_p, hp, p_prompt[l].reshape(mp, D_PLE), *post_w, tm=512)
        outs["kp"].append(jnp.transpose(kt_p.reshape(batch, N_HEADS, 2, HEAD_DIM, seq), (0, 4, 1, 2, 3)))
        outs["vp"].append(vi_p.reshape(batch, seq, N_HEADS, V_DIM))
        outs["cp"].append(u_p.reshape(batch, seq, D_CONV)[:, seq - (CONV_WIDTH - 1):])

        q_s, kt_s, vi_s, _, u_s = _inproj(hs, _row(g_pre_mix[l]), w_in16, w_kt16, n_seq=1, tm=ms)
        ya_s = _attn_paged(q_s, kt_s[0], vi_s, kt_pool, vi_pool, page_table, lam_vecs, gsub,
                           dec_batch, dec_seq, lam_init)
        yc_s, st_tm = _conv_sample(jnp.transpose(state_conv[l], (1, 0, 2)), u_s, *conv_args,
                                   dec_batch, dec_seq)
        hs = _post(ya_s, yc_s, hs, p_sample[l].reshape(ms, D_PLE), *post_w, tm=ms)
        outs["ks"].append(jnp.transpose(kt_s.reshape(N_HEADS, 2, HEAD_DIM, dec_batch, dec_seq),
                                        (3, 4, 0, 1, 2)))
        outs["vs"].append(vi_s.reshape(dec_batch, dec_seq, N_HEADS, V_DIM))
        outs["cs"].append(jnp.transpose(st_tm, (1, 0, 2)))

    return (hp.reshape(batch, seq, D_MODEL), hs.reshape(dec_batch, dec_seq, D_MODEL),
            jnp.stack(outs["kp"]), jnp.stack(outs["vp"]), jnp.stack(outs["cp"]),
            jnp.stack(outs["ks"]), jnp.stack(outs["vs"]), jnp.stack(outs["cs"]))
```

```python
import functools
import math

import jax
import jax.numpy as jnp
from jax import lax
from jax.experimental import pallas as pl
from jax.experimental.pallas import tpu as pltpu

D_MODEL = 1024
D_ATTN = 512
D_CONV = 512
N_HEADS = 4
HEAD_DIM = 64
V_DIM = 128
CONV_WIDTH = 31
D_FF = 4096
D_PLE = 256
PAGE_SIZE = 128
EPS = 1e-6
N_QKV = 3 * D_ATTN
N_IN = N_QKV + 2 * D_CONV
NEG_BIG = -1e30
LOG2E = math.log2(math.e)
Q_SCALE = HEAD_DIM ** -0.5 * LOG2E
ALIBI_SLOPES = tuple(2.0 ** (-8.0 * (h + 1) / N_HEADS) * LOG2E for h in range(N_HEADS))

F32 = jnp.float32
BF16 = jnp.bfloat16

VMEM_LIMIT_BYTES = 56 * 1024 * 1024


def _cparams(semantics):
    return pltpu.CompilerParams(dimension_semantics=semantics, vmem_limit_bytes=VMEM_LIMIT_BYTES)


def _rms(x, g):
    return x * lax.rsqrt(jnp.mean(x * x, axis=-1, keepdims=True) + EPS) * g


def _diff_lambda(lq1, lk1, lq2, lk2, lam_init):
    return (jnp.exp(jnp.sum(lq1 * lk1, axis=-1, keepdims=True))
            - jnp.exp(jnp.sum(lq2 * lk2, axis=-1, keepdims=True)) + lam_init)


def _nt_dot(a, b):
    return lax.dot_general(a, b, (((1,), (1,)), ((), ())), preferred_element_type=F32)


def _inproj_kernel(x_ref, g_ref, w_ref, wkt_ref, q_ref, kt_ref, vi_ref, v16_ref, u_ref, *, tm):
    xn = _rms(x_ref[...], g_ref[...]).astype(BF16)
    q = jnp.dot(xn, w_ref[:, 0:D_ATTN], preferred_element_type=F32)
    q_ref[...] = (q * Q_SCALE).astype(BF16)
    kt_ref[...] = _nt_dot(wkt_ref[...], xn)
    v = jnp.dot(xn, w_ref[:, 2 * D_ATTN:N_QKV], preferred_element_type=F32)
    v16_ref[...] = v.astype(BF16)
    for h in range(N_HEADS):
        vi_ref[pl.ds(h, tm, stride=N_HEADS), :] = v[:, h * V_DIM:(h + 1) * V_DIM]
    a = jnp.dot(xn, w_ref[:, N_QKV:N_QKV + D_CONV], preferred_element_type=F32)
    gt = jnp.dot(xn, w_ref[:, N_QKV + D_CONV:], preferred_element_type=F32)
    u_ref[...] = a * jax.nn.sigmoid(gt)


def _inproj(x2d, g_pre_mix, w_in16, w_kt16, n_seq, tm):
    m = x2d.shape[0]
    seq = m // n_seq
    nt = seq // tm
    return pl.pallas_call(
        functools.partial(_inproj_kernel, tm=tm),
        grid=(m // tm,),
        in_specs=[
            pl.BlockSpec((tm, D_MODEL), lambda i: (i, 0)),
            pl.BlockSpec((1, D_MODEL), lambda i: (0, 0)),
            pl.BlockSpec((D_MODEL, N_IN), lambda i: (0, 0)),
            pl.BlockSpec((D_ATTN, D_MODEL), lambda i: (0, 0)),
        ],
        out_specs=[
            pl.BlockSpec((tm, D_ATTN), lambda i: (i, 0)),
            pl.BlockSpec((None, D_ATTN, tm), lambda i: (i // nt, 0, i % nt)),
            pl.BlockSpec((tm * N_HEADS, V_DIM), lambda i: (i, 0)),
            pl.BlockSpec((tm, D_ATTN), lambda i: (i, 0)),
            pl.BlockSpec((tm, D_CONV), lambda i: (i, 0)),
        ],
        out_shape=[
            jax.ShapeDtypeStruct((m, D_ATTN), BF16),
            jax.ShapeDtypeStruct((n_seq, D_ATTN, seq), F32),
            jax.ShapeDtypeStruct((m * N_HEADS, V_DIM), F32),
            jax.ShapeDtypeStruct((m, D_ATTN), BF16),
            jax.ShapeDtypeStruct((m, D_CONV), F32),
        ],
        compiler_params=_cparams(("parallel",)),
        name="inproj",
    )(x2d, g_pre_mix, w_in16, w_kt16)


ATTN_TILE = 256
ATTN_ROW_BLOCK = 64
LANES = 128


def _attn_prompt_kernel(lq1_ref, lk1_ref, lq2_ref, lk2_ref, gsub_ref, q_ref, kt_ref, v_ref, o_ref,
                        qs_scr, m_scr, l_scr, acc_scr, *, lam_init):
    t, rb = ATTN_TILE, ATTN_ROW_BLOCK
    n_rb = t // rb
    qi = pl.program_id(1)
    lane = lax.broadcasted_iota(jnp.int32, (rb, 2 * HEAD_DIM), 1)
    kcol = lax.broadcasted_iota(jnp.int32, (1, t), 1)

    for h in range(N_HEADS):
        hs = slice(h * V_DIM, (h + 1) * V_DIM)
        for g in range(n_rb):
            q = q_ref[g * rb:(g + 1) * rb, hs]
            zero = jnp.zeros_like(q)
            qs_scr[h, g] = jnp.concatenate([jnp.where(lane < HEAD_DIM, q, zero),
                                            jnp.where(lane >= HEAD_DIM, q, zero)], axis=0)
    m_scr[...] = jnp.full(m_scr.shape, NEG_BIG, F32)
    l_scr[...] = jnp.zeros(l_scr.shape, F32)
    acc_scr[...] = jnp.zeros(acc_scr.shape, F32)

    def tile(kv, diagonal):
        k0 = pl.multiple_of(kv * t, t)
        koff = (kcol + (kv - qi) * t).astype(F32)
        for h in range(N_HEADS):
            hs = slice(h * V_DIM, (h + 1) * V_DIM)
            kt = kt_ref[hs, pl.ds(k0, t)].astype(BF16)
            vv = v_ref[pl.ds(k0, t), hs]
            bias = ALIBI_SLOPES[h] * koff
            for g in range(n_rb):
                s = jnp.dot(qs_scr[h, g], kt, preferred_element_type=F32) + bias
                if diagonal:
                    row = lax.broadcasted_iota(jnp.int32, (2 * rb, t), 0)
                    row = jnp.where(row >= rb, row - rb, row) + g * rb
                    col = lax.broadcasted_iota(jnp.int32, (2 * rb, t), 1)
                    s = jnp.where(col <= row, s, NEG_BIG)
                m_prev = m_scr[h, g]
                m_new = jnp.maximum(m_prev, jnp.max(s, axis=-1, keepdims=True))
                alpha = jnp.exp2(m_prev - m_new)
                p = jnp.exp2(s - jnp.concatenate([m_new] * (t // LANES), axis=-1))
                l_scr[h, g] = alpha * l_scr[h, g] + jnp.sum(p, axis=-1, keepdims=True)
                acc_scr[h, g] = alpha * acc_scr[h, g] + jnp.dot(p.astype(BF16), vv,
                                                               preferred_element_type=F32)
                m_scr[h, g] = m_new

    def below_diagonal(kv, carry):
        tile(kv, False)
        return carry

    lax.fori_loop(0, qi, below_diagonal, 0)
    tile(qi, True)

    lam = _diff_lambda(lq1_ref[...], lk1_ref[...], lq2_ref[...], lk2_ref[...], lam_init)
    for h in range(N_HEADS):
        for g in range(n_rb):
            o = acc_scr[h, g] / l_scr[h, g]
            oh = o[:rb] - lam * o[rb:]
            y = _rms(oh, gsub_ref[...]) * (1.0 - lam_init)
            o_ref[g * rb:(g + 1) * rb, h * V_DIM:(h + 1) * V_DIM] = y.astype(o_ref.dtype)


def _attn_prompt(q16, kt, v16, lam_vecs, g_subln, batch, seq, lam_init):
    t, rb = ATTN_TILE, ATTN_ROW_BLOCK
    nq = seq // t
    n_rb = t // rb
    kern = functools.partial(_attn_prompt_kernel, lam_init=lam_init)
    vec = lambda n: pl.BlockSpec((1, n), lambda b, qi: (0, 0))
    return pl.pallas_call(
        kern,
        grid=(batch, nq),
        in_specs=[vec(HEAD_DIM)] * 4 + [
            vec(V_DIM),
            pl.BlockSpec((t, D_ATTN), lambda b, qi: (b * nq + qi, 0)),
            pl.BlockSpec((None, D_ATTN, seq), lambda b, qi: (b, 0, 0)),
            pl.BlockSpec((seq, D_ATTN), lambda b, qi: (b, 0)),
        ],
        out_specs=pl.BlockSpec((t, D_ATTN), lambda b, qi: (b * nq + qi, 0)),
        out_shape=jax.ShapeDtypeStruct((batch * seq, D_ATTN), BF16),
        scratch_shapes=[
            pltpu.VMEM((N_HEADS, n_rb, 2 * rb, 2 * HEAD_DIM), BF16),
            pltpu.VMEM((N_HEADS, n_rb, 2 * rb, LANES), F32),
            pltpu.VMEM((N_HEADS, n_rb, 2 * rb, LANES), F32),
            pltpu.VMEM((N_HEADS, n_rb, 2 * rb, V_DIM), F32),
        ],
        compiler_params=_cparams(("parallel", "arbitrary")),
        name="attn_prompt",
    )(*lam_vecs, g_subln, q16, kt, v16)


PAGED_CHAINS = 2


def _paged_rows(dec_seq):
    rph = 2 * dec_seq
    n_rows = N_HEADS * rph
    row = lax.broadcasted_iota(jnp.int32, (n_rows, 1), 0)
    head = row // rph
    comp = (row % rph) // dec_seq
    tok = row % dec_seq
    slope = jnp.zeros((n_rows, 1), F32)
    for h in range(N_HEADS):
        slope = jnp.where(head == h, ALIBI_SLOPES[h], slope)
    return rph, n_rows, head, comp, tok, slope


def _paged_init(q_ref, q2_scr, qbd_scr, m_scr, l_scr, acc_scr, *, dec_seq):
    rph, n_rows, head, comp, _, _ = _paged_rows(dec_seq)
    q = q_ref[...]
    q2_scr[0:dec_seq, :] = q
    q2_scr[dec_seq:rph, :] = q
    qrep = jnp.concatenate([q2_scr[...]] * N_HEADS, axis=0)
    col = lax.broadcasted_iota(jnp.int32, (n_rows, D_ATTN), 1)
    qbd_scr[...] = jnp.where(col // HEAD_DIM == head * 2 + comp, qrep, 0.0).astype(BF16)
    m_scr[...] = jnp.full(m_scr.shape, NEG_BIG, F32)
    l_scr[...] = jnp.zeros(l_scr.shape, F32)
    acc_scr[...] = jnp.zeros(acc_scr.shape, F32)


def _paged_update(c, s, v_refs, n_slots, m_scr, l_scr, acc_scr, rph):
    m_prev = m_scr[c]
    m_new = jnp.maximum(m_prev, jnp.max(s, axis=-1, keepdims=True))
    alpha = jnp.exp2(m_prev - m_new)
    p = jnp.exp2(s - m_new)
    l_scr[c] = alpha * l_scr[c] + jnp.sum(p, axis=-1, keepdims=True)
    p16 = p.astype(BF16)
    for h in range(N_HEADS):
        rs = slice(h * rph, (h + 1) * rph)
        vh = jnp.concatenate([r[pl.ds(h, n_slots, stride=N_HEADS), :] for r in v_refs], axis=0)
        pv = jnp.dot(p16[rs, :], vh.astype(BF16), preferred_element_type=F32)
        acc_scr[c, rs, :] = alpha[rs] * acc_scr[c, rs, :] + pv
    m_scr[c] = m_new


def _paged_pages(j, ktp_refs, vip_refs, qbd_scr, m_scr, l_scr, acc_scr, *, past_len, dec_seq):
    rph, _, _, _, _, slope = _paged_rows(dec_seq)
    pps = len(ktp_refs)
    ppc = pps // PAGED_CHAINS
    qbd = qbd_scr[...]
    kcol = lax.broadcasted_iota(jnp.int32, (1, ppc * PAGE_SIZE), 1)
    for c in range(PAGED_CHAINS):
        pages = range(c * ppc, (c + 1) * ppc)
        s = jnp.concatenate(
            [jnp.dot(qbd, ktp_refs[i][...].astype(BF16), preferred_element_type=F32) for i in pages], axis=-1)
        koff = (kcol + ((j * pps + c * ppc) * PAGE_SIZE - past_len)).astype(F32)
        _paged_update(c, s + slope * koff, [vip_refs[i] for i in pages], PAGE_SIZE, m_scr, l_scr, acc_scr, rph)


def _paged_finish(b, lam_refs, gsub_ref, ktn_ref, vin_ref, o_ref, qbd_scr, m_scr, l_scr, acc_scr, *,
                  dec_seq, lam_init):
    rph, n_rows, _, _, tok, slope = _paged_rows(dec_seq)
    n_new = ktn_ref.shape[1]
    sn = jnp.dot(qbd_scr[...], ktn_ref[...].astype(BF16), preferred_element_type=F32)
    col = lax.broadcasted_iota(jnp.int32, (n_rows, n_new), 1)
    ctok = col % dec_seq
    valid = jnp.logical_and(col // dec_seq == b, ctok <= tok)
    sn = jnp.where(valid, sn + slope * ctok.astype(F32), NEG_BIG)
    _paged_update(0, sn, [vin_ref], n_new, m_scr, l_scr, acc_scr, rph)

    m = m_scr[0]
    for c in range(1, PAGED_CHAINS):
        m = jnp.maximum(m, m_scr[c])
    l = jnp.zeros((n_rows, 1), F32)
    acc = jnp.zeros((n_rows, V_DIM), F32)
    for c in range(PAGED_CHAINS):
        w = jnp.exp2(m_scr[c] - m)
        l = l + w * l_scr[c]
        acc = acc + w * acc_scr[c]

    lam = _diff_lambda(*[r[...] for r in lam_refs], lam_init)
    o = acc / l
    for h in range(N_HEADS):
        oh = o[h * rph:h * rph + dec_seq] - lam * o[h * rph + dec_seq:(h + 1) * rph]
        o_ref[:, h * V_DIM:(h + 1) * V_DIM] = _rms(oh, gsub_ref[...]) * (1.0 - lam_init)


SUBLANES = 8
CONV_HALO = 32
CONV_ROW_CHUNK = 16


def _ln_swish(c, g, b):
    mu = jnp.mean(c, axis=-1, keepdims=True)
    d = c - mu
    var = jnp.mean(d * d, axis=-1, keepdims=True)
    y = d * lax.rsqrt(var + EPS) * g + b
    return y * jax.nn.sigmoid(y)


def _conv_prompt_kernel(prev_ref, cur_ref, w_ref, b_ref, g_ref, beta_ref, o_ref, ext_scr, sh_scr, *, tm):
    i = pl.program_id(1)
    n_ext = CONV_HALO + tm
    ext_scr[0:CONV_HALO, :] = jnp.where(i == 0, 0.0, prev_ref[...])
    ext_scr[CONV_HALO:n_ext, :] = cur_ref[...]
    ext = ext_scr[...]
    for r in range(1, SUBLANES):
        sh_scr[r] = pltpu.roll(ext, n_ext - r, axis=0)
    first = CONV_HALO - (CONV_WIDTH - 1)
    for c in range(0, tm, CONV_ROW_CHUNK):
        acc = jnp.broadcast_to(b_ref[...], (CONV_ROW_CHUNK, D_CONV))
        for j in range(CONV_WIDTH):
            r = (first + j) % SUBLANES
            a = c + first + j - r
            src = ext_scr[a:a + CONV_ROW_CHUNK, :] if r == 0 else sh_scr[r, a:a + CONV_ROW_CHUNK, :]
            acc = acc + jnp.concatenate([w_ref[j]] * (CONV_ROW_CHUNK // SUBLANES), axis=0) * src
        o_ref[c:c + CONV_ROW_CHUNK, :] = _ln_swish(acc, g_ref[...], beta_ref[...]).astype(o_ref.dtype)


def _conv_prompt(u2d, w_dw, b_dw, ln_g, ln_b, batch, seq, tm=128):
    nt = seq // tm
    halo_per_tile = tm // CONV_HALO
    kern = functools.partial(_conv_prompt_kernel, tm=tm)
    vec = pl.BlockSpec((1, D_CONV), lambda b, i: (0, 0))
    w_rep = jnp.broadcast_to(w_dw[:, None, :], (CONV_WIDTH, SUBLANES, D_CONV))
    return pl.pallas_call(
        kern,
        grid=(batch, nt),
        in_specs=[
            pl.BlockSpec((CONV_HALO, D_CONV),
                         lambda b, i: (jnp.maximum((b * nt + i) * halo_per_tile - 1, 0), 0)),
            pl.BlockSpec((tm, D_CONV), lambda b, i: (b * nt + i, 0)),
            pl.BlockSpec((CONV_WIDTH, SUBLANES, D_CONV), lambda b, i: (0, 0, 0)),
            vec, vec, vec,
        ],
        out_specs=pl.BlockSpec((tm, D_CONV), lambda b, i: (b * nt + i, 0)),
        out_shape=jax.ShapeDtypeStruct((batch * seq, D_CONV), BF16),
        scratch_shapes=[pltpu.VMEM((CONV_HALO + tm, D_CONV), F32),
                        pltpu.VMEM((SUBLANES, CONV_HALO + tm, D_CONV), F32)],
        compiler_params=_cparams(("parallel", "parallel")),
        name="conv_prompt",
    )(u2d, u2d, w_rep, b_dw, ln_g, ln_b)


def _conv_sample_kernel(state_ref, u_ref, w_ref, b_ref, g_ref, beta_ref, o_ref, st_ref, *, dec_batch, dec_seq):
    n_state = CONV_WIDTH - 1

    def ext(i):
        return state_ref[i] if i < n_state else u_ref[i - n_state]

    for t in range(dec_seq):
        acc = jnp.broadcast_to(b_ref[...], (dec_batch, D_CONV))
        for j in range(CONV_WIDTH):
            acc = acc + w_ref[j:j + 1, :] * ext(t + j)
        o_ref[t] = _ln_swish(acc, g_ref[...], beta_ref[...])
    for i in range(n_state):
        st_ref[i] = ext(i + dec_seq)


def _conv_sample(state_tm, u_s, w_dw, b_dw, ln_g, ln_b, dec_batch, dec_seq):
    kern = functools.partial(_conv_sample_kernel, dec_batch=dec_batch, dec_seq=dec_seq)
    u_tm = jnp.transpose(u_s.reshape(dec_batch, dec_seq, D_CONV), (1, 0, 2))
    y_tm, st_tm = pl.pallas_call(
        kern,
        out_shape=[
            jax.ShapeDtypeStruct((dec_seq, dec_batch, D_CONV), F32),
            jax.ShapeDtypeStruct(state_tm.shape, F32),
        ],
        compiler_params=pltpu.CompilerParams(vmem_limit_bytes=VMEM_LIMIT_BYTES),
        name="conv_sample",
    )(state_tm, u_tm, w_dw, b_dw, ln_g, ln_b)
    return jnp.transpose(y_tm, (1, 0, 2)).reshape(dec_batch * dec_seq, D_CONV), st_tm


def _post_first(ya_ref, yc_ref, x_ref, wo_ref, gpm_ref, gpf_ref, h_scr, hn_scr, acc_scr):
    m = (jnp.dot(ya_ref[...].astype(BF16), wo_ref[0:D_ATTN, :], preferred_element_type=F32)
         + jnp.dot(yc_ref[...].astype(BF16), wo_ref[D_ATTN:, :], preferred_element_type=F32))
    h = x_ref[...] + _rms(m, gpm_ref[...])
    h_scr[...] = h
    hn_scr[...] = _rms(h, gpf_ref[...]).astype(BF16)
    acc_scr[...] = jnp.zeros(acc_scr.shape, F32)


def _post_ffn_chunk(w1_ref, w2_ref, hn_scr, acc_scr):
    a = jnp.dot(hn_scr[...], w1_ref[...], preferred_element_type=F32)
    a = jnp.square(jnp.maximum(a, 0.0)).astype(BF16)
    acc_scr[...] += jnp.dot(a, w2_ref[...], preferred_element_type=F32)


def _post_last(pe_ref, gpost_ref, wg_ref, wp_ref, o_ref, h_scr, acc_scr):
    h2 = h_scr[...] + _rms(acc_scr[...], gpost_ref[...])
    gate = jax.nn.sigmoid(jnp.dot(h2.astype(BF16), wg_ref[...], preferred_element_type=F32))
    pev = jnp.dot(pe_ref[...].astype(BF16), wp_ref[...], preferred_element_type=F32)
    o_ref[...] = h2 + gate * pev


def _post_kernel(ya_ref, yc_ref, x_ref, pe_ref, wo_ref, gpm_ref, gpf_ref, w1_ref, w2_ref, gpost_ref,
                 wg_ref, wp_ref, o_ref, h_scr, hn_scr, acc_scr):
    k = pl.program_id(1)

    @pl.when(k == 0)
    def _():
        _post_first(ya_ref, yc_ref, x_ref, wo_ref, gpm_ref, gpf_ref, h_scr, hn_scr, acc_scr)

    _post_ffn_chunk(w1_ref, w2_ref, hn_scr, acc_scr)

    @pl.when(k == pl.num_programs(1) - 1)
    def _():
        _post_last(pe_ref, gpost_ref, wg_ref, wp_ref, o_ref, h_scr, acc_scr)


N_POST_IN = 12
N_PAGED_IN = 8


def _post_paged_kernel(pt_ref, *refs, pps, past_len, dec_seq, lam_init):
    (ya_ref, yc_ref, x_ref, pe_ref, wo_ref, gpm_ref, gpf_ref, w1_ref, w2_ref, gpost_ref, wg_ref,
     wp_ref) = refs[:N_POST_IN]
    lam_refs = refs[N_POST_IN:N_POST_IN + 4]
    gsub_ref, q_ref, ktn_ref, vin_ref = refs[N_POST_IN + 4:N_POST_IN + N_PAGED_IN]
    pages = refs[N_POST_IN + N_PAGED_IN:]
    ktp_refs, vip_refs = pages[:pps], pages[pps:2 * pps]
    o_ref, oa_ref = pages[2 * pps:2 * pps + 2]
    h_scr, hn_scr, acc_scr, q2_scr, qbd_scr, m_scr, l_scr, pacc_scr = pages[2 * pps + 2:]
    i = pl.program_id(0)
    k = pl.program_id(1)

    @pl.when(k == 0)
    def _():
        _post_first(ya_ref, yc_ref, x_ref, wo_ref, gpm_ref, gpf_ref, h_scr, hn_scr, acc_scr)
        _paged_init(q_ref, q2_scr, qbd_scr, m_scr, l_scr, pacc_scr, dec_seq=dec_seq)

    _post_ffn_chunk(w1_ref, w2_ref, hn_scr, acc_scr)
    _paged_pages(k, ktp_refs, vip_refs, qbd_scr, m_scr, l_scr, pacc_scr, past_len=past_len, dec_seq=dec_seq)

    @pl.when(k == pl.num_programs(1) - 1)
    def _():
        _post_last(pe_ref, gpost_ref, wg_ref, wp_ref, o_ref, h_scr, acc_scr)
        _paged_finish(i, lam_refs, gsub_ref, ktn_ref, vin_ref, oa_ref, qbd_scr, m_scr, l_scr, pacc_scr,
                      dec_seq=dec_seq, lam_init=lam_init)


POST_ROW_TILE = 512
POST_FF_CHUNK = 1024


def _post_specs(tm, tf, imap):
    row_tile = lambda n: pl.BlockSpec((tm, n), imap(lambda i, k: (i, 0)))
    vec = pl.BlockSpec((1, D_MODEL), imap(lambda i, k: (0, 0)))
    resident = lambda r, c: pl.BlockSpec((r, c), imap(lambda i, k: (0, 0)), pipeline_mode=pl.Buffered(1))
    in_specs = [
        row_tile(D_ATTN), row_tile(D_CONV), row_tile(D_MODEL), row_tile(D_PLE),
        resident(D_MODEL, D_MODEL), vec, vec,
        pl.BlockSpec((D_MODEL, tf), imap(lambda i, k: (0, k))),
        pl.BlockSpec((tf, D_MODEL), imap(lambda i, k: (k, 0))),
        vec,
        resident(D_MODEL, D_MODEL), resident(D_PLE, D_MODEL),
    ]
    scratch = [pltpu.VMEM((tm, D_MODEL), F32), pltpu.VMEM((tm, D_MODEL), BF16), pltpu.VMEM((tm, D_MODEL), F32)]
    return in_specs, row_tile(D_MODEL), scratch


def _post(post_args, tm):
    m = post_args[2].shape[0]
    in_specs, out_spec, scratch = _post_specs(tm, POST_FF_CHUNK, lambda f: f)
    return pl.pallas_call(
        _post_kernel,
        grid=(m // tm, D_FF // POST_FF_CHUNK),
        in_specs=in_specs,
        out_specs=out_spec,
        out_shape=jax.ShapeDtypeStruct((m, D_MODEL), F32),
        scratch_shapes=scratch,
        compiler_params=_cparams(("parallel", "arbitrary")),
        name="post",
    )(*post_args)


def _post_paged(post_args, q_s, kt_new, vi_new, kt_pool, vi_pool, page_table, lam_vecs, g_subln,
                dec_batch, dec_seq, lam_init):
    tm, tf = POST_ROW_TILE, POST_FF_CHUNK
    m = post_args[2].shape[0]
    n_row_tiles, n_ff = m // tm, D_FF // tf
    n_pages = page_table.shape[1]
    assert n_row_tiles == dec_batch and n_pages % n_ff == 0
    pps = n_pages // n_ff
    assert pps % PAGED_CHAINS == 0
    past_len = n_pages * PAGE_SIZE
    rph = 2 * dec_seq
    n_rows = N_HEADS * rph
    n_new = dec_batch * dec_seq
    q3 = q_s.astype(F32).reshape(dec_batch, dec_seq, D_ATTN)
    with_pt = lambda f: (lambda i, k, pt: f(i, k))
    post_in, post_out, post_scratch = _post_specs(tm, tf, with_pt)
    vec = lambda n: pl.BlockSpec((1, n), lambda i, k, pt: (0, 0))

    def kpage_spec(p):
        return pl.BlockSpec((None, D_ATTN, PAGE_SIZE), lambda i, k, pt: (pt[i, k * pps + p], 0, 0))

    def vpage_spec(p):
        return pl.BlockSpec((None, PAGE_SIZE * N_HEADS, V_DIM), lambda i, k, pt: (pt[i, k * pps + p], 0, 0))

    grid_spec = pltpu.PrefetchScalarGridSpec(
        num_scalar_prefetch=1,
        grid=(n_row_tiles, n_ff),
        in_specs=post_in + [vec(HEAD_DIM)] * 4 + [
            vec(V_DIM),
            pl.BlockSpec((None, dec_seq, D_ATTN), lambda i, k, pt: (i, 0, 0)),
            pl.BlockSpec((D_ATTN, n_new), lambda i, k, pt: (0, 0)),
            pl.BlockSpec((n_new * N_HEADS, V_DIM), lambda i, k, pt: (0, 0)),
        ] + [kpage_spec(p) for p in range(pps)] + [vpage_spec(p) for p in range(pps)],
        out_specs=[post_out, pl.BlockSpec((None, dec_seq, D_ATTN), lambda i, k, pt: (i, 0, 0))],
        scratch_shapes=post_scratch + [
            pltpu.VMEM((rph, D_ATTN), F32),
            pltpu.VMEM((n_rows, D_ATTN), BF16),
            pltpu.VMEM((PAGED_CHAINS, n_rows, 1), F32),
            pltpu.VMEM((PAGED_CHAINS, n_rows, 1), F32),
            pltpu.VMEM((PAGED_CHAINS, n_rows, V_DIM), F32),
        ],
    )
    kern = functools.partial(_post_paged_kernel, pps=pps, past_len=past_len, dec_seq=dec_seq, lam_init=lam_init)
    out, ya_s = pl.pallas_call(
        kern,
        grid_spec=grid_spec,
        out_shape=[jax.ShapeDtypeStruct((m, D_MODEL), F32),
                   jax.ShapeDtypeStruct((dec_batch, dec_seq, D_ATTN), F32)],
        compiler_params=_cparams(("parallel", "arbitrary")),
        name="post_paged",
    )(page_table, *post_args, *lam_vecs, g_subln, q3, kt_new, vi_new, *([kt_pool] * pps), *([vi_pool] * pps))
    return out, ya_s.reshape(dec_batch * dec_seq, D_ATTN)


def _row(v):
    return v.reshape(1, -1)


def kernel(x_prompt, x_sample, cache_k, cache_v, state_conv, page_table, p_prompt, p_sample,
           w_in, w_out, lambda_q1, lambda_k1, lambda_q2, lambda_k2, g_subln, w_dw, b_dw,
           ln_conv_g, ln_conv_b, g_pre_mix, g_post_mix, g_pre_ffn, g_post_ffn,
           w_ff1, w_ff2, w_ple, w_ple_gate):
    depth = w_in.shape[0]
    batch, seq, _ = x_prompt.shape
    dec_batch, dec_seq, _ = x_sample.shape
    n_pool = cache_k.shape[1]
    mp, ms = batch * seq, dec_batch * dec_seq
    assert 2 * dec_seq == 8 and cache_k.shape[2] == PAGE_SIZE

    hp = x_prompt.reshape(mp, D_MODEL)
    hs = x_sample.reshape(ms, D_MODEL)
    outs = {k: [] for k in ("kp", "vp", "cp", "ks", "vs", "cs")}
    for l in range(depth):
        lam_init = 0.8 - 0.6 * math.exp(-0.3 * l)
        w_in16 = w_in[l].astype(BF16)
        w_kt16 = w_in[l][:, D_ATTN:2 * D_ATTN].T.astype(BF16)
        w_out16 = w_out[l].astype(BF16)
        w_ff1_16 = w_ff1[l].astype(BF16)
        w_ff2_16 = w_ff2[l].astype(BF16)
        w_gate16 = w_ple_gate[l].astype(BF16)
        w_ple16 = w_ple[l].astype(BF16)
        lam_vecs = [_row(lambda_q1[l]), _row(lambda_k1[l]), _row(lambda_q2[l]), _row(lambda_k2[l])]
        gsub = _row(g_subln[l])
        conv_args = (w_dw[l], _row(b_dw[l]), _row(ln_conv_g[l]), _row(ln_conv_b[l]))
        kt_pool = jnp.transpose(cache_k[l], (0, 2, 3, 4, 1)).reshape(n_pool, D_ATTN, PAGE_SIZE)
        vi_pool = cache_v[l].reshape(n_pool, PAGE_SIZE * N_HEADS, V_DIM)
        post_w = (w_out16, _row(g_post_mix[l]), _row(g_pre_ffn[l]), w_ff1_16, w_ff2_16, _row(g_post_ffn[l]),
                  w_gate16, w_ple16)

        q_p, kt_p, vi_p, v16_p, u_p = _inproj(hp, _row(g_pre_mix[l]), w_in16, w_kt16, n_seq=batch, tm=512)
        ya_p = _attn_prompt(q_p, kt_p, v16_p, lam_vecs, gsub, batch, seq, lam_init)
        yc_p = _conv_prompt(u_p, *conv_args, batch, seq)
        q_s, kt_s, vi_s, _, u_s = _inproj(hs, _row(g_pre_mix[l]), w_in16, w_kt16, n_seq=1, tm=ms)
        yc_s, st_tm = _conv_sample(jnp.transpose(state_conv[l], (1, 0, 2)), u_s, *conv_args,
                                   dec_batch, dec_seq)

        hp, ya_s = _post_paged((ya_p, yc_p, hp, p_prompt[l].reshape(mp, D_PLE)) + post_w,
                               q_s, kt_s[0], vi_s, kt_pool, vi_pool, page_table, lam_vecs, gsub,
                               dec_batch, dec_seq, lam_init)
        hs = _post((ya_s, yc_s, hs, p_sample[l].reshape(ms, D_PLE)) + post_w, tm=ms)

        outs["kp"].append(jnp.transpose(kt_p.reshape(batch, N_HEADS, 2, HEAD_DIM, seq), (0, 4, 1, 2, 3)))
        outs["vp"].append(vi_p.reshape(batch, seq, N_HEADS, V_DIM))
        outs["cp"].append(u_p.reshape(batch, seq, D_CONV)[:, seq - (CONV_WIDTH - 1):])
        outs["ks"].append(jnp.transpose(kt_s.reshape(N_HEADS, 2, HEAD_DIM, dec_batch, dec_seq),
                                        (3, 4, 0, 1, 2)))
        outs["vs"].append(vi_s.reshape(dec_batch, dec_seq, N_HEADS, V_DIM))
        outs["cs"].append(jnp.transpose(st_tm, (1, 0, 2)))

    return (hp.reshape(batch, seq, D_MODEL), hs.reshape(dec_batch, dec_seq, D_MODEL),
            jnp.stack(outs["kp"]), jnp.stack(outs["vp"]), jnp.stack(outs["cp"]),
            jnp.stack(outs["ks"]), jnp.stack(outs["vs"]), jnp.stack(outs["cs"]))
```

```python
import functools
import math

import jax
import jax.numpy as jnp
from jax import lax
from jax.experimental import pallas as pl
from jax.experimental.pallas import tpu as pltpu

D_MODEL = 1024
D_ATTN = 512
D_CONV = 512
N_HEADS = 4
HEAD_DIM = 64
V_DIM = 128
CONV_WIDTH = 31
D_FF = 4096
D_PLE = 256
PAGE_SIZE = 128
EPS = 1e-6
N_QKV = 3 * D_ATTN
N_IN = N_QKV + 2 * D_CONV
NEG_BIG = -1e30
LOG2E = math.log2(math.e)
Q_SCALE = HEAD_DIM ** -0.5 * LOG2E
ALIBI_SLOPES = tuple(2.0 ** (-8.0 * (h + 1) / N_HEADS) * LOG2E for h in range(N_HEADS))

F32 = jnp.float32
BF16 = jnp.bfloat16

VMEM_LIMIT_BYTES = 56 * 1024 * 1024


def _cparams(semantics):
    return pltpu.CompilerParams(dimension_semantics=semantics, vmem_limit_bytes=VMEM_LIMIT_BYTES)


def _rms(x, g):
    return x * lax.rsqrt(jnp.mean(x * x, axis=-1, keepdims=True) + EPS) * g


def _diff_lambda(lq1, lk1, lq2, lk2, lam_init):
    return (jnp.exp(jnp.sum(lq1 * lk1, axis=-1, keepdims=True))
            - jnp.exp(jnp.sum(lq2 * lk2, axis=-1, keepdims=True)) + lam_init)


def _inproj_kernel(x_ref, g_ref, w_ref, q_ref, kt_ref, vi_ref, v16_ref, u_ref, *, tm):
    xn = _rms(x_ref[...], g_ref[...]).astype(BF16)
    q = jnp.dot(xn, w_ref[:, 0:D_ATTN], preferred_element_type=F32)
    q_ref[...] = (q * Q_SCALE).astype(BF16)
    kt_ref[...] = jnp.dot(xn, w_ref[:, D_ATTN:2 * D_ATTN], preferred_element_type=F32).T
    v = jnp.dot(xn, w_ref[:, 2 * D_ATTN:N_QKV], preferred_element_type=F32)
    v16_ref[...] = v.astype(BF16)
    for h in range(N_HEADS):
        vi_ref[pl.ds(h, tm, stride=N_HEADS), :] = v[:, h * V_DIM:(h + 1) * V_DIM]
    a = jnp.dot(xn, w_ref[:, N_QKV:N_QKV + D_CONV], preferred_element_type=F32)
    gt = jnp.dot(xn, w_ref[:, N_QKV + D_CONV:], preferred_element_type=F32)
    u_ref[...] = a * jax.nn.sigmoid(gt)


def _inproj(x2d, g_pre_mix, w_in16, n_seq, tm):
    m = x2d.shape[0]
    seq = m // n_seq
    nt = seq // tm
    return pl.pallas_call(
        functools.partial(_inproj_kernel, tm=tm),
        grid=(m // tm,),
        in_specs=[
            pl.BlockSpec((tm, D_MODEL), lambda i: (i, 0)),
            pl.BlockSpec((1, D_MODEL), lambda i: (0, 0)),
            pl.BlockSpec((D_MODEL, N_IN), lambda i: (0, 0)),
        ],
        out_specs=[
            pl.BlockSpec((tm, D_ATTN), lambda i: (i, 0)),
            pl.BlockSpec((None, D_ATTN, tm), lambda i: (i // nt, 0, i % nt)),
            pl.BlockSpec((tm * N_HEADS, V_DIM), lambda i: (i, 0)),
            pl.BlockSpec((tm, D_ATTN), lambda i: (i, 0)),
            pl.BlockSpec((tm, D_CONV), lambda i: (i, 0)),
        ],
        out_shape=[
            jax.ShapeDtypeStruct((m, D_ATTN), BF16),
            jax.ShapeDtypeStruct((n_seq, D_ATTN, seq), F32),
            jax.ShapeDtypeStruct((m * N_HEADS, V_DIM), F32),
            jax.ShapeDtypeStruct((m, D_ATTN), BF16),
            jax.ShapeDtypeStruct((m, D_CONV), F32),
        ],
        compiler_params=_cparams(("parallel",)),
        name="inproj",
    )(x2d, g_pre_mix, w_in16)


ATTN_TILE = 256
ATTN_ROW_BLOCK = 64
LANES = 128


def _attn_prompt_kernel(lq1_ref, lk1_ref, lq2_ref, lk2_ref, gsub_ref, q_ref, kt_ref, v_ref, o_ref,
                        qs_scr, m_scr, l_scr, acc_scr, *, lam_init):
    t, rb = ATTN_TILE, ATTN_ROW_BLOCK
    n_rb = t // rb
    qi = pl.program_id(1)
    lane = lax.broadcasted_iota(jnp.int32, (rb, 2 * HEAD_DIM), 1)
    kcol = lax.broadcasted_iota(jnp.int32, (1, t), 1)

    for h in range(N_HEADS):
        hs = slice(h * V_DIM, (h + 1) * V_DIM)
        for g in range(n_rb):
            q = q_ref[g * rb:(g + 1) * rb, hs]
            zero = jnp.zeros_like(q)
            qs_scr[h, g] = jnp.concatenate([jnp.where(lane < HEAD_DIM, q, zero),
                                            jnp.where(lane >= HEAD_DIM, q, zero)], axis=0)
    m_scr[...] = jnp.full(m_scr.shape, NEG_BIG, F32)
    l_scr[...] = jnp.zeros(l_scr.shape, F32)
    acc_scr[...] = jnp.zeros(acc_scr.shape, F32)

    def tile(kv, diagonal):
        k0 = pl.multiple_of(kv * t, t)
        koff = (kcol + (kv - qi) * t).astype(F32)
        for h in range(N_HEADS):
            hs = slice(h * V_DIM, (h + 1) * V_DIM)
            kt = kt_ref[hs, pl.ds(k0, t)].astype(BF16)
            vv = v_ref[pl.ds(k0, t), hs]
            bias = ALIBI_SLOPES[h] * koff
            for g in range(n_rb):
                s = jnp.dot(qs_scr[h, g], kt, preferred_element_type=F32) + bias
                if diagonal:
                    row = lax.broadcasted_iota(jnp.int32, (2 * rb, t), 0)
                    row = jnp.where(row >= rb, row - rb, row) + g * rb
                    col = lax.broadcasted_iota(jnp.int32, (2 * rb, t), 1)
                    s = jnp.where(col <= row, s, NEG_BIG)
                m_prev = m_scr[h, g]
                m_new = jnp.maximum(m_prev, jnp.max(s, axis=-1, keepdims=True))
                alpha = jnp.exp2(m_prev - m_new)
                p = jnp.exp2(s - jnp.concatenate([m_new] * (t // LANES), axis=-1))
                l_scr[h, g] = alpha * l_scr[h, g] + jnp.sum(p, axis=-1, keepdims=True)
                acc_scr[h, g] = alpha * acc_scr[h, g] + jnp.dot(p.astype(BF16), vv,
                                                               preferred_element_type=F32)
                m_scr[h, g] = m_new

    def below_diagonal(kv, carry):
        tile(kv, False)
        return carry

    lax.fori_loop(0, qi, below_diagonal, 0)
    tile(qi, True)

    lam = _diff_lambda(lq1_ref[...], lk1_ref[...], lq2_ref[...], lk2_ref[...], lam_init)
    for h in range(N_HEADS):
        for g in range(n_rb):
            o = acc_scr[h, g] / l_scr[h, g]
            oh = o[:rb] - lam * o[rb:]
            y = _rms(oh, gsub_ref[...]) * (1.0 - lam_init)
            o_ref[g * rb:(g + 1) * rb, h * V_DIM:(h + 1) * V_DIM] = y.astype(o_ref.dtype)


def _attn_prompt(q16, kt, v16, lam_vecs, g_subln, batch, seq, lam_init):
    t, rb = ATTN_TILE, ATTN_ROW_BLOCK
    nq = seq // t
    n_rb = t // rb
    kern = functools.partial(_attn_prompt_kernel, lam_init=lam_init)
    vec = lambda n: pl.BlockSpec((1, n), lambda b, qi: (0, 0))
    return pl.pallas_call(
        kern,
        grid=(batch, nq),
        in_specs=[vec(HEAD_DIM)] * 4 + [
            vec(V_DIM),
            pl.BlockSpec((t, D_ATTN), lambda b, qi: (b * nq + qi, 0)),
            pl.BlockSpec((None, D_ATTN, seq), lambda b, qi: (b, 0, 0)),
            pl.BlockSpec((seq, D_ATTN), lambda b, qi: (b, 0)),
        ],
        out_specs=pl.BlockSpec((t, D_ATTN), lambda b, qi: (b * nq + qi, 0)),
        out_shape=jax.ShapeDtypeStruct((batch * seq, D_ATTN), BF16),
        scratch_shapes=[
            pltpu.VMEM((N_HEADS, n_rb, 2 * rb, 2 * HEAD_DIM), BF16),
            pltpu.VMEM((N_HEADS, n_rb, 2 * rb, LANES), F32),
            pltpu.VMEM((N_HEADS, n_rb, 2 * rb, LANES), F32),
            pltpu.VMEM((N_HEADS, n_rb, 2 * rb, V_DIM), F32),
        ],
        compiler_params=_cparams(("parallel", "arbitrary")),
        name="attn_prompt",
    )(*lam_vecs, g_subln, q16, kt, v16)


PAGED_CHAINS = 1


def _paged_rows(dec_seq):
    rph = 2 * dec_seq
    n_rows = N_HEADS * rph
    row = lax.broadcasted_iota(jnp.int32, (n_rows, 1), 0)
    head = row // rph
    comp = (row % rph) // dec_seq
    tok = row % dec_seq
    slope = jnp.zeros((n_rows, 1), F32)
    for h in range(N_HEADS):
        slope = jnp.where(head == h, ALIBI_SLOPES[h], slope)
    return rph, n_rows, head, comp, tok, slope


def _paged_init(q_ref, q2_scr, qbd_scr, m_scr, l_scr, acc_scr, *, dec_seq):
    rph, n_rows, head, comp, _, _ = _paged_rows(dec_seq)
    q = q_ref[...]
    q2_scr[0:dec_seq, :] = q
    q2_scr[dec_seq:rph, :] = q
    qrep = jnp.concatenate([q2_scr[...]] * N_HEADS, axis=0)
    col = lax.broadcasted_iota(jnp.int32, (n_rows, D_ATTN), 1)
    qbd_scr[...] = jnp.where(col // HEAD_DIM == head * 2 + comp, qrep, 0.0).astype(BF16)
    m_scr[...] = jnp.full(m_scr.shape, NEG_BIG, F32)
    l_scr[...] = jnp.zeros(l_scr.shape, F32)
    acc_scr[...] = jnp.zeros(acc_scr.shape, F32)


def _paged_update(c, s, v_refs, n_slots, m_scr, l_scr, acc_scr, rph):
    m_prev = m_scr[c]
    m_new = jnp.maximum(m_prev, jnp.max(s, axis=-1, keepdims=True))
    alpha = jnp.exp2(m_prev - m_new)
    p = jnp.exp2(s - m_new)
    l_scr[c] = alpha * l_scr[c] + jnp.sum(p, axis=-1, keepdims=True)
    p16 = p.astype(BF16)
    for h in range(N_HEADS):
        rs = slice(h * rph, (h + 1) * rph)
        vh = jnp.concatenate([r[pl.ds(h, n_slots, stride=N_HEADS), :] for r in v_refs], axis=0)
        pv = jnp.dot(p16[rs, :], vh.astype(BF16), preferred_element_type=F32)
        acc_scr[c, rs, :] = alpha[rs] * acc_scr[c, rs, :] + pv
    m_scr[c] = m_new


def _paged_pages(j, ktp_refs, vip_refs, qbd_scr, m_scr, l_scr, acc_scr, *, past_len, dec_seq):
    rph, _, _, _, _, slope = _paged_rows(dec_seq)
    pps = len(ktp_refs)
    ppc = pps // PAGED_CHAINS
    qbd = qbd_scr[...]
    kcol = lax.broadcasted_iota(jnp.int32, (1, ppc * PAGE_SIZE), 1)
    for c in range(PAGED_CHAINS):
        pages = range(c * ppc, (c + 1) * ppc)
        s = jnp.concatenate(
            [jnp.dot(qbd, ktp_refs[i][...].astype(BF16), preferred_element_type=F32) for i in pages], axis=-1)
        koff = (kcol + ((j * pps + c * ppc) * PAGE_SIZE - past_len)).astype(F32)
        _paged_update(c, s + slope * koff, [vip_refs[i] for i in pages], PAGE_SIZE, m_scr, l_scr, acc_scr, rph)


def _paged_finish(b, lam_refs, gsub_ref, ktn_ref, vin_ref, o_ref, qbd_scr, m_scr, l_scr, acc_scr, *,
                  dec_seq, lam_init):
    rph, n_rows, _, _, tok, slope = _paged_rows(dec_seq)
    n_new = ktn_ref.shape[1]
    sn = jnp.dot(qbd_scr[...], ktn_ref[...].astype(BF16), preferred_element_type=F32)
    col = lax.broadcasted_iota(jnp.int32, (n_rows, n_new), 1)
    ctok = col % dec_seq
    valid = jnp.logical_and(col // dec_seq == b, ctok <= tok)
    sn = jnp.where(valid, sn + slope * ctok.astype(F32), NEG_BIG)
    _paged_update(0, sn, [vin_ref], n_new, m_scr, l_scr, acc_scr, rph)

    m = m_scr[0]
    for c in range(1, PAGED_CHAINS):
        m = jnp.maximum(m, m_scr[c])
    l = jnp.zeros((n_rows, 1), F32)
    acc = jnp.zeros((n_rows, V_DIM), F32)
    for c in range(PAGED_CHAINS):
        w = jnp.exp2(m_scr[c] - m)
        l = l + w * l_scr[c]
        acc = acc + w * acc_scr[c]

    lam = _diff_lambda(*[r[...] for r in lam_refs], lam_init)
    o = acc / l
    for h in range(N_HEADS):
        oh = o[h * rph:h * rph + dec_seq] - lam * o[h * rph + dec_seq:(h + 1) * rph]
        o_ref[:, h * V_DIM:(h + 1) * V_DIM] = _rms(oh, gsub_ref[...]) * (1.0 - lam_init)


SUBLANES = 8
CONV_HALO = 32
CONV_ROW_CHUNK = 16


def _ln_swish(c, g, b):
    mu = jnp.mean(c, axis=-1, keepdims=True)
    d = c - mu
    var = jnp.mean(d * d, axis=-1, keepdims=True)
    y = d * lax.rsqrt(var + EPS) * g + b
    return y * jax.nn.sigmoid(y)


def _conv_prompt_kernel(prev_ref, cur_ref, w_ref, b_ref, g_ref, beta_ref, o_ref, ext_scr, sh_scr, *, tm):
    i = pl.program_id(1)
    n_ext = CONV_HALO + tm
    ext_scr[0:CONV_HALO, :] = jnp.where(i == 0, 0.0, prev_ref[...])
    ext_scr[CONV_HALO:n_ext, :] = cur_ref[...]
    ext = ext_scr[...]
    for r in range(1, SUBLANES):
        sh_scr[r] = pltpu.roll(ext, n_ext - r, axis=0)
    first = CONV_HALO - (CONV_WIDTH - 1)
    for c in range(0, tm, CONV_ROW_CHUNK):
        acc = jnp.broadcast_to(b_ref[...], (CONV_ROW_CHUNK, D_CONV))
        for j in range(CONV_WIDTH):
            r = (first + j) % SUBLANES
            a = c + first + j - r
            src = ext_scr[a:a + CONV_ROW_CHUNK, :] if r == 0 else sh_scr[r, a:a + CONV_ROW_CHUNK, :]
            acc = acc + jnp.concatenate([w_ref[j]] * (CONV_ROW_CHUNK // SUBLANES), axis=0) * src
        o_ref[c:c + CONV_ROW_CHUNK, :] = _ln_swish(acc, g_ref[...], beta_ref[...]).astype(o_ref.dtype)


def _conv_prompt(u2d, w_dw, b_dw, ln_g, ln_b, batch, seq, tm=512):
    nt = seq // tm
    halo_per_tile = tm // CONV_HALO
    kern = functools.partial(_conv_prompt_kernel, tm=tm)
    vec = pl.BlockSpec((1, D_CONV), lambda b, i: (0, 0))
    w_rep = jnp.broadcast_to(w_dw[:, None, :], (CONV_WIDTH, SUBLANES, D_CONV))
    return pl.pallas_call(
        kern,
        grid=(batch, nt),
        in_specs=[
            pl.BlockSpec((CONV_HALO, D_CONV),
                         lambda b, i: (jnp.maximum((b * nt + i) * halo_per_tile - 1, 0), 0)),
            pl.BlockSpec((tm, D_CONV), lambda b, i: (b * nt + i, 0)),
            pl.BlockSpec((CONV_WIDTH, SUBLANES, D_CONV), lambda b, i: (0, 0, 0)),
            vec, vec, vec,
        ],
        out_specs=pl.BlockSpec((tm, D_CONV), lambda b, i: (b * nt + i, 0)),
        out_shape=jax.ShapeDtypeStruct((batch * seq, D_CONV), BF16),
        scratch_shapes=[pltpu.VMEM((CONV_HALO + tm, D_CONV), F32),
                        pltpu.VMEM((SUBLANES, CONV_HALO + tm, D_CONV), F32)],
        compiler_params=_cparams(("parallel", "parallel")),
        name="conv_prompt",
    )(u2d, u2d, w_rep, b_dw, ln_g, ln_b)


def _conv_sample_kernel(state_ref, u_ref, w_ref, b_ref, g_ref, beta_ref, o_ref, st_ref, *, dec_batch, dec_seq):
    n_state = CONV_WIDTH - 1

    def ext(i):
        return state_ref[i] if i < n_state else u_ref[i - n_state]

    for t in range(dec_seq):
        acc = jnp.broadcast_to(b_ref[...], (dec_batch, D_CONV))
        for j in range(CONV_WIDTH):
            acc = acc + w_ref[j:j + 1, :] * ext(t + j)
        o_ref[t] = _ln_swish(acc, g_ref[...], beta_ref[...])
    for i in range(n_state):
        st_ref[i] = ext(i + dec_seq)


def _conv_sample(state_tm, u_s, w_dw, b_dw, ln_g, ln_b, dec_batch, dec_seq):
    kern = functools.partial(_conv_sample_kernel, dec_batch=dec_batch, dec_seq=dec_seq)
    u_tm = jnp.transpose(u_s.reshape(dec_batch, dec_seq, D_CONV), (1, 0, 2))
    y_tm, st_tm = pl.pallas_call(
        kern,
        out_shape=[
            jax.ShapeDtypeStruct((dec_seq, dec_batch, D_CONV), F32),
            jax.ShapeDtypeStruct(state_tm.shape, F32),
        ],
        compiler_params=pltpu.CompilerParams(vmem_limit_bytes=VMEM_LIMIT_BYTES),
        name="conv_sample",
    )(state_tm, u_tm, w_dw, b_dw, ln_g, ln_b)
    return jnp.transpose(y_tm, (1, 0, 2)).reshape(dec_batch * dec_seq, D_CONV), st_tm


def _post_first(ya_ref, yc_ref, x_ref, wo_ref, gpm_ref, gpf_ref, h_scr, hn_scr, acc_scr):
    m = (jnp.dot(ya_ref[...].astype(BF16), wo_ref[0:D_ATTN, :], preferred_element_type=F32)
         + jnp.dot(yc_ref[...].astype(BF16), wo_ref[D_ATTN:, :], preferred_element_type=F32))
    h = x_ref[...] + _rms(m, gpm_ref[...])
    h_scr[...] = h
    hn_scr[...] = _rms(h, gpf_ref[...]).astype(BF16)
    acc_scr[...] = jnp.zeros(acc_scr.shape, F32)


def _post_ffn_chunk(w1_ref, w2_ref, hn_scr, acc_scr):
    a = jnp.dot(hn_scr[...], w1_ref[...], preferred_element_type=F32)
    a = jnp.square(jnp.maximum(a, 0.0)).astype(BF16)
    acc_scr[...] += jnp.dot(a, w2_ref[...], preferred_element_type=F32)


def _post_last(pe_ref, gpost_ref, wg_ref, wp_ref, o_ref, h_scr, acc_scr):
    h2 = h_scr[...] + _rms(acc_scr[...], gpost_ref[...])
    gate = jax.nn.sigmoid(jnp.dot(h2.astype(BF16), wg_ref[...], preferred_element_type=F32))
    pev = jnp.dot(pe_ref[...].astype(BF16), wp_ref[...], preferred_element_type=F32)
    o_ref[...] = h2 + gate * pev


def _post_kernel(ya_ref, yc_ref, x_ref, pe_ref, wo_ref, gpm_ref, gpf_ref, w1_ref, w2_ref, gpost_ref,
                 wg_ref, wp_ref, o_ref, h_scr, hn_scr, acc_scr):
    k = pl.program_id(1)

    @pl.when(k == 0)
    def _():
        _post_first(ya_ref, yc_ref, x_ref, wo_ref, gpm_ref, gpf_ref, h_scr, hn_scr, acc_scr)

    _post_ffn_chunk(w1_ref, w2_ref, hn_scr, acc_scr)

    @pl.when(k == pl.num_programs(1) - 1)
    def _():
        _post_last(pe_ref, gpost_ref, wg_ref, wp_ref, o_ref, h_scr, acc_scr)


N_POST_IN = 12
N_PAGED_IN = 8


def _post_paged_kernel(pt_ref, *refs, pps, past_len, dec_seq, lam_init):
    (ya_ref, yc_ref, x_ref, pe_ref, wo_ref, gpm_ref, gpf_ref, w1_ref, w2_ref, gpost_ref, wg_ref,
     wp_ref) = refs[:N_POST_IN]
    lam_refs = refs[N_POST_IN:N_POST_IN + 4]
    gsub_ref, q_ref, ktn_ref, vin_ref = refs[N_POST_IN + 4:N_POST_IN + N_PAGED_IN]
    pages = refs[N_POST_IN + N_PAGED_IN:]
    ktp_refs, vip_refs = pages[:pps], pages[pps:2 * pps]
    o_ref, oa_ref = pages[2 * pps:2 * pps + 2]
    h_scr, hn_scr, acc_scr, q2_scr, qbd_scr, m_scr, l_scr, pacc_scr = pages[2 * pps + 2:]
    i = pl.program_id(0)
    k = pl.program_id(1)

    @pl.when(k == 0)
    def _():
        _post_first(ya_ref, yc_ref, x_ref, wo_ref, gpm_ref, gpf_ref, h_scr, hn_scr, acc_scr)
        _paged_init(q_ref, q2_scr, qbd_scr, m_scr, l_scr, pacc_scr, dec_seq=dec_seq)

    _post_ffn_chunk(w1_ref, w2_ref, hn_scr, acc_scr)
    _paged_pages(k, ktp_refs, vip_refs, qbd_scr, m_scr, l_scr, pacc_scr, past_len=past_len, dec_seq=dec_seq)

    @pl.when(k == pl.num_programs(1) - 1)
    def _():
        _post_last(pe_ref, gpost_ref, wg_ref, wp_ref, o_ref, h_scr, acc_scr)
        _paged_finish(i, lam_refs, gsub_ref, ktn_ref, vin_ref, oa_ref, qbd_scr, m_scr, l_scr, pacc_scr,
                      dec_seq=dec_seq, lam_init=lam_init)


POST_ROW_TILE = 512
POST_FF_CHUNK = 1024


def _post_specs(tm, tf, imap):
    row_tile = lambda n: pl.BlockSpec((tm, n), imap(lambda i, k: (i, 0)))
    vec = pl.BlockSpec((1, D_MODEL), imap(lambda i, k: (0, 0)))
    resident = lambda r, c: pl.BlockSpec((r, c), imap(lambda i, k: (0, 0)), pipeline_mode=pl.Buffered(1))
    in_specs = [
        row_tile(D_ATTN), row_tile(D_CONV), row_tile(D_MODEL), row_tile(D_PLE),
        resident(D_MODEL, D_MODEL), vec, vec,
        pl.BlockSpec((D_MODEL, tf), imap(lambda i, k: (0, k))),
        pl.BlockSpec((tf, D_MODEL), imap(lambda i, k: (k, 0))),
        vec,
        resident(D_MODEL, D_MODEL), resident(D_PLE, D_MODEL),
    ]
    scratch = [pltpu.VMEM((tm, D_MODEL), F32), pltpu.VMEM((tm, D_MODEL), BF16), pltpu.VMEM((tm, D_MODEL), F32)]
    return in_specs, row_tile(D_MODEL), scratch


def _post(post_args, tm):
    m = post_args[2].shape[0]
    in_specs, out_spec, scratch = _post_specs(tm, POST_FF_CHUNK, lambda f: f)
    return pl.pallas_call(
        _post_kernel,
        grid=(m // tm, D_FF // POST_FF_CHUNK),
        in_specs=in_specs,
        out_specs=out_spec,
        out_shape=jax.ShapeDtypeStruct((m, D_MODEL), F32),
        scratch_shapes=scratch,
        compiler_params=_cparams(("parallel", "arbitrary")),
        name="post",
    )(*post_args)


def _post_paged(post_args, q_s, kt_new, vi_new, kt_pool, vi_pool, page_table, lam_vecs, g_subln,
                dec_batch, dec_seq, lam_init):
    tm, tf = POST_ROW_TILE, POST_FF_CHUNK
    m = post_args[2].shape[0]
    n_row_tiles, n_ff = m // tm, D_FF // tf
    n_pages = page_table.shape[1]
    assert n_row_tiles == dec_batch and n_pages % n_ff == 0
    pps = n_pages // n_ff
    assert pps % PAGED_CHAINS == 0
    past_len = n_pages * PAGE_SIZE
    rph = 2 * dec_seq
    n_rows = N_HEADS * rph
    n_new = dec_batch * dec_seq
    q3 = q_s.astype(F32).reshape(dec_batch, dec_seq, D_ATTN)
    with_pt = lambda f: (lambda i, k, pt: f(i, k))
    post_in, post_out, post_scratch = _post_specs(tm, tf, with_pt)
    vec = lambda n: pl.BlockSpec((1, n), lambda i, k, pt: (0, 0))

    def kpage_spec(p):
        return pl.BlockSpec((None, D_ATTN, PAGE_SIZE), lambda i, k, pt: (pt[i, k * pps + p], 0, 0))

    def vpage_spec(p):
        return pl.BlockSpec((None, PAGE_SIZE * N_HEADS, V_DIM), lambda i, k, pt: (pt[i, k * pps + p], 0, 0))

    grid_spec = pltpu.PrefetchScalarGridSpec(
        num_scalar_prefetch=1,
        grid=(n_row_tiles, n_ff),
        in_specs=post_in + [vec(HEAD_DIM)] * 4 + [
            vec(V_DIM),
            pl.BlockSpec((None, dec_seq, D_ATTN), lambda i, k, pt: (i, 0, 0)),
            pl.BlockSpec((D_ATTN, n_new), lambda i, k, pt: (0, 0)),
            pl.BlockSpec((n_new * N_HEADS, V_DIM), lambda i, k, pt: (0, 0)),
        ] + [kpage_spec(p) for p in range(pps)] + [vpage_spec(p) for p in range(pps)],
        out_specs=[post_out, pl.BlockSpec((None, dec_seq, D_ATTN), lambda i, k, pt: (i, 0, 0))],
        scratch_shapes=post_scratch + [
            pltpu.VMEM((rph, D_ATTN), F32),
            pltpu.VMEM((n_rows, D_ATTN), BF16),
            pltpu.VMEM((PAGED_CHAINS, n_rows, 1), F32),
            pltpu.VMEM((PAGED_CHAINS, n_rows, 1), F32),
            pltpu.VMEM((PAGED_CHAINS, n_rows, V_DIM), F32),
        ],
    )
    kern = functools.partial(_post_paged_kernel, pps=pps, past_len=past_len, dec_seq=dec_seq, lam_init=lam_init)
    out, ya_s = pl.pallas_call(
        kern,
        grid_spec=grid_spec,
        out_shape=[jax.ShapeDtypeStruct((m, D_MODEL), F32),
                   jax.ShapeDtypeStruct((dec_batch, dec_seq, D_ATTN), F32)],
        compiler_params=_cparams(("parallel", "arbitrary")),
        name="post_paged",
    )(page_table, *post_args, *lam_vecs, g_subln, q3, kt_new, vi_new, *([kt_pool] * pps), *([vi_pool] * pps))
    return out, ya_s.reshape(dec_batch * dec_seq, D_ATTN)


def _row(v):
    return v.reshape(1, -1)


def kernel(x_prompt, x_sample, cache_k, cache_v, state_conv, page_table, p_prompt, p_sample,
           w_in, w_out, lambda_q1, lambda_k1, lambda_q2, lambda_k2, g_subln, w_dw, b_dw,
           ln_conv_g, ln_conv_b, g_pre_mix, g_post_mix, g_pre_ffn, g_post_ffn,
           w_ff1, w_ff2, w_ple, w_ple_gate):
    depth = w_in.shape[0]
    batch, seq, _ = x_prompt.shape
    dec_batch, dec_seq, _ = x_sample.shape
    n_pool = cache_k.shape[1]
    mp, ms = batch * seq, dec_batch * dec_seq
    assert 2 * dec_seq == 8 and cache_k.shape[2] == PAGE_SIZE

    hp = x_prompt.reshape(mp, D_MODEL)
    hs = x_sample.reshape(ms, D_MODEL)
    outs = {k: [] for k in ("kp", "vp", "cp", "ks", "vs", "cs")}
    for l in range(depth):
        lam_init = 0.8 - 0.6 * math.exp(-0.3 * l)
        w_in16 = w_in[l].astype(BF16)
        w_out16 = w_out[l].astype(BF16)
        w_ff1_16 = w_ff1[l].astype(BF16)
        w_ff2_16 = w_ff2[l].astype(BF16)
        w_gate16 = w_ple_gate[l].astype(BF16)
        w_ple16 = w_ple[l].astype(BF16)
        lam_vecs = [_row(lambda_q1[l]), _row(lambda_k1[l]), _row(lambda_q2[l]), _row(lambda_k2[l])]
        gsub = _row(g_subln[l])
        conv_args = (w_dw[l], _row(b_dw[l]), _row(ln_conv_g[l]), _row(ln_conv_b[l]))
        kt_pool = jnp.transpose(cache_k[l], (0, 2, 3, 4, 1)).reshape(n_pool, D_ATTN, PAGE_SIZE)
        vi_pool = cache_v[l].reshape(n_pool, PAGE_SIZE * N_HEADS, V_DIM)
        post_w = (w_out16, _row(g_post_mix[l]), _row(g_pre_ffn[l]), w_ff1_16, w_ff2_16, _row(g_post_ffn[l]),
                  w_gate16, w_ple16)

        q_p, kt_p, vi_p, v16_p, u_p = _inproj(hp, _row(g_pre_mix[l]), w_in16, n_seq=batch, tm=512)
        ya_p = _attn_prompt(q_p, kt_p, v16_p, lam_vecs, gsub, batch, seq, lam_init)
        yc_p = _conv_prompt(u_p, *conv_args, batch, seq)
        q_s, kt_s, vi_s, _, u_s = _inproj(hs, _row(g_pre_mix[l]), w_in16, n_seq=1, tm=ms)
        yc_s, st_tm = _conv_sample(jnp.transpose(state_conv[l], (1, 0, 2)), u_s, *conv_args,
                                   dec_batch, dec_seq)

        hp, ya_s = _post_paged((ya_p, yc_p, hp, p_prompt[l].reshape(mp, D_PLE)) + post_w,
                               q_s, kt_s[0], vi_s, kt_pool, vi_pool, page_table, lam_vecs, gsub,
                               dec_batch, dec_seq, lam_init)
        hs = _post((ya_s, yc_s, hs, p_sample[l].reshape(ms, D_PLE)) + post_w, tm=ms)

        outs["kp"].append(jnp.transpose(kt_p.reshape(batch, N_HEADS, 2, HEAD_DIM, seq), (0, 4, 1, 2, 3)))
        outs["vp"].append(vi_p.reshape(batch, seq, N_HEADS, V_DIM))
        outs["cp"].append(u_p.reshape(batch, seq, D_CONV)[:, seq - (CONV_WIDTH - 1):])
        outs["ks"].append(jnp.transpose(kt_s.reshape(N_HEADS, 2, HEAD_DIM, dec_batch, dec_seq),
                                        (3, 4, 0, 1, 2)))
        outs["vs"].append(vi_s.reshape(dec_batch, dec_seq, N_HEADS, V_DIM))
        outs["cs"].append(jnp.transpose(st_tm, (1, 0, 2)))

    return (hp.reshape(batch, seq, D_MODEL), hs.reshape(dec_batch, dec_seq, D_MODEL),
            jnp.stack(outs["kp"]), jnp.stack(outs["vp"]), jnp.stack(outs["cp"]),
            jnp.stack(outs["ks"]), jnp.stack(outs["vs"]), jnp.stack(outs["cs"]))
```

```python
import functools
import math

import jax
import jax.numpy as jnp
from jax import lax
from jax.experimental import pallas as pl
from jax.experimental.pallas import tpu as pltpu

D_MODEL = 1024
D_ATTN = 512
D_CONV = 512
N_HEADS = 4
HEAD_DIM = 64
V_DIM = 128
CONV_WIDTH = 31
D_FF = 4096
D_PLE = 256
PAGE_SIZE = 128
EPS = 1e-6
N_QKV = 3 * D_ATTN
N_IN = N_QKV + 2 * D_CONV
NEG_BIG = -1e30
LOG2E = math.log2(math.e)
Q_SCALE = HEAD_DIM ** -0.5 * LOG2E
ALIBI_SLOPES = tuple(2.0 ** (-8.0 * (h + 1) / N_HEADS) * LOG2E for h in range(N_HEADS))

F32 = jnp.float32
BF16 = jnp.bfloat16

VMEM_LIMIT_BYTES = 56 * 1024 * 1024


def _cparams(semantics):
    return pltpu.CompilerParams(dimension_semantics=semantics, vmem_limit_bytes=VMEM_LIMIT_BYTES)


def _rms(x, g):
    return x * lax.rsqrt(jnp.mean(x * x, axis=-1, keepdims=True) + EPS) * g


def _diff_lambda(lq1, lk1, lq2, lk2, lam_init):
    return (jnp.exp(jnp.sum(lq1 * lk1, axis=-1, keepdims=True))
            - jnp.exp(jnp.sum(lq2 * lk2, axis=-1, keepdims=True)) + lam_init)


def _inproj_kernel(x_ref, g_ref, w_ref, q_ref, kt_ref, vi_ref, v16_ref, u_ref, *, tm):
    xn = _rms(x_ref[...], g_ref[...]).astype(BF16)
    q = jnp.dot(xn, w_ref[:, 0:D_ATTN], preferred_element_type=F32)
    q_ref[...] = (q * Q_SCALE).astype(BF16)
    kt_ref[...] = jnp.dot(xn, w_ref[:, D_ATTN:2 * D_ATTN], preferred_element_type=F32).T
    v = jnp.dot(xn, w_ref[:, 2 * D_ATTN:N_QKV], preferred_element_type=F32)
    v16_ref[...] = v.astype(BF16)
    for h in range(N_HEADS):
        vi_ref[pl.ds(h, tm, stride=N_HEADS), :] = v[:, h * V_DIM:(h + 1) * V_DIM]
    a = jnp.dot(xn, w_ref[:, N_QKV:N_QKV + D_CONV], preferred_element_type=F32)
    gt = jnp.dot(xn, w_ref[:, N_QKV + D_CONV:], preferred_element_type=F32)
    u_ref[...] = a * jax.nn.sigmoid(gt)


def _inproj(x2d, g_pre_mix, w_in16, n_seq, tm):
    m = x2d.shape[0]
    seq = m // n_seq
    nt = seq // tm
    return pl.pallas_call(
        functools.partial(_inproj_kernel, tm=tm),
        grid=(m // tm,),
        in_specs=[
            pl.BlockSpec((tm, D_MODEL), lambda i: (i, 0)),
            pl.BlockSpec((1, D_MODEL), lambda i: (0, 0)),
            pl.BlockSpec((D_MODEL, N_IN), lambda i: (0, 0)),
        ],
        out_specs=[
            pl.BlockSpec((tm, D_ATTN), lambda i: (i, 0)),
            pl.BlockSpec((None, D_ATTN, tm), lambda i: (i // nt, 0, i % nt)),
            pl.BlockSpec((tm * N_HEADS, V_DIM), lambda i: (i, 0)),
            pl.BlockSpec((tm, D_ATTN), lambda i: (i, 0)),
            pl.BlockSpec((tm, D_CONV), lambda i: (i, 0)),
        ],
        out_shape=[
            jax.ShapeDtypeStruct((m, D_ATTN), BF16),
            jax.ShapeDtypeStruct((n_seq, D_ATTN, seq), F32),
            jax.ShapeDtypeStruct((m * N_HEADS, V_DIM), F32),
            jax.ShapeDtypeStruct((m, D_ATTN), BF16),
            jax.ShapeDtypeStruct((m, D_CONV), F32),
        ],
        compiler_params=_cparams(("parallel",)),
        name="inproj",
    )(x2d, g_pre_mix, w_in16)


ATTN_TILE = 256
ATTN_ROW_BLOCK = 64
LANES = 128


def _attn_prompt_kernel(lq1_ref, lk1_ref, lq2_ref, lk2_ref, gsub_ref, q_ref, kt_ref, v_ref, o_ref,
                        qs_scr, m_scr, l_scr, acc_scr, *, lam_init):
    t, rb = ATTN_TILE, ATTN_ROW_BLOCK
    n_rb = t // rb
    qi = pl.program_id(1)
    lane = lax.broadcasted_iota(jnp.int32, (rb, 2 * HEAD_DIM), 1)
    kcol = lax.broadcasted_iota(jnp.int32, (1, t), 1)

    for h in range(N_HEADS):
        hs = slice(h * V_DIM, (h + 1) * V_DIM)
        for g in range(n_rb):
            q = q_ref[g * rb:(g + 1) * rb, hs]
            zero = jnp.zeros_like(q)
            qs_scr[h, g] = jnp.concatenate([jnp.where(lane < HEAD_DIM, q, zero),
                                            jnp.where(lane >= HEAD_DIM, q, zero)], axis=0)
    m_scr[...] = jnp.full(m_scr.shape, NEG_BIG, F32)
    l_scr[...] = jnp.zeros(l_scr.shape, F32)
    acc_scr[...] = jnp.zeros(acc_scr.shape, F32)

    def tile(kv, diagonal):
        k0 = pl.multiple_of(kv * t, t)
        koff = (kcol + (kv - qi) * t).astype(F32)
        for h in range(N_HEADS):
            hs = slice(h * V_DIM, (h + 1) * V_DIM)
            kt = kt_ref[hs, pl.ds(k0, t)].astype(BF16)
            vv = v_ref[pl.ds(k0, t), hs]
            bias = ALIBI_SLOPES[h] * koff
            for g in range(n_rb):
                s = jnp.dot(qs_scr[h, g], kt, preferred_element_type=F32) + bias
                if diagonal:
                    row = lax.broadcasted_iota(jnp.int32, (2 * rb, t), 0)
                    row = jnp.where(row >= rb, row - rb, row) + g * rb
                    col = lax.broadcasted_iota(jnp.int32, (2 * rb, t), 1)
                    s = jnp.where(col <= row, s, NEG_BIG)
                m_prev = m_scr[h, g]
                m_new = jnp.maximum(m_prev, jnp.max(s, axis=-1, keepdims=True))
                alpha = jnp.exp2(m_prev - m_new)
                p = jnp.exp2(s - jnp.concatenate([m_new] * (t // LANES), axis=-1))
                l_scr[h, g] = alpha * l_scr[h, g] + jnp.sum(p, axis=-1, keepdims=True)
                acc_scr[h, g] = alpha * acc_scr[h, g] + jnp.dot(p.astype(BF16), vv,
                                                               preferred_element_type=F32)
                m_scr[h, g] = m_new

    def below_diagonal(kv, carry):
        tile(kv, False)
        return carry

    lax.fori_loop(0, qi, below_diagonal, 0)
    tile(qi, True)

    lam = _diff_lambda(lq1_ref[...], lk1_ref[...], lq2_ref[...], lk2_ref[...], lam_init)
    for h in range(N_HEADS):
        for g in range(n_rb):
            o = acc_scr[h, g] / l_scr[h, g]
            oh = o[:rb] - lam * o[rb:]
            y = _rms(oh, gsub_ref[...]) * (1.0 - lam_init)
            o_ref[g * rb:(g + 1) * rb, h * V_DIM:(h + 1) * V_DIM] = y.astype(o_ref.dtype)


def _attn_prompt(q16, kt, v16, lam_vecs, g_subln, batch, seq, lam_init):
    t, rb = ATTN_TILE, ATTN_ROW_BLOCK
    nq = seq // t
    n_rb = t // rb
    kern = functools.partial(_attn_prompt_kernel, lam_init=lam_init)
    vec = lambda n: pl.BlockSpec((1, n), lambda b, qi: (0, 0))
    return pl.pallas_call(
        kern,
        grid=(batch, nq),
        in_specs=[vec(HEAD_DIM)] * 4 + [
            vec(V_DIM),
            pl.BlockSpec((t, D_ATTN), lambda b, qi: (b * nq + qi, 0)),
            pl.BlockSpec((None, D_ATTN, seq), lambda b, qi: (b, 0, 0)),
            pl.BlockSpec((seq, D_ATTN), lambda b, qi: (b, 0)),
        ],
        out_specs=pl.BlockSpec((t, D_ATTN), lambda b, qi: (b * nq + qi, 0)),
        out_shape=jax.ShapeDtypeStruct((batch * seq, D_ATTN), BF16),
        scratch_shapes=[
            pltpu.VMEM((N_HEADS, n_rb, 2 * rb, 2 * HEAD_DIM), BF16),
            pltpu.VMEM((N_HEADS, n_rb, 2 * rb, LANES), F32),
            pltpu.VMEM((N_HEADS, n_rb, 2 * rb, LANES), F32),
            pltpu.VMEM((N_HEADS, n_rb, 2 * rb, V_DIM), F32),
        ],
        compiler_params=_cparams(("parallel", "arbitrary")),
        name="attn_prompt",
    )(*lam_vecs, g_subln, q16, kt, v16)


PAGED_CHAINS = 1


def _paged_rows(dec_seq):
    rph = 2 * dec_seq
    n_rows = N_HEADS * rph
    row = lax.broadcasted_iota(jnp.int32, (n_rows, 1), 0)
    head = row // rph
    comp = (row % rph) // dec_seq
    tok = row % dec_seq
    slope = jnp.zeros((n_rows, 1), F32)
    for h in range(N_HEADS):
        slope = jnp.where(head == h, ALIBI_SLOPES[h], slope)
    return rph, n_rows, head, comp, tok, slope


def _paged_init(q_ref, q2_scr, qbd_scr, m_scr, l_scr, acc_scr, *, dec_seq):
    rph, n_rows, head, comp, _, _ = _paged_rows(dec_seq)
    q = q_ref[...]
    q2_scr[0:dec_seq, :] = q
    q2_scr[dec_seq:rph, :] = q
    qrep = jnp.concatenate([q2_scr[...]] * N_HEADS, axis=0)
    col = lax.broadcasted_iota(jnp.int32, (n_rows, D_ATTN), 1)
    qbd_scr[...] = jnp.where(col // HEAD_DIM == head * 2 + comp, qrep, 0.0).astype(BF16)
    m_scr[...] = jnp.full(m_scr.shape, NEG_BIG, F32)
    l_scr[...] = jnp.zeros(l_scr.shape, F32)
    acc_scr[...] = jnp.zeros(acc_scr.shape, F32)


def _paged_update(c, s, v_refs, n_slots, m_scr, l_scr, acc_scr, rph):
    m_prev = m_scr[c]
    m_new = jnp.maximum(m_prev, jnp.max(s, axis=-1, keepdims=True))
    alpha = jnp.exp2(m_prev - m_new)
    p = jnp.exp2(s - m_new)
    l_scr[c] = alpha * l_scr[c] + jnp.sum(p, axis=-1, keepdims=True)
    p16 = p.astype(BF16)
    for h in range(N_HEADS):
        rs = slice(h * rph, (h + 1) * rph)
        vh = jnp.concatenate([r[pl.ds(h, n_slots, stride=N_HEADS), :] for r in v_refs], axis=0)
        pv = jnp.dot(p16[rs, :], vh.astype(BF16), preferred_element_type=F32)
        acc_scr[c, rs, :] = alpha[rs] * acc_scr[c, rs, :] + pv
    m_scr[c] = m_new


def _paged_pages(j, ktp_refs, vip_refs, qbd_scr, m_scr, l_scr, acc_scr, *, past_len, dec_seq):
    rph, _, _, _, _, slope = _paged_rows(dec_seq)
    pps = len(ktp_refs)
    ppc = pps // PAGED_CHAINS
    qbd = qbd_scr[...]
    kcol = lax.broadcasted_iota(jnp.int32, (1, ppc * PAGE_SIZE), 1)
    for c in range(PAGED_CHAINS):
        pages = range(c * ppc, (c + 1) * ppc)
        s = jnp.concatenate(
            [jnp.dot(qbd, ktp_refs[i][...].astype(BF16), preferred_element_type=F32) for i in pages], axis=-1)
        koff = (kcol + ((j * pps + c * ppc) * PAGE_SIZE - past_len)).astype(F32)
        _paged_update(c, s + slope * koff, [vip_refs[i] for i in pages], PAGE_SIZE, m_scr, l_scr, acc_scr, rph)


def _paged_finish(b, lam_refs, gsub_ref, ktn_ref, vin_ref, o_ref, qbd_scr, m_scr, l_scr, acc_scr, *,
                  dec_seq, lam_init):
    rph, n_rows, _, _, tok, slope = _paged_rows(dec_seq)
    n_new = ktn_ref.shape[1]
    sn = jnp.dot(qbd_scr[...], ktn_ref[...].astype(BF16), preferred_element_type=F32)
    col = lax.broadcasted_iota(jnp.int32, (n_rows, n_new), 1)
    ctok = col % dec_seq
    valid = jnp.logical_and(col // dec_seq == b, ctok <= tok)
    sn = jnp.where(valid, sn + slope * ctok.astype(F32), NEG_BIG)
    _paged_update(0, sn, [vin_ref], n_new, m_scr, l_scr, acc_scr, rph)

    m = m_scr[0]
    for c in range(1, PAGED_CHAINS):
        m = jnp.maximum(m, m_scr[c])
    l = jnp.zeros((n_rows, 1), F32)
    acc = jnp.zeros((n_rows, V_DIM), F32)
    for c in range(PAGED_CHAINS):
        w = jnp.exp2(m_scr[c] - m)
        l = l + w * l_scr[c]
        acc = acc + w * acc_scr[c]

    lam = _diff_lambda(*[r[...] for r in lam_refs], lam_init)
    o = acc / l
    for h in range(N_HEADS):
        oh = o[h * rph:h * rph + dec_seq] - lam * o[h * rph + dec_seq:(h + 1) * rph]
        o_ref[:, h * V_DIM:(h + 1) * V_DIM] = _rms(oh, gsub_ref[...]) * (1.0 - lam_init)


SUBLANES = 8
CONV_HALO = 32
CONV_ROW_CHUNK = 16


def _ln_swish(c, g, b):
    mu = jnp.mean(c, axis=-1, keepdims=True)
    d = c - mu
    var = jnp.mean(d * d, axis=-1, keepdims=True)
    y = d * lax.rsqrt(var + EPS) * g + b
    return y * jax.nn.sigmoid(y)


def _conv_prompt_kernel(prev_ref, cur_ref, w_ref, b_ref, g_ref, beta_ref, o_ref, ext_scr, sh_scr, *, tm):
    i = pl.program_id(1)
    n_ext = CONV_HALO + tm
    ext_scr[0:CONV_HALO, :] = jnp.where(i == 0, 0.0, prev_ref[...])
    ext_scr[CONV_HALO:n_ext, :] = cur_ref[...]
    ext = ext_scr[...]
    for r in range(1, SUBLANES):
        sh_scr[r] = pltpu.roll(ext, n_ext - r, axis=0)
    first = CONV_HALO - (CONV_WIDTH - 1)
    for c in range(0, tm, CONV_ROW_CHUNK):
        acc = jnp.broadcast_to(b_ref[...], (CONV_ROW_CHUNK, D_CONV))
        for j in range(CONV_WIDTH):
            r = (first + j) % SUBLANES
            a = c + first + j - r
            src = ext_scr[a:a + CONV_ROW_CHUNK, :] if r == 0 else sh_scr[r, a:a + CONV_ROW_CHUNK, :]
            acc = acc + jnp.concatenate([w_ref[j]] * (CONV_ROW_CHUNK // SUBLANES), axis=0) * src
        o_ref[c:c + CONV_ROW_CHUNK, :] = _ln_swish(acc, g_ref[...], beta_ref[...]).astype(o_ref.dtype)


def _conv_prompt(u2d, w_dw, b_dw, ln_g, ln_b, batch, seq, tm=512):
    nt = seq // tm
    halo_per_tile = tm // CONV_HALO
    kern = functools.partial(_conv_prompt_kernel, tm=tm)
    vec = pl.BlockSpec((1, D_CONV), lambda b, i: (0, 0))
    w_rep = jnp.broadcast_to(w_dw[:, None, :], (CONV_WIDTH, SUBLANES, D_CONV))
    return pl.pallas_call(
        kern,
        grid=(batch, nt),
        in_specs=[
            pl.BlockSpec((CONV_HALO, D_CONV),
                         lambda b, i: (jnp.maximum((b * nt + i) * halo_per_tile - 1, 0), 0)),
            pl.BlockSpec((tm, D_CONV), lambda b, i: (b * nt + i, 0)),
            pl.BlockSpec((CONV_WIDTH, SUBLANES, D_CONV), lambda b, i: (0, 0, 0)),
            vec, vec, vec,
        ],
        out_specs=pl.BlockSpec((tm, D_CONV), lambda b, i: (b * nt + i, 0)),
        out_shape=jax.ShapeDtypeStruct((batch * seq, D_CONV), BF16),
        scratch_shapes=[pltpu.VMEM((CONV_HALO + tm, D_CONV), F32),
                        pltpu.VMEM((SUBLANES, CONV_HALO + tm, D_CONV), F32)],
        compiler_params=_cparams(("parallel", "parallel")),
        name="conv_prompt",
    )(u2d, u2d, w_rep, b_dw, ln_g, ln_b)


def _conv_sample_kernel(state_ref, u_ref, w_ref, b_ref, g_ref, beta_ref, o_ref, st_ref, *, dec_batch, dec_seq):
    n_state = CONV_WIDTH - 1

    def ext(i):
        return state_ref[i] if i < n_state else u_ref[i - n_state]

    for t in range(dec_seq):
        acc = jnp.broadcast_to(b_ref[...], (dec_batch, D_CONV))
        for j in range(CONV_WIDTH):
            acc = acc + w_ref[j:j + 1, :] * ext(t + j)
        o_ref[t] = _ln_swish(acc, g_ref[...], beta_ref[...])
    for i in range(n_state):
        st_ref[i] = ext(i + dec_seq)


def _conv_sample(state_tm, u_s, w_dw, b_dw, ln_g, ln_b, dec_batch, dec_seq):
    kern = functools.partial(_conv_sample_kernel, dec_batch=dec_batch, dec_seq=dec_seq)
    u_tm = jnp.transpose(u_s.reshape(dec_batch, dec_seq, D_CONV), (1, 0, 2))
    y_tm, st_tm = pl.pallas_call(
        kern,
        out_shape=[
            jax.ShapeDtypeStruct((dec_seq, dec_batch, D_CONV), F32),
            jax.ShapeDtypeStruct(state_tm.shape, F32),
        ],
        compiler_params=pltpu.CompilerParams(vmem_limit_bytes=VMEM_LIMIT_BYTES),
        name="conv_sample",
    )(state_tm, u_tm, w_dw, b_dw, ln_g, ln_b)
    return jnp.transpose(y_tm, (1, 0, 2)).reshape(dec_batch * dec_seq, D_CONV), st_tm


def _post_first(ya_ref, yc_ref, x_ref, wo_ref, gpm_ref, gpf_ref, h_scr, hn_scr, acc_scr):
    m = (jnp.dot(ya_ref[...].astype(BF16), wo_ref[0:D_ATTN, :], preferred_element_type=F32)
         + jnp.dot(yc_ref[...].astype(BF16), wo_ref[D_ATTN:, :], preferred_element_type=F32))
    h = x_ref[...] + _rms(m, gpm_ref[...])
    h_scr[...] = h
    hn_scr[...] = _rms(h, gpf_ref[...]).astype(BF16)
    acc_scr[...] = jnp.zeros(acc_scr.shape, F32)


def _post_ffn_chunk(w1_ref, w2_ref, hn_scr, acc_scr):
    a = jnp.dot(hn_scr[...], w1_ref[...], preferred_element_type=F32)
    a = jnp.square(jnp.maximum(a, 0.0)).astype(BF16)
    acc_scr[...] += jnp.dot(a, w2_ref[...], preferred_element_type=F32)


def _post_last(pe_ref, gpost_ref, wg_ref, wp_ref, o_ref, h_scr, acc_scr):
    h2 = h_scr[...] + _rms(acc_scr[...], gpost_ref[...])
    gate = jax.nn.sigmoid(jnp.dot(h2.astype(BF16), wg_ref[...], preferred_element_type=F32))
    pev = jnp.dot(pe_ref[...].astype(BF16), wp_ref[...], preferred_element_type=F32)
    o_ref[...] = h2 + gate * pev


def _post_kernel(ya_ref, yc_ref, x_ref, pe_ref, wo_ref, gpm_ref, gpf_ref, w1_ref, w2_ref, gpost_ref,
                 wg_ref, wp_ref, o_ref, h_scr, hn_scr, acc_scr):
    k = pl.program_id(1)

    @pl.when(k == 0)
    def _():
        _post_first(ya_ref, yc_ref, x_ref, wo_ref, gpm_ref, gpf_ref, h_scr, hn_scr, acc_scr)

    _post_ffn_chunk(w1_ref, w2_ref, hn_scr, acc_scr)

    @pl.when(k == pl.num_programs(1) - 1)
    def _():
        _post_last(pe_ref, gpost_ref, wg_ref, wp_ref, o_ref, h_scr, acc_scr)


N_POST_IN = 12
N_PAGED_IN = 10


def _page_copies(pt_ref, seq, chunk, slot, kt_pool_ref, vi_pool_ref, kbuf, vbuf, sem, pps):
    copies = []
    for p in range(pps):
        page = pt_ref[seq, chunk * pps + p]
        copies.append(pltpu.make_async_copy(kt_pool_ref.at[page], kbuf.at[slot, p], sem.at[slot, 0]))
        copies.append(pltpu.make_async_copy(vi_pool_ref.at[page], vbuf.at[slot, p], sem.at[slot, 1]))
    return copies


def _post_paged_kernel(pt_ref, *refs, pps, past_len, dec_seq, lam_init):
    (ya_ref, yc_ref, x_ref, pe_ref, wo_ref, gpm_ref, gpf_ref, w1_ref, w2_ref, gpost_ref, wg_ref,
     wp_ref) = refs[:N_POST_IN]
    lam_refs = refs[N_POST_IN:N_POST_IN + 4]
    gsub_ref, q_ref, ktn_ref, vin_ref, kt_pool_ref, vi_pool_ref = refs[N_POST_IN + 4:N_POST_IN + N_PAGED_IN]
    o_ref, oa_ref = refs[N_POST_IN + N_PAGED_IN:N_POST_IN + N_PAGED_IN + 2]
    (h_scr, hn_scr, acc_scr, q2_scr, qbd_scr, m_scr, l_scr, pacc_scr, kbuf, vbuf,
     sem) = refs[N_POST_IN + N_PAGED_IN + 2:]
    i = pl.program_id(0)
    k = pl.program_id(1)
    n_k = pl.num_programs(1)
    step = i * n_k + k
    slot = step % 2
    copies = functools.partial(_page_copies, pt_ref, kt_pool_ref=kt_pool_ref, vi_pool_ref=vi_pool_ref,
                               kbuf=kbuf, vbuf=vbuf, sem=sem, pps=pps)

    @pl.when(step == 0)
    def _():
        for c in copies(0, 0, 0):
            c.start()

    @pl.when(step + 1 < pl.num_programs(0) * n_k)
    def _():
        nxt = step + 1
        for c in copies(nxt // n_k, nxt % n_k, 1 - slot):
            c.start()

    @pl.when(k == 0)
    def _():
        _post_first(ya_ref, yc_ref, x_ref, wo_ref, gpm_ref, gpf_ref, h_scr, hn_scr, acc_scr)
        _paged_init(q_ref, q2_scr, qbd_scr, m_scr, l_scr, pacc_scr, dec_seq=dec_seq)

    _post_ffn_chunk(w1_ref, w2_ref, hn_scr, acc_scr)
    for c in copies(i, k, slot):
        c.wait()
    _paged_pages(k, [kbuf.at[slot, p] for p in range(pps)], [vbuf.at[slot, p] for p in range(pps)],
                 qbd_scr, m_scr, l_scr, pacc_scr, past_len=past_len, dec_seq=dec_seq)

    @pl.when(k == n_k - 1)
    def _():
        _post_last(pe_ref, gpost_ref, wg_ref, wp_ref, o_ref, h_scr, acc_scr)
        _paged_finish(i, lam_refs, gsub_ref, ktn_ref, vin_ref, oa_ref, qbd_scr, m_scr, l_scr, pacc_scr,
                      dec_seq=dec_seq, lam_init=lam_init)


POST_ROW_TILE = 512
POST_FF_CHUNK = 1024


def _post_specs(tm, tf, imap):
    row_tile = lambda n: pl.BlockSpec((tm, n), imap(lambda i, k: (i, 0)))
    vec = pl.BlockSpec((1, D_MODEL), imap(lambda i, k: (0, 0)))
    resident = lambda r, c: pl.BlockSpec((r, c), imap(lambda i, k: (0, 0)), pipeline_mode=pl.Buffered(1))
    in_specs = [
        row_tile(D_ATTN), row_tile(D_CONV), row_tile(D_MODEL), row_tile(D_PLE),
        resident(D_MODEL, D_MODEL), vec, vec,
        pl.BlockSpec((D_MODEL, tf), imap(lambda i, k: (0, k))),
        pl.BlockSpec((tf, D_MODEL), imap(lambda i, k: (k, 0))),
        vec,
        resident(D_MODEL, D_MODEL), resident(D_PLE, D_MODEL),
    ]
    scratch = [pltpu.VMEM((tm, D_MODEL), F32), pltpu.VMEM((tm, D_MODEL), BF16), pltpu.VMEM((tm, D_MODEL), F32)]
    return in_specs, row_tile(D_MODEL), scratch


def _post(post_args, tm):
    m = post_args[2].shape[0]
    in_specs, out_spec, scratch = _post_specs(tm, POST_FF_CHUNK, lambda f: f)
    return pl.pallas_call(
        _post_kernel,
        grid=(m // tm, D_FF // POST_FF_CHUNK),
        in_specs=in_specs,
        out_specs=out_spec,
        out_shape=jax.ShapeDtypeStruct((m, D_MODEL), F32),
        scratch_shapes=scratch,
        compiler_params=_cparams(("parallel", "arbitrary")),
        name="post",
    )(*post_args)


def _post_paged(post_args, q_s, kt_new, vi_new, kt_pool, vi_pool, page_table, lam_vecs, g_subln,
                dec_batch, dec_seq, lam_init):
    tm, tf = POST_ROW_TILE, POST_FF_CHUNK
    m = post_args[2].shape[0]
    n_row_tiles, n_ff = m // tm, D_FF // tf
    n_pages = page_table.shape[1]
    assert n_row_tiles == dec_batch and n_pages % n_ff == 0
    pps = n_pages // n_ff
    assert pps % PAGED_CHAINS == 0
    past_len = n_pages * PAGE_SIZE
    rph = 2 * dec_seq
    n_rows = N_HEADS * rph
    n_new = dec_batch * dec_seq
    q3 = q_s.astype(F32).reshape(dec_batch, dec_seq, D_ATTN)
    with_pt = lambda f: (lambda i, k, pt: f(i, k))
    post_in, post_out, post_scratch = _post_specs(tm, tf, with_pt)
    vec = lambda n: pl.BlockSpec((1, n), lambda i, k, pt: (0, 0))

    grid_spec = pltpu.PrefetchScalarGridSpec(
        num_scalar_prefetch=1,
        grid=(n_row_tiles, n_ff),
        in_specs=post_in + [vec(HEAD_DIM)] * 4 + [
            vec(V_DIM),
            pl.BlockSpec((None, dec_seq, D_ATTN), lambda i, k, pt: (i, 0, 0)),
            pl.BlockSpec((D_ATTN, n_new), lambda i, k, pt: (0, 0)),
            pl.BlockSpec((n_new * N_HEADS, V_DIM), lambda i, k, pt: (0, 0)),
            pl.BlockSpec(memory_space=pl.ANY),
            pl.BlockSpec(memory_space=pl.ANY),
        ],
        out_specs=[post_out, pl.BlockSpec((None, dec_seq, D_ATTN), lambda i, k, pt: (i, 0, 0))],
        scratch_shapes=post_scratch + [
            pltpu.VMEM((rph, D_ATTN), F32),
            pltpu.VMEM((n_rows, D_ATTN), BF16),
            pltpu.VMEM((PAGED_CHAINS, n_rows, 1), F32),
            pltpu.VMEM((PAGED_CHAINS, n_rows, 1), F32),
            pltpu.VMEM((PAGED_CHAINS, n_rows, V_DIM), F32),
            pltpu.VMEM((2, pps, D_ATTN, PAGE_SIZE), F32),
            pltpu.VMEM((2, pps, PAGE_SIZE * N_HEADS, V_DIM), F32),
            pltpu.SemaphoreType.DMA((2, 2)),
        ],
    )
    kern = functools.partial(_post_paged_kernel, pps=pps, past_len=past_len, dec_seq=dec_seq, lam_init=lam_init)
    out, ya_s = pl.pallas_call(
        kern,
        grid_spec=grid_spec,
        out_shape=[jax.ShapeDtypeStruct((m, D_MODEL), F32),
                   jax.ShapeDtypeStruct((dec_batch, dec_seq, D_ATTN), F32)],
        compiler_params=_cparams(("arbitrary", "arbitrary")),
        name="post_paged",
    )(page_table, *post_args, *lam_vecs, g_subln, q3, kt_new, vi_new, kt_pool, vi_pool)
    return out, ya_s.reshape(dec_batch * dec_seq, D_ATTN)


def _row(v):
    return v.reshape(1, -1)


def kernel(x_prompt, x_sample, cache_k, cache_v, state_conv, page_table, p_prompt, p_sample,
           w_in, w_out, lambda_q1, lambda_k1, lambda_q2, lambda_k2, g_subln, w_dw, b_dw,
           ln_conv_g, ln_conv_b, g_pre_mix, g_post_mix, g_pre_ffn, g_post_ffn,
           w_ff1, w_ff2, w_ple, w_ple_gate):
    depth = w_in.shape[0]
    batch, seq, _ = x_prompt.shape
    dec_batch, dec_seq, _ = x_sample.shape
    n_pool = cache_k.shape[1]
    mp, ms = batch * seq, dec_batch * dec_seq
    assert 2 * dec_seq == 8 and cache_k.shape[2] == PAGE_SIZE

    hp = x_prompt.reshape(mp, D_MODEL)
    hs = x_sample.reshape(ms, D_MODEL)
    outs = {k: [] for k in ("kp", "vp", "cp", "ks", "vs", "cs")}
    for l in range(depth):
        lam_init = 0.8 - 0.6 * math.exp(-0.3 * l)
        w_in16 = w_in[l].astype(BF16)
        w_out16 = w_out[l].astype(BF16)
        w_ff1_16 = w_ff1[l].astype(BF16)
        w_ff2_16 = w_ff2[l].astype(BF16)
        w_gate16 = w_ple_gate[l].astype(BF16)
        w_ple16 = w_ple[l].astype(BF16)
        lam_vecs = [_row(lambda_q1[l]), _row(lambda_k1[l]), _row(lambda_q2[l]), _row(lambda_k2[l])]
        gsub = _row(g_subln[l])
        conv_args = (w_dw[l], _row(b_dw[l]), _row(ln_conv_g[l]), _row(ln_conv_b[l]))
        kt_pool = jnp.transpose(cache_k[l], (0, 2, 3, 4, 1)).reshape(n_pool, D_ATTN, PAGE_SIZE)
        vi_pool = cache_v[l].reshape(n_pool, PAGE_SIZE * N_HEADS, V_DIM)
        post_w = (w_out16, _row(g_post_mix[l]), _row(g_pre_ffn[l]), w_ff1_16, w_ff2_16, _row(g_post_ffn[l]),
                  w_gate16, w_ple16)

        q_p, kt_p, vi_p, v16_p, u_p = _inproj(hp, _row(g_pre_mix[l]), w_in16, n_seq=batch, tm=512)
        ya_p = _attn_prompt(q_p, kt_p, v16_p, lam_vecs, gsub, batch, seq, lam_init)
        yc_p = _conv_prompt(u_p, *conv_args, batch, seq)
        q_s, kt_s, vi_s, _, u_s = _inproj(hs, _row(g_pre_mix[l]), w_in16, n_seq=1, tm=ms)
        yc_s, st_tm = _conv_sample(jnp.transpose(state_conv[l], (1, 0, 2)), u_s, *conv_args,
                                   dec_batch, dec_seq)

        hp, ya_s = _post_paged((ya_p, yc_p, hp, p_prompt[l].reshape(mp, D_PLE)) + post_w,
                               q_s, kt_s[0], vi_s, kt_pool, vi_pool, page_table, lam_vecs, gsub,
                               dec_batch, dec_seq, lam_init)
        hs = _post((ya_s, yc_s, hs, p_sample[l].reshape(ms, D_PLE)) + post_w, tm=ms)

        outs["kp"].append(jnp.transpose(kt_p.reshape(batch, N_HEADS, 2, HEAD_DIM, seq), (0, 4, 1, 2, 3)))
        outs["vp"].append(vi_p.reshape(batch, seq, N_HEADS, V_DIM))
        outs["cp"].append(u_p.reshape(batch, seq, D_CONV)[:, seq - (CONV_WIDTH - 1):])
        outs["ks"].append(jnp.transpose(kt_s.reshape(N_HEADS, 2, HEAD_DIM, dec_batch, dec_seq),
                                        (3, 4, 0, 1, 2)))
        outs["vs"].append(vi_s.reshape(dec_batch, dec_seq, N_HEADS, V_DIM))
        outs["cs"].append(jnp.transpose(st_tm, (1, 0, 2)))

    return (hp.reshape(batch, seq, D_MODEL), hs.reshape(dec_batch, dec_seq, D_MODEL),
            jnp.stack(outs["kp"]), jnp.stack(outs["vp"]), jnp.stack(outs["cp"]),
            jnp.stack(outs["ks"]), jnp.stack(outs["vs"]), jnp.stack(outs["cs"]))
```

```python
import functools
import math

import jax
import jax.numpy as jnp
from jax import lax
from jax.experimental import pallas as pl
from jax.experimental.pallas import tpu as pltpu

D_MODEL = 1024
D_ATTN = 512
D_CONV = 512
N_HEADS = 4
HEAD_DIM = 64
V_DIM = 128
CONV_WIDTH = 31
D_FF = 4096
D_PLE = 256
PAGE_SIZE = 128
EPS = 1e-6
N_QKV = 3 * D_ATTN
N_IN = N_QKV + 2 * D_CONV
NEG_BIG = -1e30
LOG2E = math.log2(math.e)
Q_SCALE = HEAD_DIM ** -0.5 * LOG2E
ALIBI_SLOPES = tuple(2.0 ** (-8.0 * (h + 1) / N_HEADS) * LOG2E for h in range(N_HEADS))

F32 = jnp.float32
BF16 = jnp.bfloat16

VMEM_LIMIT_BYTES = 56 * 1024 * 1024


def _cparams(semantics):
    return pltpu.CompilerParams(dimension_semantics=semantics, vmem_limit_bytes=VMEM_LIMIT_BYTES)


def _rms(x, g):
    return x * lax.rsqrt(jnp.mean(x * x, axis=-1, keepdims=True) + EPS) * g


def _diff_lambda(lq1, lk1, lq2, lk2, lam_init):
    return (jnp.exp(jnp.sum(lq1 * lk1, axis=-1, keepdims=True))
            - jnp.exp(jnp.sum(lq2 * lk2, axis=-1, keepdims=True)) + lam_init)


def _inproj_kernel(x_ref, g_ref, w_ref, q_ref, kt_ref, vi_ref, v16_ref, u_ref, *, tm):
    xn = _rms(x_ref[...], g_ref[...]).astype(BF16)
    q = jnp.dot(xn, w_ref[:, 0:D_ATTN], preferred_element_type=F32)
    q_ref[...] = (q * Q_SCALE).astype(BF16)
    kt_ref[...] = jnp.dot(xn, w_ref[:, D_ATTN:2 * D_ATTN], preferred_element_type=F32).T
    v = jnp.dot(xn, w_ref[:, 2 * D_ATTN:N_QKV], preferred_element_type=F32)
    v16_ref[...] = v.astype(BF16)
    for h in range(N_HEADS):
        vi_ref[pl.ds(h, tm, stride=N_HEADS), :] = v[:, h * V_DIM:(h + 1) * V_DIM]
    a = jnp.dot(xn, w_ref[:, N_QKV:N_QKV + D_CONV], preferred_element_type=F32)
    gt = jnp.dot(xn, w_ref[:, N_QKV + D_CONV:], preferred_element_type=F32)
    u_ref[...] = a * jax.nn.sigmoid(gt)


def _inproj(x2d, g_pre_mix, w_in16, n_seq, tm):
    m = x2d.shape[0]
    seq = m // n_seq
    nt = seq // tm
    return pl.pallas_call(
        functools.partial(_inproj_kernel, tm=tm),
        grid=(m // tm,),
        in_specs=[
            pl.BlockSpec((tm, D_MODEL), lambda i: (i, 0)),
            pl.BlockSpec((1, D_MODEL), lambda i: (0, 0)),
            pl.BlockSpec((D_MODEL, N_IN), lambda i: (0, 0)),
        ],
        out_specs=[
            pl.BlockSpec((tm, D_ATTN), lambda i: (i, 0)),
            pl.BlockSpec((None, D_ATTN, tm), lambda i: (i // nt, 0, i % nt)),
            pl.BlockSpec((tm * N_HEADS, V_DIM), lambda i: (i, 0)),
            pl.BlockSpec((tm, D_ATTN), lambda i: (i, 0)),
            pl.BlockSpec((tm, D_CONV), lambda i: (i, 0)),
        ],
        out_shape=[
            jax.ShapeDtypeStruct((m, D_ATTN), BF16),
            jax.ShapeDtypeStruct((n_seq, D_ATTN, seq), F32),
            jax.ShapeDtypeStruct((m * N_HEADS, V_DIM), F32),
            jax.ShapeDtypeStruct((m, D_ATTN), BF16),
            jax.ShapeDtypeStruct((m, D_CONV), F32),
        ],
        compiler_params=_cparams(("parallel",)),
        name="inproj",
    )(x2d, g_pre_mix, w_in16)


ATTN_TILE = 1024
ATTN_KEY_TILE = 256
ATTN_ROW_BLOCK = 64
LANES = 128


def _attn_prompt_kernel(lq1_ref, lk1_ref, lq2_ref, lk2_ref, gsub_ref, q_ref, kt_ref, v_ref, o_ref,
                        qs_scr, m_scr, l_scr, acc_scr, *, lam_init):
    t, w, rb = ATTN_TILE, ATTN_KEY_TILE, ATTN_ROW_BLOCK
    n_rb = t // rb
    qi = pl.program_id(1)
    lane = lax.broadcasted_iota(jnp.int32, (rb, 2 * HEAD_DIM), 1)
    kcol = lax.broadcasted_iota(jnp.int32, (1, w), 1)

    for h in range(N_HEADS):
        hs = slice(h * V_DIM, (h + 1) * V_DIM)
        for g in range(n_rb):
            q = q_ref[g * rb:(g + 1) * rb, hs]
            zero = jnp.zeros_like(q)
            qs_scr[h, g] = jnp.concatenate([jnp.where(lane < HEAD_DIM, q, zero),
                                            jnp.where(lane >= HEAD_DIM, q, zero)], axis=0)
    m_scr[...] = jnp.full(m_scr.shape, NEG_BIG, F32)
    l_scr[...] = jnp.zeros(l_scr.shape, F32)
    acc_scr[...] = jnp.zeros(acc_scr.shape, F32)

    def tile(k0, d):
        koff = (kcol + (k0 - qi * t)).astype(F32)
        for h in range(N_HEADS):
            hs = slice(h * V_DIM, (h + 1) * V_DIM)
            kt = kt_ref[hs, pl.ds(k0, w)].astype(BF16)
            vv = v_ref[pl.ds(k0, w), hs]
            bias = ALIBI_SLOPES[h] * koff
            for g in range(n_rb):
                masked = False
                if d is not None:
                    if (g + 1) * rb <= d * w:
                        continue
                    masked = g * rb < (d + 1) * w - 1
                s = jnp.dot(qs_scr[h, g], kt, preferred_element_type=F32) + bias
                if masked:
                    row = lax.broadcasted_iota(jnp.int32, (2 * rb, w), 0)
                    row = jnp.where(row >= rb, row - rb, row) + (g * rb - d * w)
                    col = lax.broadcasted_iota(jnp.int32, (2 * rb, w), 1)
                    s = jnp.where(col <= row, s, NEG_BIG)
                m_prev = m_scr[h, g]
                m_new = jnp.maximum(m_prev, jnp.max(s, axis=-1, keepdims=True))
                alpha = jnp.exp2(m_prev - m_new)
                p = jnp.exp2(s - jnp.concatenate([m_new] * (w // LANES), axis=-1))
                l_scr[h, g] = alpha * l_scr[h, g] + jnp.sum(p, axis=-1, keepdims=True)
                acc_scr[h, g] = alpha * acc_scr[h, g] + jnp.dot(p.astype(BF16), vv,
                                                               preferred_element_type=F32)
                m_scr[h, g] = m_new

    def below_diagonal(kv, carry):
        tile(pl.multiple_of(kv * w, w), None)
        return carry

    lax.fori_loop(0, qi * (t // w), below_diagonal, 0)
    for d in range(t // w):
        tile(pl.multiple_of(qi * t + d * w, w), d)

    lam = _diff_lambda(lq1_ref[...], lk1_ref[...], lq2_ref[...], lk2_ref[...], lam_init)
    for h in range(N_HEADS):
        for g in range(n_rb):
            o = acc_scr[h, g] / l_scr[h, g]
            oh = o[:rb] - lam * o[rb:]
            y = _rms(oh, gsub_ref[...]) * (1.0 - lam_init)
            o_ref[g * rb:(g + 1) * rb, h * V_DIM:(h + 1) * V_DIM] = y.astype(o_ref.dtype)


def _attn_prompt(q16, kt, v16, lam_vecs, g_subln, batch, seq, lam_init):
    t, rb = ATTN_TILE, ATTN_ROW_BLOCK
    nq = seq // t
    n_rb = t // rb
    kern = functools.partial(_attn_prompt_kernel, lam_init=lam_init)
    vec = lambda n: pl.BlockSpec((1, n), lambda b, qi: (0, 0))
    return pl.pallas_call(
        kern,
        grid=(batch, nq),
        in_specs=[vec(HEAD_DIM)] * 4 + [
            vec(V_DIM),
            pl.BlockSpec((t, D_ATTN), lambda b, qi: (b * nq + qi, 0)),
            pl.BlockSpec((None, D_ATTN, seq), lambda b, qi: (b, 0, 0)),
            pl.BlockSpec((seq, D_ATTN), lambda b, qi: (b, 0)),
        ],
        out_specs=pl.BlockSpec((t, D_ATTN), lambda b, qi: (b * nq + qi, 0)),
        out_shape=jax.ShapeDtypeStruct((batch * seq, D_ATTN), BF16),
        scratch_shapes=[
            pltpu.VMEM((N_HEADS, n_rb, 2 * rb, 2 * HEAD_DIM), BF16),
            pltpu.VMEM((N_HEADS, n_rb, 2 * rb, LANES), F32),
            pltpu.VMEM((N_HEADS, n_rb, 2 * rb, LANES), F32),
            pltpu.VMEM((N_HEADS, n_rb, 2 * rb, V_DIM), F32),
        ],
        compiler_params=_cparams(("parallel", "arbitrary")),
        name="attn_prompt",
    )(*lam_vecs, g_subln, q16, kt, v16)


PAGED_CHAINS = 1


def _paged_rows(dec_seq):
    rph = 2 * dec_seq
    n_rows = N_HEADS * rph
    row = lax.broadcasted_iota(jnp.int32, (n_rows, 1), 0)
    head = row // rph
    comp = (row % rph) // dec_seq
    tok = row % dec_seq
    slope = jnp.zeros((n_rows, 1), F32)
    for h in range(N_HEADS):
        slope = jnp.where(head == h, ALIBI_SLOPES[h], slope)
    return rph, n_rows, head, comp, tok, slope


def _paged_init(q_ref, q2_scr, qbd_scr, m_scr, l_scr, acc_scr, *, dec_seq):
    rph, n_rows, head, comp, _, _ = _paged_rows(dec_seq)
    q = q_ref[...]
    q2_scr[0:dec_seq, :] = q
    q2_scr[dec_seq:rph, :] = q
    qrep = jnp.concatenate([q2_scr[...]] * N_HEADS, axis=0)
    col = lax.broadcasted_iota(jnp.int32, (n_rows, D_ATTN), 1)
    qbd_scr[...] = jnp.where(col // HEAD_DIM == head * 2 + comp, qrep, 0.0).astype(BF16)
    m_scr[...] = jnp.full(m_scr.shape, NEG_BIG, F32)
    l_scr[...] = jnp.zeros(l_scr.shape, F32)
    acc_scr[...] = jnp.zeros(acc_scr.shape, F32)


def _paged_update(c, s, v_refs, n_slots, m_scr, l_scr, acc_scr, rph):
    m_prev = m_scr[c]
    m_new = jnp.maximum(m_prev, jnp.max(s, axis=-1, keepdims=True))
    alpha = jnp.exp2(m_prev - m_new)
    p = jnp.exp2(s - m_new)
    l_scr[c] = alpha * l_scr[c] + jnp.sum(p, axis=-1, keepdims=True)
    p16 = p.astype(BF16)
    for h in range(N_HEADS):
        rs = slice(h * rph, (h + 1) * rph)
        vh = jnp.concatenate([r[pl.ds(h, n_slots, stride=N_HEADS), :] for r in v_refs], axis=0)
        pv = jnp.dot(p16[rs, :], vh.astype(BF16), preferred_element_type=F32)
        acc_scr[c, rs, :] = alpha[rs] * acc_scr[c, rs, :] + pv
    m_scr[c] = m_new


def _paged_pages(j, ktp_refs, vip_refs, qbd_scr, m_scr, l_scr, acc_scr, *, past_len, dec_seq):
    rph, _, _, _, _, slope = _paged_rows(dec_seq)
    pps = len(ktp_refs)
    ppc = pps // PAGED_CHAINS
    qbd = qbd_scr[...]
    kcol = lax.broadcasted_iota(jnp.int32, (1, ppc * PAGE_SIZE), 1)
    for c in range(PAGED_CHAINS):
        pages = range(c * ppc, (c + 1) * ppc)
        s = jnp.concatenate(
            [jnp.dot(qbd, ktp_refs[i][...].astype(BF16), preferred_element_type=F32) for i in pages], axis=-1)
        koff = (kcol + ((j * pps + c * ppc) * PAGE_SIZE - past_len)).astype(F32)
        _paged_update(c, s + slope * koff, [vip_refs[i] for i in pages], PAGE_SIZE, m_scr, l_scr, acc_scr, rph)


def _paged_finish(b, lam_refs, gsub_ref, ktn_ref, vin_ref, o_ref, qbd_scr, m_scr, l_scr, acc_scr, *,
                  dec_seq, lam_init):
    rph, n_rows, _, _, tok, slope = _paged_rows(dec_seq)
    n_new = ktn_ref.shape[1]
    sn = jnp.dot(qbd_scr[...], ktn_ref[...].astype(BF16), preferred_element_type=F32)
    col = lax.broadcasted_iota(jnp.int32, (n_rows, n_new), 1)
    ctok = col % dec_seq
    valid = jnp.logical_and(col // dec_seq == b, ctok <= tok)
    sn = jnp.where(valid, sn + slope * ctok.astype(F32), NEG_BIG)
    _paged_update(0, sn, [vin_ref], n_new, m_scr, l_scr, acc_scr, rph)

    m = m_scr[0]
    for c in range(1, PAGED_CHAINS):
        m = jnp.maximum(m, m_scr[c])
    l = jnp.zeros((n_rows, 1), F32)
    acc = jnp.zeros((n_rows, V_DIM), F32)
    for c in range(PAGED_CHAINS):
        w = jnp.exp2(m_scr[c] - m)
        l = l + w * l_scr[c]
        acc = acc + w * acc_scr[c]

    lam = _diff_lambda(*[r[...] for r in lam_refs], lam_init)
    o = acc / l
    for h in range(N_HEADS):
        oh = o[h * rph:h * rph + dec_seq] - lam * o[h * rph + dec_seq:(h + 1) * rph]
        o_ref[:, h * V_DIM:(h + 1) * V_DIM] = _rms(oh, gsub_ref[...]) * (1.0 - lam_init)


SUBLANES = 8
CONV_HALO = 32
CONV_ROW_CHUNK = 16


def _ln_swish(c, g, b):
    mu = jnp.mean(c, axis=-1, keepdims=True)
    d = c - mu
    var = jnp.mean(d * d, axis=-1, keepdims=True)
    y = d * lax.rsqrt(var + EPS) * g + b
    return y * jax.nn.sigmoid(y)


def _conv_prompt_kernel(prev_ref, cur_ref, w_ref, b_ref, g_ref, beta_ref, o_ref, ext_scr, sh_scr, *, tm):
    i = pl.program_id(1)
    n_ext = CONV_HALO + tm
    ext_scr[0:CONV_HALO, :] = jnp.where(i == 0, 0.0, prev_ref[...])
    ext_scr[CONV_HALO:n_ext, :] = cur_ref[...]
    ext = ext_scr[...]
    for r in range(1, SUBLANES):
        sh_scr[r] = pltpu.roll(ext, n_ext - r, axis=0)
    first = CONV_HALO - (CONV_WIDTH - 1)
    for c in range(0, tm, CONV_ROW_CHUNK):
        acc = jnp.broadcast_to(b_ref[...], (CONV_ROW_CHUNK, D_CONV))
        for j in range(CONV_WIDTH):
            r = (first + j) % SUBLANES
            a = c + first + j - r
            src = ext_scr[a:a + CONV_ROW_CHUNK, :] if r == 0 else sh_scr[r, a:a + CONV_ROW_CHUNK, :]
            acc = acc + jnp.concatenate([w_ref[j]] * (CONV_ROW_CHUNK // SUBLANES), axis=0) * src
        o_ref[c:c + CONV_ROW_CHUNK, :] = _ln_swish(acc, g_ref[...], beta_ref[...]).astype(o_ref.dtype)


def _conv_prompt(u2d, w_dw, b_dw, ln_g, ln_b, batch, seq, tm=512):
    nt = seq // tm
    halo_per_tile = tm // CONV_HALO
    kern = functools.partial(_conv_prompt_kernel, tm=tm)
    vec = pl.BlockSpec((1, D_CONV), lambda b, i: (0, 0))
    w_rep = jnp.broadcast_to(w_dw[:, None, :], (CONV_WIDTH, SUBLANES, D_CONV))
    return pl.pallas_call(
        kern,
        grid=(batch, nt),
        in_specs=[
            pl.BlockSpec((CONV_HALO, D_CONV),
                         lambda b, i: (jnp.maximum((b * nt + i) * halo_per_tile - 1, 0), 0)),
            pl.BlockSpec((tm, D_CONV), lambda b, i: (b * nt + i, 0)),
            pl.BlockSpec((CONV_WIDTH, SUBLANES, D_CONV), lambda b, i: (0, 0, 0)),
            vec, vec, vec,
        ],
        out_specs=pl.BlockSpec((tm, D_CONV), lambda b, i: (b * nt + i, 0)),
        out_shape=jax.ShapeDtypeStruct((batch * seq, D_CONV), BF16),
        scratch_shapes=[pltpu.VMEM((CONV_HALO + tm, D_CONV), F32),
                        pltpu.VMEM((SUBLANES, CONV_HALO + tm, D_CONV), F32)],
        compiler_params=_cparams(("parallel", "parallel")),
        name="conv_prompt",
    )(u2d, u2d, w_rep, b_dw, ln_g, ln_b)


def _conv_sample_kernel(state_ref, u_ref, w_ref, b_ref, g_ref, beta_ref, o_ref, st_ref, *, dec_batch, dec_seq):
    n_state = CONV_WIDTH - 1

    def ext(i):
        return state_ref[i] if i < n_state else u_ref[i - n_state]

    for t in range(dec_seq):
        acc = jnp.broadcast_to(b_ref[...], (dec_batch, D_CONV))
        for j in range(CONV_WIDTH):
            acc = acc + w_ref[j:j + 1, :] * ext(t + j)
        o_ref[t] = _ln_swish(acc, g_ref[...], beta_ref[...])
    for i in range(n_state):
        st_ref[i] = ext(i + dec_seq)


def _conv_sample(state_tm, u_s, w_dw, b_dw, ln_g, ln_b, dec_batch, dec_seq):
    kern = functools.partial(_conv_sample_kernel, dec_batch=dec_batch, dec_seq=dec_seq)
    u_tm = jnp.transpose(u_s.reshape(dec_batch, dec_seq, D_CONV), (1, 0, 2))
    y_tm, st_tm = pl.pallas_call(
        kern,
        out_shape=[
            jax.ShapeDtypeStruct((dec_seq, dec_batch, D_CONV), F32),
            jax.ShapeDtypeStruct(state_tm.shape, F32),
        ],
        compiler_params=pltpu.CompilerParams(vmem_limit_bytes=VMEM_LIMIT_BYTES),
        name="conv_sample",
    )(state_tm, u_tm, w_dw, b_dw, ln_g, ln_b)
    return jnp.transpose(y_tm, (1, 0, 2)).reshape(dec_batch * dec_seq, D_CONV), st_tm


def _post_first(ya_ref, yc_ref, x_ref, wo_ref, gpm_ref, gpf_ref, h_scr, hn_scr, acc_scr):
    m = (jnp.dot(ya_ref[...].astype(BF16), wo_ref[0:D_ATTN, :], preferred_element_type=F32)
         + jnp.dot(yc_ref[...].astype(BF16), wo_ref[D_ATTN:, :], preferred_element_type=F32))
    h = x_ref[...] + _rms(m, gpm_ref[...])
    h_scr[...] = h
    hn_scr[...] = _rms(h, gpf_ref[...]).astype(BF16)
    acc_scr[...] = jnp.zeros(acc_scr.shape, F32)


def _post_ffn_chunk(w1_ref, w2_ref, hn_scr, acc_scr):
    a = jnp.dot(hn_scr[...], w1_ref[...], preferred_element_type=F32)
    a = jnp.square(jnp.maximum(a, 0.0)).astype(BF16)
    acc_scr[...] += jnp.dot(a, w2_ref[...], preferred_element_type=F32)


def _post_last(pe_ref, gpost_ref, wg_ref, wp_ref, o_ref, h_scr, acc_scr):
    h2 = h_scr[...] + _rms(acc_scr[...], gpost_ref[...])
    gate = jax.nn.sigmoid(jnp.dot(h2.astype(BF16), wg_ref[...], preferred_element_type=F32))
    pev = jnp.dot(pe_ref[...].astype(BF16), wp_ref[...], preferred_element_type=F32)
    o_ref[...] = h2 + gate * pev


def _post_kernel(ya_ref, yc_ref, x_ref, pe_ref, wo_ref, gpm_ref, gpf_ref, w1_ref, w2_ref, gpost_ref,
                 wg_ref, wp_ref, o_ref, h_scr, hn_scr, acc_scr):
    k = pl.program_id(1)

    @pl.when(k == 0)
    def _():
        _post_first(ya_ref, yc_ref, x_ref, wo_ref, gpm_ref, gpf_ref, h_scr, hn_scr, acc_scr)

    _post_ffn_chunk(w1_ref, w2_ref, hn_scr, acc_scr)

    @pl.when(k == pl.num_programs(1) - 1)
    def _():
        _post_last(pe_ref, gpost_ref, wg_ref, wp_ref, o_ref, h_scr, acc_scr)


N_POST_IN = 12
N_PAGED_IN = 10


def _page_copies(pt_ref, seq, chunk, slot, kt_pool_ref, vi_pool_ref, kbuf, vbuf, sem, pps):
    copies = []
    for p in range(pps):
        page = pt_ref[seq, chunk * pps + p]
        copies.append(pltpu.make_async_copy(kt_pool_ref.at[page], kbuf.at[slot, p], sem.at[slot, 0]))
        copies.append(pltpu.make_async_copy(vi_pool_ref.at[page], vbuf.at[slot, p], sem.at[slot, 1]))
    return copies


def _post_paged_kernel(pt_ref, *refs, pps, past_len, dec_seq, lam_init):
    (ya_ref, yc_ref, x_ref, pe_ref, wo_ref, gpm_ref, gpf_ref, w1_ref, w2_ref, gpost_ref, wg_ref,
     wp_ref) = refs[:N_POST_IN]
    lam_refs = refs[N_POST_IN:N_POST_IN + 4]
    gsub_ref, q_ref, ktn_ref, vin_ref, kt_pool_ref, vi_pool_ref = refs[N_POST_IN + 4:N_POST_IN + N_PAGED_IN]
    o_ref, oa_ref = refs[N_POST_IN + N_PAGED_IN:N_POST_IN + N_PAGED_IN + 2]
    (h_scr, hn_scr, acc_scr, q2_scr, qbd_scr, m_scr, l_scr, pacc_scr, kbuf, vbuf,
     sem) = refs[N_POST_IN + N_PAGED_IN + 2:]
    i = pl.program_id(0)
    k = pl.program_id(1)
    n_k = pl.num_programs(1)
    step = i * n_k + k
    slot = step % 2
    copies = functools.partial(_page_copies, pt_ref, kt_pool_ref=kt_pool_ref, vi_pool_ref=vi_pool_ref,
                               kbuf=kbuf, vbuf=vbuf, sem=sem, pps=pps)

    @pl.when(step == 0)
    def _():
        for c in copies(0, 0, 0):
            c.start()

    @pl.when(step + 1 < pl.num_programs(0) * n_k)
    def _():
        nxt = step + 1
        for c in copies(nxt // n_k, nxt % n_k, 1 - slot):
            c.start()

    @pl.when(k == 0)
    def _():
        _post_first(ya_ref, yc_ref, x_ref, wo_ref, gpm_ref, gpf_ref, h_scr, hn_scr, acc_scr)
        _paged_init(q_ref, q2_scr, qbd_scr, m_scr, l_scr, pacc_scr, dec_seq=dec_seq)

    _post_ffn_chunk(w1_ref, w2_ref, hn_scr, acc_scr)
    for c in copies(i, k, slot):
        c.wait()
    _paged_pages(k, [kbuf.at[slot, p] for p in range(pps)], [vbuf.at[slot, p] for p in range(pps)],
                 qbd_scr, m_scr, l_scr, pacc_scr, past_len=past_len, dec_seq=dec_seq)

    @pl.when(k == n_k - 1)
    def _():
        _post_last(pe_ref, gpost_ref, wg_ref, wp_ref, o_ref, h_scr, acc_scr)
        _paged_finish(i, lam_refs, gsub_ref, ktn_ref, vin_ref, oa_ref, qbd_scr, m_scr, l_scr, pacc_scr,
                      dec_seq=dec_seq, lam_init=lam_init)


POST_ROW_TILE = 512
POST_FF_CHUNK = 1024


def _post_specs(tm, tf, imap):
    row_tile = lambda n: pl.BlockSpec((tm, n), imap(lambda i, k: (i, 0)))
    vec = pl.BlockSpec((1, D_MODEL), imap(lambda i, k: (0, 0)))
    resident = lambda r, c: pl.BlockSpec((r, c), imap(lambda i, k: (0, 0)), pipeline_mode=pl.Buffered(1))
    in_specs = [
        row_tile(D_ATTN), row_tile(D_CONV), row_tile(D_MODEL), row_tile(D_PLE),
        resident(D_MODEL, D_MODEL), vec, vec,
        pl.BlockSpec((D_MODEL, tf), imap(lambda i, k: (0, k))),
        pl.BlockSpec((tf, D_MODEL), imap(lambda i, k: (k, 0))),
        vec,
        resident(D_MODEL, D_MODEL), resident(D_PLE, D_MODEL),
    ]
    scratch = [pltpu.VMEM((tm, D_MODEL), F32), pltpu.VMEM((tm, D_MODEL), BF16), pltpu.VMEM((tm, D_MODEL), F32)]
    return in_specs, row_tile(D_MODEL), scratch


def _post(post_args, tm):
    m = post_args[2].shape[0]
    in_specs, out_spec, scratch = _post_specs(tm, POST_FF_CHUNK, lambda f: f)
    return pl.pallas_call(
        _post_kernel,
        grid=(m // tm, D_FF // POST_FF_CHUNK),
        in_specs=in_specs,
        out_specs=out_spec,
        out_shape=jax.ShapeDtypeStruct((m, D_MODEL), F32),
        scratch_shapes=scratch,
        compiler_params=_cparams(("parallel", "arbitrary")),
        name="post",
    )(*post_args)


def _post_paged(post_args, q_s, kt_new, vi_new, kt_pool, vi_pool, page_table, lam_vecs, g_subln,
                dec_batch, dec_seq, lam_init):
    tm, tf = POST_ROW_TILE, POST_FF_CHUNK
    m = post_args[2].shape[0]
    n_row_tiles, n_ff = m // tm, D_FF // tf
    n_pages = page_table.shape[1]
    assert n_row_tiles == dec_batch and n_pages % n_ff == 0
    pps = n_pages // n_ff
    assert pps % PAGED_CHAINS == 0
    past_len = n_pages * PAGE_SIZE
    rph = 2 * dec_seq
    n_rows = N_HEADS * rph
    n_new = dec_batch * dec_seq
    q3 = q_s.astype(F32).reshape(dec_batch, dec_seq, D_ATTN)
    with_pt = lambda f: (lambda i, k, pt: f(i, k))
    post_in, post_out, post_scratch = _post_specs(tm, tf, with_pt)
    vec = lambda n: pl.BlockSpec((1, n), lambda i, k, pt: (0, 0))

    grid_spec = pltpu.PrefetchScalarGridSpec(
        num_scalar_prefetch=1,
        grid=(n_row_tiles, n_ff),
        in_specs=post_in + [vec(HEAD_DIM)] * 4 + [
            vec(V_DIM),
            pl.BlockSpec((None, dec_seq, D_ATTN), lambda i, k, pt: (i, 0, 0)),
            pl.BlockSpec((D_ATTN, n_new), lambda i, k, pt: (0, 0)),
            pl.BlockSpec((n_new * N_HEADS, V_DIM), lambda i, k, pt: (0, 0)),
            pl.BlockSpec(memory_space=pl.ANY),
            pl.BlockSpec(memory_space=pl.ANY),
        ],
        out_specs=[post_out, pl.BlockSpec((None, dec_seq, D_ATTN), lambda i, k, pt: (i, 0, 0))],
        scratch_shapes=post_scratch + [
            pltpu.VMEM((rph, D_ATTN), F32),
            pltpu.VMEM((n_rows, D_ATTN), BF16),
            pltpu.VMEM((PAGED_CHAINS, n_rows, 1), F32),
            pltpu.VMEM((PAGED_CHAINS, n_rows, 1), F32),
            pltpu.VMEM((PAGED_CHAINS, n_rows, V_DIM), F32),
            pltpu.VMEM((2, pps, D_ATTN, PAGE_SIZE), F32),
            pltpu.VMEM((2, pps, PAGE_SIZE * N_HEADS, V_DIM), F32),
            pltpu.SemaphoreType.DMA((2, 2)),
        ],
    )
    kern = functools.partial(_post_paged_kernel, pps=pps, past_len=past_len, dec_seq=dec_seq, lam_init=lam_init)
    out, ya_s = pl.pallas_call(
        kern,
        grid_spec=grid_spec,
        out_shape=[jax.ShapeDtypeStruct((m, D_MODEL), F32),
                   jax.ShapeDtypeStruct((dec_batch, dec_seq, D_ATTN), F32)],
        compiler_params=_cparams(("arbitrary", "arbitrary")),
        name="post_paged",
    )(page_table, *post_args, *lam_vecs, g_subln, q3, kt_new, vi_new, kt_pool, vi_pool)
    return out, ya_s.reshape(dec_batch * dec_seq, D_ATTN)


def _row(v):
    return v.reshape(1, -1)


def kernel(x_prompt, x_sample, cache_k, cache_v, state_conv, page_table, p_prompt, p_sample,
           w_in, w_out, lambda_q1, lambda_k1, lambda_q2, lambda_k2, g_subln, w_dw, b_dw,
           ln_conv_g, ln_conv_b, g_pre_mix, g_post_mix, g_pre_ffn, g_post_ffn,
           w_ff1, w_ff2, w_ple, w_ple_gate):
    depth = w_in.shape[0]
    batch, seq, _ = x_prompt.shape
    dec_batch, dec_seq, _ = x_sample.shape
    n_pool = cache_k.shape[1]
    mp, ms = batch * seq, dec_batch * dec_seq
    assert 2 * dec_seq == 8 and cache_k.shape[2] == PAGE_SIZE

    hp = x_prompt.reshape(mp, D_MODEL)
    hs = x_sample.reshape(ms, D_MODEL)
    outs = {k: [] for k in ("kp", "vp", "cp", "ks", "vs", "cs")}
    for l in range(depth):
        lam_init = 0.8 - 0.6 * math.exp(-0.3 * l)
        w_in16 = w_in[l].astype(BF16)
        w_out16 = w_out[l].astype(BF16)
        w_ff1_16 = w_ff1[l].astype(BF16)
        w_ff2_16 = w_ff2[l].astype(BF16)
        w_gate16 = w_ple_gate[l].astype(BF16)
        w_ple16 = w_ple[l].astype(BF16)
        lam_vecs = [_row(lambda_q1[l]), _row(lambda_k1[l]), _row(lambda_q2[l]), _row(lambda_k2[l])]
        gsub = _row(g_subln[l])
        conv_args = (w_dw[l], _row(b_dw[l]), _row(ln_conv_g[l]), _row(ln_conv_b[l]))
        kt_pool = jnp.transpose(cache_k[l], (0, 2, 3, 4, 1)).reshape(n_pool, D_ATTN, PAGE_SIZE)
        vi_pool = cache_v[l].reshape(n_pool, PAGE_SIZE * N_HEADS, V_DIM)
        post_w = (w_out16, _row(g_post_mix[l]), _row(g_pre_ffn[l]), w_ff1_16, w_ff2_16, _row(g_post_ffn[l]),
                  w_gate16, w_ple16)

        q_p, kt_p, vi_p, v16_p, u_p = _inproj(hp, _row(g_pre_mix[l]), w_in16, n_seq=batch, tm=512)
        ya_p = _attn_prompt(q_p, kt_p, v16_p, lam_vecs, gsub, batch, seq, lam_init)
        yc_p = _conv_prompt(u_p, *conv_args, batch, seq)
        q_s, kt_s, vi_s, _, u_s = _inproj(hs, _row(g_pre_mix[l]), w_in16, n_seq=1, tm=ms)
        yc_s, st_tm = _conv_sample(jnp.transpose(state_conv[l], (1, 0, 2)), u_s, *conv_args,
                                   dec_batch, dec_seq)

        hp, ya_s = _post_paged((ya_p, yc_p, hp, p_prompt[l].reshape(mp, D_PLE)) + post_w,
                               q_s, kt_s[0], vi_s, kt_pool, vi_pool, page_table, lam_vecs, gsub,
                               dec_batch, dec_seq, lam_init)
        hs = _post((ya_s, yc_s, hs, p_sample[l].reshape(ms, D_PLE)) + post_w, tm=ms)

        outs["kp"].append(jnp.transpose(kt_p.reshape(batch, N_HEADS, 2, HEAD_DIM, seq), (0, 4, 1, 2, 3)))
        outs["vp"].append(vi_p.reshape(batch, seq, N_HEADS, V_DIM))
        outs["cp"].append(u_p.reshape(batch, seq, D_CONV)[:, seq - (CONV_WIDTH - 1):])
        outs["ks"].append(jnp.transpose(kt_s.reshape(N_HEADS, 2, HEAD_DIM, dec_batch, dec_seq),
                                        (3, 4, 0, 1, 2)))
        outs["vs"].append(vi_s.reshape(dec_batch, dec_seq, N_HEADS, V_DIM))
        outs["cs"].append(jnp.transpose(st_tm, (1, 0, 2)))

    return (hp.reshape(batch, seq, D_MODEL), hs.reshape(dec_batch, dec_seq, D_MODEL),
            jnp.stack(outs["kp"]), jnp.stack(outs["vp"]), jnp.stack(outs["cp"]),
            jnp.stack(outs["ks"]), jnp.stack(outs["vs"]), jnp.stack(outs["cs"]))
```

```python
import functools
import math

import jax
import jax.numpy as jnp
from jax import lax
from jax.experimental import pallas as pl
from jax.experimental.pallas import tpu as pltpu

D_MODEL = 1024
D_ATTN = 512
D_CONV = 512
N_HEADS = 4
HEAD_DIM = 64
V_DIM = 128
CONV_WIDTH = 31
D_FF = 4096
D_PLE = 256
PAGE_SIZE = 128
EPS = 1e-6
N_QKV = 3 * D_ATTN
N_IN = N_QKV + 2 * D_CONV
NEG_BIG = -1e30
LOG2E = math.log2(math.e)
Q_SCALE = HEAD_DIM ** -0.5 * LOG2E
ALIBI_SLOPES = tuple(2.0 ** (-8.0 * (h + 1) / N_HEADS) * LOG2E for h in range(N_HEADS))

F32 = jnp.float32
BF16 = jnp.bfloat16

VMEM_LIMIT_BYTES = 56 * 1024 * 1024
PROMPT_ROW_TILE = 512


def _cparams(semantics):
    return pltpu.CompilerParams(dimension_semantics=semantics, vmem_limit_bytes=VMEM_LIMIT_BYTES)


def _rms(x, g):
    return x * lax.rsqrt(jnp.mean(x * x, axis=-1, keepdims=True) + EPS) * g


def _diff_lambda(lq1, lk1, lq2, lk2, lam_init):
    return (jnp.exp(jnp.sum(lq1 * lk1, axis=-1, keepdims=True))
            - jnp.exp(jnp.sum(lq2 * lk2, axis=-1, keepdims=True)) + lam_init)


N_INPROJ_OUT = 5


def _inproj_kernel(x_ref, g_ref, w_ref, *refs, tm):
    n_cast = (len(refs) - N_INPROJ_OUT) // 2
    q_ref, kt_ref, vi_ref, v16_ref, u_ref = refs[n_cast:n_cast + N_INPROJ_OUT]
    for src_ref, dst_ref in zip(refs[:n_cast], refs[n_cast + N_INPROJ_OUT:]):
        dst_ref[...] = src_ref[...].astype(BF16)
    xn = _rms(x_ref[...], g_ref[...]).astype(BF16)
    q = jnp.dot(xn, w_ref[:, 0:D_ATTN], preferred_element_type=F32)
    q_ref[...] = (q * Q_SCALE).astype(BF16)
    kt_ref[...] = jnp.dot(xn, w_ref[:, D_ATTN:2 * D_ATTN], preferred_element_type=F32).T
    v = jnp.dot(xn, w_ref[:, 2 * D_ATTN:N_QKV], preferred_element_type=F32)
    v16_ref[...] = v.astype(BF16)
    for h in range(N_HEADS):
        vi_ref[pl.ds(h, tm, stride=N_HEADS), :] = v[:, h * V_DIM:(h + 1) * V_DIM]
    a = jnp.dot(xn, w_ref[:, N_QKV:N_QKV + D_CONV], preferred_element_type=F32)
    gt = jnp.dot(xn, w_ref[:, N_QKV + D_CONV:], preferred_element_type=F32)
    u_ref[...] = a * jax.nn.sigmoid(gt)


def _inproj(x2d, g_pre_mix, w_in16, n_seq, tm, cast_weights=()):
    m = x2d.shape[0]
    seq = m // n_seq
    nt = seq // tm
    n_steps = m // tm
    cast_specs = [pl.BlockSpec((w.shape[0] // n_steps, w.shape[1]), lambda i: (i, 0)) for w in cast_weights]
    return pl.pallas_call(
        functools.partial(_inproj_kernel, tm=tm),
        grid=(n_steps,),
        in_specs=[
            pl.BlockSpec((tm, D_MODEL), lambda i: (i, 0)),
            pl.BlockSpec((1, D_MODEL), lambda i: (0, 0)),
            pl.BlockSpec((D_MODEL, N_IN), lambda i: (0, 0)),
        ] + cast_specs,
        out_specs=[
            pl.BlockSpec((tm, D_ATTN), lambda i: (i, 0)),
            pl.BlockSpec((None, D_ATTN, tm), lambda i: (i // nt, 0, i % nt)),
            pl.BlockSpec((tm * N_HEADS, V_DIM), lambda i: (i, 0)),
            pl.BlockSpec((tm, D_ATTN), lambda i: (i, 0)),
            pl.BlockSpec((tm, D_CONV), lambda i: (i, 0)),
        ] + cast_specs,
        out_shape=[
            jax.ShapeDtypeStruct((m, D_ATTN), BF16),
            jax.ShapeDtypeStruct((n_seq, D_ATTN, seq), F32),
            jax.ShapeDtypeStruct((m * N_HEADS, V_DIM), F32),
            jax.ShapeDtypeStruct((m, D_ATTN), BF16),
            jax.ShapeDtypeStruct((m, D_CONV), F32),
        ] + [jax.ShapeDtypeStruct(w.shape, BF16) for w in cast_weights],
        compiler_params=_cparams(("parallel",)),
        name="inproj",
    )(x2d, g_pre_mix, w_in16, *cast_weights)


ATTN_TILE = 1024
ATTN_KEY_TILE = 256
ATTN_ROW_BLOCK = 64
LANES = 128


def _attn_prompt_kernel(lq1_ref, lk1_ref, lq2_ref, lk2_ref, gsub_ref, q_ref, kt_ref, v_ref, o_ref,
                        qs_scr, m_scr, l_scr, acc_scr, *, lam_init):
    t, w, rb = ATTN_TILE, ATTN_KEY_TILE, ATTN_ROW_BLOCK
    n_rb = t // rb
    qi = pl.program_id(1)
    lane = lax.broadcasted_iota(jnp.int32, (rb, 2 * HEAD_DIM), 1)
    kcol = lax.broadcasted_iota(jnp.int32, (1, w), 1)

    for h in range(N_HEADS):
        hs = slice(h * V_DIM, (h + 1) * V_DIM)
        for g in range(n_rb):
            q = q_ref[g * rb:(g + 1) * rb, hs]
            zero = jnp.zeros_like(q)
            qs_scr[h, g] = jnp.concatenate([jnp.where(lane < HEAD_DIM, q, zero),
                                            jnp.where(lane >= HEAD_DIM, q, zero)], axis=0)
    m_scr[...] = jnp.full(m_scr.shape, NEG_BIG, F32)
    l_scr[...] = jnp.zeros(l_scr.shape, F32)
    acc_scr[...] = jnp.zeros(acc_scr.shape, F32)

    def tile(k0, d):
        koff = (kcol + (k0 - qi * t)).astype(F32)
        for h in range(N_HEADS):
            hs = slice(h * V_DIM, (h + 1) * V_DIM)
            kt = kt_ref[hs, pl.ds(k0, w)].astype(BF16)
            vv = v_ref[pl.ds(k0, w), hs]
            bias = ALIBI_SLOPES[h] * koff
            for g in range(n_rb):
                masked = False
                if d is not None:
                    if (g + 1) * rb <= d * w:
                        continue
                    masked = g * rb < (d + 1) * w - 1
                s = jnp.dot(qs_scr[h, g], kt, preferred_element_type=F32) + bias
                if masked:
                    row = lax.broadcasted_iota(jnp.int32, (2 * rb, w), 0)
                    row = jnp.where(row >= rb, row - rb, row) + (g * rb - d * w)
                    col = lax.broadcasted_iota(jnp.int32, (2 * rb, w), 1)
                    s = jnp.where(col <= row, s, NEG_BIG)
                m_prev = m_scr[h, g]
                m_new = jnp.maximum(m_prev, jnp.max(s, axis=-1, keepdims=True))
                alpha = jnp.exp2(m_prev - m_new)
                p = jnp.exp2(s - jnp.concatenate([m_new] * (w // LANES), axis=-1))
                l_scr[h, g] = alpha * l_scr[h, g] + jnp.sum(p, axis=-1, keepdims=True)
                acc_scr[h, g] = alpha * acc_scr[h, g] + jnp.dot(p.astype(BF16), vv,
                                                               preferred_element_type=F32)
                m_scr[h, g] = m_new

    def below_diagonal(kv, carry):
        tile(pl.multiple_of(kv * w, w), None)
        return carry

    lax.fori_loop(0, qi * (t // w), below_diagonal, 0)
    for d in range(t // w):
        tile(pl.multiple_of(qi * t + d * w, w), d)

    lam = _diff_lambda(lq1_ref[...], lk1_ref[...], lq2_ref[...], lk2_ref[...], lam_init)
    for h in range(N_HEADS):
        for g in range(n_rb):
            o = acc_scr[h, g] / l_scr[h, g]
            oh = o[:rb] - lam * o[rb:]
            y = _rms(oh, gsub_ref[...]) * (1.0 - lam_init)
            o_ref[g * rb:(g + 1) * rb, h * V_DIM:(h + 1) * V_DIM] = y.astype(o_ref.dtype)


def _attn_prompt(q16, kt, v16, lam_vecs, g_subln, batch, seq, lam_init):
    t, rb = ATTN_TILE, ATTN_ROW_BLOCK
    nq = seq // t
    n_rb = t // rb
    kern = functools.partial(_attn_prompt_kernel, lam_init=lam_init)
    vec = lambda n: pl.BlockSpec((1, n), lambda b, qi: (0, 0))
    return pl.pallas_call(
        kern,
        grid=(batch, nq),
        in_specs=[vec(HEAD_DIM)] * 4 + [
            vec(V_DIM),
            pl.BlockSpec((t, D_ATTN), lambda b, qi: (b * nq + qi, 0)),
            pl.BlockSpec((None, D_ATTN, seq), lambda b, qi: (b, 0, 0)),
            pl.BlockSpec((seq, D_ATTN), lambda b, qi: (b, 0)),
        ],
        out_specs=pl.BlockSpec((t, D_ATTN), lambda b, qi: (b * nq + qi, 0)),
        out_shape=jax.ShapeDtypeStruct((batch * seq, D_ATTN), BF16),
        scratch_shapes=[
            pltpu.VMEM((N_HEADS, n_rb, 2 * rb, 2 * HEAD_DIM), BF16),
            pltpu.VMEM((N_HEADS, n_rb, 2 * rb, LANES), F32),
            pltpu.VMEM((N_HEADS, n_rb, 2 * rb, LANES), F32),
            pltpu.VMEM((N_HEADS, n_rb, 2 * rb, V_DIM), F32),
        ],
        compiler_params=_cparams(("parallel", "arbitrary")),
        name="attn_prompt",
    )(*lam_vecs, g_subln, q16, kt, v16)


PAGED_CHAINS = 1


def _paged_rows(dec_seq):
    rph = 2 * dec_seq
    n_rows = N_HEADS * rph
    row = lax.broadcasted_iota(jnp.int32, (n_rows, 1), 0)
    head = row // rph
    comp = (row % rph) // dec_seq
    tok = row % dec_seq
    slope = jnp.zeros((n_rows, 1), F32)
    for h in range(N_HEADS):
        slope = jnp.where(head == h, ALIBI_SLOPES[h], slope)
    return rph, n_rows, head, comp, tok, slope


def _paged_init(q_ref, q2_scr, qbd_scr, m_scr, l_scr, acc_scr, *, dec_seq):
    rph, n_rows, head, comp, _, _ = _paged_rows(dec_seq)
    q = q_ref[...]
    q2_scr[0:dec_seq, :] = q
    q2_scr[dec_seq:rph, :] = q
    qrep = jnp.concatenate([q2_scr[...]] * N_HEADS, axis=0)
    col = lax.broadcasted_iota(jnp.int32, (n_rows, D_ATTN), 1)
    qbd_scr[...] = jnp.where(col // HEAD_DIM == head * 2 + comp, qrep, 0.0).astype(BF16)
    m_scr[...] = jnp.full(m_scr.shape, NEG_BIG, F32)
    l_scr[...] = jnp.zeros(l_scr.shape, F32)
    acc_scr[...] = jnp.zeros(acc_scr.shape, F32)


def _paged_update(c, s, v_refs, n_slots, m_scr, l_scr, acc_scr, rph):
    m_prev = m_scr[c]
    m_new = jnp.maximum(m_prev, jnp.max(s, axis=-1, keepdims=True))
    alpha = jnp.exp2(m_prev - m_new)
    p = jnp.exp2(s - m_new)
    l_scr[c] = alpha * l_scr[c] + jnp.sum(p, axis=-1, keepdims=True)
    p16 = p.astype(BF16)
    for h in range(N_HEADS):
        rs = slice(h * rph, (h + 1) * rph)
        vh = jnp.concatenate([r[pl.ds(h, n_slots, stride=N_HEADS), :] for r in v_refs], axis=0)
        pv = jnp.dot(p16[rs, :], vh.astype(BF16), preferred_element_type=F32)
        acc_scr[c, rs, :] = alpha[rs] * acc_scr[c, rs, :] + pv
    m_scr[c] = m_new


def _paged_pages(j, ktp_refs, vip_refs, qbd_scr, m_scr, l_scr, acc_scr, *, past_len, dec_seq):
    rph, _, _, _, _, slope = _paged_rows(dec_seq)
    pps = len(ktp_refs)
    ppc = pps // PAGED_CHAINS
    qbd = qbd_scr[...]
    kcol = lax.broadcasted_iota(jnp.int32, (1, ppc * PAGE_SIZE), 1)
    for c in range(PAGED_CHAINS):
        pages = range(c * ppc, (c + 1) * ppc)
        s = jnp.concatenate(
            [jnp.dot(qbd, ktp_refs[i][...].astype(BF16), preferred_element_type=F32) for i in pages], axis=-1)
        koff = (kcol + ((j * pps + c * ppc) * PAGE_SIZE - past_len)).astype(F32)
        _paged_update(c, s + slope * koff, [vip_refs[i] for i in pages], PAGE_SIZE, m_scr, l_scr, acc_scr, rph)


def _paged_finish(b, lam_refs, gsub_ref, ktn_ref, vin_ref, o_ref, qbd_scr, m_scr, l_scr, acc_scr, *,
                  dec_seq, lam_init):
    rph, n_rows, _, _, tok, slope = _paged_rows(dec_seq)
    n_new = ktn_ref.shape[1]
    sn = jnp.dot(qbd_scr[...], ktn_ref[...].astype(BF16), preferred_element_type=F32)
    col = lax.broadcasted_iota(jnp.int32, (n_rows, n_new), 1)
    ctok = col % dec_seq
    valid = jnp.logical_and(col // dec_seq == b, ctok <= tok)
    sn = jnp.where(valid, sn + slope * ctok.astype(F32), NEG_BIG)
    _paged_update(0, sn, [vin_ref], n_new, m_scr, l_scr, acc_scr, rph)

    m = m_scr[0]
    for c in range(1, PAGED_CHAINS):
        m = jnp.maximum(m, m_scr[c])
    l = jnp.zeros((n_rows, 1), F32)
    acc = jnp.zeros((n_rows, V_DIM), F32)
    for c in range(PAGED_CHAINS):
        w = jnp.exp2(m_scr[c] - m)
        l = l + w * l_scr[c]
        acc = acc + w * acc_scr[c]

    lam = _diff_lambda(*[r[...] for r in lam_refs], lam_init)
    o = acc / l
    for h in range(N_HEADS):
        oh = o[h * rph:h * rph + dec_seq] - lam * o[h * rph + dec_seq:(h + 1) * rph]
        o_ref[:, h * V_DIM:(h + 1) * V_DIM] = _rms(oh, gsub_ref[...]) * (1.0 - lam_init)


SUBLANES = 8
CONV_HALO = 32
CONV_ROW_CHUNK = 16


def _ln_swish(c, g, b):
    mu = jnp.mean(c, axis=-1, keepdims=True)
    d = c - mu
    var = jnp.mean(d * d, axis=-1, keepdims=True)
    y = d * lax.rsqrt(var + EPS) * g + b
    return y * jax.nn.sigmoid(y)


def _conv_prompt_kernel(prev_ref, cur_ref, w_ref, b_ref, g_ref, beta_ref, o_ref, ext_scr, sh_scr, *, tm):
    i = pl.program_id(1)
    n_ext = CONV_HALO + tm
    ext_scr[0:CONV_HALO, :] = jnp.where(i == 0, 0.0, prev_ref[...])
    ext_scr[CONV_HALO:n_ext, :] = cur_ref[...]
    ext = ext_scr[...]
    for r in range(1, SUBLANES):
        sh_scr[r] = pltpu.roll(ext, n_ext - r, axis=0)
    first = CONV_HALO - (CONV_WIDTH - 1)
    for c in range(0, tm, CONV_ROW_CHUNK):
        acc = jnp.broadcast_to(b_ref[...], (CONV_ROW_CHUNK, D_CONV))
        for j in range(CONV_WIDTH):
            r = (first + j) % SUBLANES
            a = c + first + j - r
            src = ext_scr[a:a + CONV_ROW_CHUNK, :] if r == 0 else sh_scr[r, a:a + CONV_ROW_CHUNK, :]
            acc = acc + jnp.concatenate([w_ref[j]] * (CONV_ROW_CHUNK // SUBLANES), axis=0) * src
        o_ref[c:c + CONV_ROW_CHUNK, :] = _ln_swish(acc, g_ref[...], beta_ref[...]).astype(o_ref.dtype)


def _conv_prompt(u2d, w_dw, b_dw, ln_g, ln_b, batch, seq, tm=PROMPT_ROW_TILE):
    nt = seq // tm
    halo_per_tile = tm // CONV_HALO
    kern = functools.partial(_conv_prompt_kernel, tm=tm)
    vec = pl.BlockSpec((1, D_CONV), lambda b, i: (0, 0))
    w_rep = jnp.broadcast_to(w_dw[:, None, :], (CONV_WIDTH, SUBLANES, D_CONV))
    return pl.pallas_call(
        kern,
        grid=(batch, nt),
        in_specs=[
            pl.BlockSpec((CONV_HALO, D_CONV),
                         lambda b, i: (jnp.maximum((b * nt + i) * halo_per_tile - 1, 0), 0)),
            pl.BlockSpec((tm, D_CONV), lambda b, i: (b * nt + i, 0)),
            pl.BlockSpec((CONV_WIDTH, SUBLANES, D_CONV), lambda b, i: (0, 0, 0)),
            vec, vec, vec,
        ],
        out_specs=pl.BlockSpec((tm, D_CONV), lambda b, i: (b * nt + i, 0)),
        out_shape=jax.ShapeDtypeStruct((batch * seq, D_CONV), BF16),
        scratch_shapes=[pltpu.VMEM((CONV_HALO + tm, D_CONV), F32),
                        pltpu.VMEM((SUBLANES, CONV_HALO + tm, D_CONV), F32)],
        compiler_params=_cparams(("parallel", "parallel")),
        name="conv_prompt",
    )(u2d, u2d, w_rep, b_dw, ln_g, ln_b)


def _conv_sample_kernel(state_ref, u_ref, w_ref, b_ref, g_ref, beta_ref, o_ref, st_ref, *, dec_batch, dec_seq):
    n_state = CONV_WIDTH - 1

    def ext(i):
        return state_ref[i] if i < n_state else u_ref[i - n_state]

    for t in range(dec_seq):
        acc = jnp.broadcast_to(b_ref[...], (dec_batch, D_CONV))
        for j in range(CONV_WIDTH):
            acc = acc + w_ref[j:j + 1, :] * ext(t + j)
        o_ref[t] = _ln_swish(acc, g_ref[...], beta_ref[...])
    for i in range(n_state):
        st_ref[i] = ext(i + dec_seq)


def _conv_sample(state_tm, u_s, w_dw, b_dw, ln_g, ln_b, dec_batch, dec_seq):
    kern = functools.partial(_conv_sample_kernel, dec_batch=dec_batch, dec_seq=dec_seq)
    u_tm = jnp.transpose(u_s.reshape(dec_batch, dec_seq, D_CONV), (1, 0, 2))
    y_tm, st_tm = pl.pallas_call(
        kern,
        out_shape=[
            jax.ShapeDtypeStruct((dec_seq, dec_batch, D_CONV), F32),
            jax.ShapeDtypeStruct(state_tm.shape, F32),
        ],
        compiler_params=pltpu.CompilerParams(vmem_limit_bytes=VMEM_LIMIT_BYTES),
        name="conv_sample",
    )(state_tm, u_tm, w_dw, b_dw, ln_g, ln_b)
    return jnp.transpose(y_tm, (1, 0, 2)).reshape(dec_batch * dec_seq, D_CONV), st_tm


def _post_first(ya_ref, yc_ref, x_ref, wo_ref, gpm_ref, gpf_ref, h_scr, hn_scr, acc_scr):
    y_mix = jnp.concatenate([ya_ref[...].astype(BF16), yc_ref[...].astype(BF16)], axis=-1)
    m = jnp.dot(y_mix, wo_ref[...], preferred_element_type=F32)
    h = x_ref[...] + _rms(m, gpm_ref[...])
    h_scr[...] = h
    hn_scr[...] = _rms(h, gpf_ref[...]).astype(BF16)
    acc_scr[...] = jnp.zeros(acc_scr.shape, F32)


def _post_ffn_chunk(w1_ref, w2_ref, hn_scr, acc_scr):
    a = jnp.dot(hn_scr[...], w1_ref[...], preferred_element_type=F32)
    a = jnp.square(jnp.maximum(a, 0.0)).astype(BF16)
    acc_scr[...] += jnp.dot(a, w2_ref[...], preferred_element_type=F32)


def _post_last(pe_ref, gpost_ref, wg_ref, wp_ref, o_ref, h_scr, acc_scr):
    h2 = h_scr[...] + _rms(acc_scr[...], gpost_ref[...])
    gate = jax.nn.sigmoid(jnp.dot(h2.astype(BF16), wg_ref[...], preferred_element_type=F32))
    pev = jnp.dot(pe_ref[...].astype(BF16), wp_ref[...], preferred_element_type=F32)
    o_ref[...] = h2 + gate * pev


def _post_kernel(ya_ref, yc_ref, x_ref, pe_ref, wo_ref, gpm_ref, gpf_ref, w1_ref, w2_ref, gpost_ref,
                 wg_ref, wp_ref, o_ref, h_scr, hn_scr, acc_scr):
    k = pl.program_id(1)

    @pl.when(k == 0)
    def _():
        _post_first(ya_ref, yc_ref, x_ref, wo_ref, gpm_ref, gpf_ref, h_scr, hn_scr, acc_scr)

    _post_ffn_chunk(w1_ref, w2_ref, hn_scr, acc_scr)

    @pl.when(k == pl.num_programs(1) - 1)
    def _():
        _post_last(pe_ref, gpost_ref, wg_ref, wp_ref, o_ref, h_scr, acc_scr)


N_POST_IN = 12
N_PAGED_IN = 10


def _page_copies(pt_ref, seq, chunk, slot, kt_pool_ref, vi_pool_ref, kbuf, vbuf, sem, pps):
    copies = []
    for p in range(pps):
        page = pt_ref[seq, chunk * pps + p]
        copies.append(pltpu.make_async_copy(kt_pool_ref.at[page], kbuf.at[slot, p], sem.at[slot, 0]))
        copies.append(pltpu.make_async_copy(vi_pool_ref.at[page], vbuf.at[slot, p], sem.at[slot, 1]))
    return copies


def _post_paged_kernel(pt_ref, *refs, pps, past_len, dec_seq, lam_init):
    (ya_ref, yc_ref, x_ref, pe_ref, wo_ref, gpm_ref, gpf_ref, w1_ref, w2_ref, gpost_ref, wg_ref,
     wp_ref) = refs[:N_POST_IN]
    lam_refs = refs[N_POST_IN:N_POST_IN + 4]
    gsub_ref, q_ref, ktn_ref, vin_ref, kt_pool_ref, vi_pool_ref = refs[N_POST_IN + 4:N_POST_IN + N_PAGED_IN]
    o_ref, oa_ref = refs[N_POST_IN + N_PAGED_IN:N_POST_IN + N_PAGED_IN + 2]
    (h_scr, hn_scr, acc_scr, q2_scr, qbd_scr, m_scr, l_scr, pacc_scr, kbuf, vbuf,
     sem) = refs[N_POST_IN + N_PAGED_IN + 2:]
    i = pl.program_id(0)
    k = pl.program_id(1)
    n_k = pl.num_programs(1)
    step = i * n_k + k
    slot = step % 2
    copies = functools.partial(_page_copies, pt_ref, kt_pool_ref=kt_pool_ref, vi_pool_ref=vi_pool_ref,
                               kbuf=kbuf, vbuf=vbuf, sem=sem, pps=pps)

    @pl.when(step == 0)
    def _():
        for c in copies(0, 0, 0):
            c.start()

    @pl.when(step + 1 < pl.num_programs(0) * n_k)
    def _():
        nxt = step + 1
        for c in copies(nxt // n_k, nxt % n_k, 1 - slot):
            c.start()

    @pl.when(k == 0)
    def _():
        _post_first(ya_ref, yc_ref, x_ref, wo_ref, gpm_ref, gpf_ref, h_scr, hn_scr, acc_scr)
        _paged_init(q_ref, q2_scr, qbd_scr, m_scr, l_scr, pacc_scr, dec_seq=dec_seq)

    _post_ffn_chunk(w1_ref, w2_ref, hn_scr, acc_scr)
    for c in copies(i, k, slot):
        c.wait()
    _paged_pages(k, [kbuf.at[slot, p] for p in range(pps)], [vbuf.at[slot, p] for p in range(pps)],
                 qbd_scr, m_scr, l_scr, pacc_scr, past_len=past_len, dec_seq=dec_seq)

    @pl.when(k == n_k - 1)
    def _():
        _post_last(pe_ref, gpost_ref, wg_ref, wp_ref, o_ref, h_scr, acc_scr)
        _paged_finish(i, lam_refs, gsub_ref, ktn_ref, vin_ref, oa_ref, qbd_scr, m_scr, l_scr, pacc_scr,
                      dec_seq=dec_seq, lam_init=lam_init)


POST_ROW_TILE = PROMPT_ROW_TILE
POST_FF_CHUNK = 1024


def _post_specs(tm, tf, imap):
    row_tile = lambda n: pl.BlockSpec((tm, n), imap(lambda i, k: (i, 0)))
    vec = pl.BlockSpec((1, D_MODEL), imap(lambda i, k: (0, 0)))
    resident = lambda r, c: pl.BlockSpec((r, c), imap(lambda i, k: (0, 0)), pipeline_mode=pl.Buffered(1))
    in_specs = [
        row_tile(D_ATTN), row_tile(D_CONV), row_tile(D_MODEL), row_tile(D_PLE),
        resident(D_MODEL, D_MODEL), vec, vec,
        pl.BlockSpec((D_MODEL, tf), imap(lambda i, k: (0, k))),
        pl.BlockSpec((tf, D_MODEL), imap(lambda i, k: (k, 0))),
        vec,
        resident(D_MODEL, D_MODEL), resident(D_PLE, D_MODEL),
    ]
    scratch = [pltpu.VMEM((tm, D_MODEL), F32), pltpu.VMEM((tm, D_MODEL), BF16), pltpu.VMEM((tm, D_MODEL), F32)]
    return in_specs, row_tile(D_MODEL), scratch


def _post(post_args, tm):
    m = post_args[2].shape[0]
    in_specs, out_spec, scratch = _post_specs(tm, POST_FF_CHUNK, lambda f: f)
    return pl.pallas_call(
        _post_kernel,
        grid=(m // tm, D_FF // POST_FF_CHUNK),
        in_specs=in_specs,
        out_specs=out_spec,
        out_shape=jax.ShapeDtypeStruct((m, D_MODEL), F32),
        scratch_shapes=scratch,
        compiler_params=_cparams(("parallel", "arbitrary")),
        name="post",
    )(*post_args)


def _post_paged(post_args, q_s, kt_new, vi_new, kt_pool, vi_pool, page_table, lam_vecs, g_subln,
                dec_batch, dec_seq, lam_init):
    tm, tf = POST_ROW_TILE, POST_FF_CHUNK
    m = post_args[2].shape[0]
    n_row_tiles, n_ff = m // tm, D_FF // tf
    n_pages = page_table.shape[1]
    assert n_row_tiles == dec_batch and n_pages % n_ff == 0
    pps = n_pages // n_ff
    assert pps % PAGED_CHAINS == 0
    past_len = n_pages * PAGE_SIZE
    rph = 2 * dec_seq
    n_rows = N_HEADS * rph
    n_new = dec_batch * dec_seq
    q3 = q_s.astype(F32).reshape(dec_batch, dec_seq, D_ATTN)
    with_pt = lambda f: (lambda i, k, pt: f(i, k))
    post_in, post_out, post_scratch = _post_specs(tm, tf, with_pt)
    vec = lambda n: pl.BlockSpec((1, n), lambda i, k, pt: (0, 0))

    grid_spec = pltpu.PrefetchScalarGridSpec(
        num_scalar_prefetch=1,
        grid=(n_row_tiles, n_ff),
        in_specs=post_in + [vec(HEAD_DIM)] * 4 + [
            vec(V_DIM),
            pl.BlockSpec((None, dec_seq, D_ATTN), lambda i, k, pt: (i, 0, 0)),
            pl.BlockSpec((D_ATTN, n_new), lambda i, k, pt: (0, 0)),
            pl.BlockSpec((n_new * N_HEADS, V_DIM), lambda i, k, pt: (0, 0)),
            pl.BlockSpec(memory_space=pl.ANY),
            pl.BlockSpec(memory_space=pl.ANY),
        ],
        out_specs=[post_out, pl.BlockSpec((None, dec_seq, D_ATTN), lambda i, k, pt: (i, 0, 0))],
        scratch_shapes=post_scratch + [
            pltpu.VMEM((rph, D_ATTN), F32),
            pltpu.VMEM((n_rows, D_ATTN), BF16),
            pltpu.VMEM((PAGED_CHAINS, n_rows, 1), F32),
            pltpu.VMEM((PAGED_CHAINS, n_rows, 1), F32),
            pltpu.VMEM((PAGED_CHAINS, n_rows, V_DIM), F32),
            pltpu.VMEM((2, pps, D_ATTN, PAGE_SIZE), F32),
            pltpu.VMEM((2, pps, PAGE_SIZE * N_HEADS, V_DIM), F32),
            pltpu.SemaphoreType.DMA((2, 2)),
        ],
    )
    kern = functools.partial(_post_paged_kernel, pps=pps, past_len=past_len, dec_seq=dec_seq, lam_init=lam_init)
    out, ya_s = pl.pallas_call(
        kern,
        grid_spec=grid_spec,
        out_shape=[jax.ShapeDtypeStruct((m, D_MODEL), F32),
                   jax.ShapeDtypeStruct((dec_batch, dec_seq, D_ATTN), F32)],
        compiler_params=_cparams(("arbitrary", "arbitrary")),
        name="post_paged",
    )(page_table, *post_args, *lam_vecs, g_subln, q3, kt_new, vi_new, kt_pool, vi_pool)
    return out, ya_s.reshape(dec_batch * dec_seq, D_ATTN)


def _row(v):
    return v.reshape(1, -1)


def kernel(x_prompt, x_sample, cache_k, cache_v, state_conv, page_table, p_prompt, p_sample,
           w_in, w_out, lambda_q1, lambda_k1, lambda_q2, lambda_k2, g_subln, w_dw, b_dw,
           ln_conv_g, ln_conv_b, g_pre_mix, g_post_mix, g_pre_ffn, g_post_ffn,
           w_ff1, w_ff2, w_ple, w_ple_gate):
    depth = w_in.shape[0]
    batch, seq, _ = x_prompt.shape
    dec_batch, dec_seq, _ = x_sample.shape
    n_pool = cache_k.shape[1]
    mp, ms = batch * seq, dec_batch * dec_seq
    assert 2 * dec_seq == 8 and cache_k.shape[2] == PAGE_SIZE

    hp = x_prompt.reshape(mp, D_MODEL)
    hs = x_sample.reshape(ms, D_MODEL)
    outs = {k: [] for k in ("kp", "vp", "cp", "ks", "vs", "cs")}
    for l in range(depth):
        lam_init = 0.8 - 0.6 * math.exp(-0.3 * l)
        w_in16 = w_in[l].astype(BF16)
        w_ple16 = w_ple[l].astype(BF16)
        lam_vecs = [_row(lambda_q1[l]), _row(lambda_k1[l]), _row(lambda_q2[l]), _row(lambda_k2[l])]
        gsub = _row(g_subln[l])
        conv_args = (w_dw[l], _row(b_dw[l]), _row(ln_conv_g[l]), _row(ln_conv_b[l]))
        kt_pool = jnp.transpose(cache_k[l], (0, 2, 3, 4, 1)).reshape(n_pool, D_ATTN, PAGE_SIZE)
        vi_pool = cache_v[l].reshape(n_pool, PAGE_SIZE * N_HEADS, V_DIM)

        q_p, kt_p, vi_p, v16_p, u_p, w_out16, w_ff1_16, w_ff2_16, w_gate16 = _inproj(
            hp, _row(g_pre_mix[l]), w_in16, n_seq=batch, tm=PROMPT_ROW_TILE,
            cast_weights=(w_out[l], w_ff1[l], w_ff2[l], w_ple_gate[l]))
        post_w = (w_out16, _row(g_post_mix[l]), _row(g_pre_ffn[l]), w_ff1_16, w_ff2_16, _row(g_post_ffn[l]),
                  w_gate16, w_ple16)
        ya_p = _attn_prompt(q_p, kt_p, v16_p, lam_vecs, gsub, batch, seq, lam_init)
        yc_p = _conv_prompt(u_p, *conv_args, batch, seq)
        q_s, kt_s, vi_s, _, u_s = _inproj(hs, _row(g_pre_mix[l]), w_in16, n_seq=1, tm=ms)
        yc_s, st_tm = _conv_sample(jnp.transpose(state_conv[l], (1, 0, 2)), u_s, *conv_args,
                                   dec_batch, dec_seq)

        hp, ya_s = _post_paged((ya_p, yc_p, hp, p_prompt[l].reshape(mp, D_PLE)) + post_w,
                               q_s, kt_s[0], vi_s, kt_pool, vi_pool, page_table, lam_vecs, gsub,
                               dec_batch, dec_seq, lam_init)
        hs = _post((ya_s, yc_s, hs, p_sample[l].reshape(ms, D_PLE)) + post_w, tm=ms)

        outs["kp"].append(jnp.transpose(kt_p.reshape(batch, N_HEADS, 2, HEAD_DIM, seq), (0, 4, 1, 2, 3)))
        outs["vp"].append(vi_p.reshape(batch, seq, N_HEADS, V_DIM))
        outs["cp"].append(u_p.reshape(batch, seq, D_CONV)[:, seq - (CONV_WIDTH - 1):])
        outs["ks"].append(jnp.transpose(kt_s.reshape(N_HEADS, 2, HEAD_DIM, dec_batch, dec_seq),
                                        (3, 4, 0, 1, 2)))
        outs["vs"].append(vi_s.reshape(dec_batch, dec_seq, N_HEADS, V_DIM))
        outs["cs"].append(jnp.transpose(st_tm, (1, 0, 2)))

    return (hp.reshape(batch, seq, D_MODEL), hs.reshape(dec_batch, dec_seq, D_MODEL),
            jnp.stack(outs["kp"]), jnp.stack(outs["vp"]), jnp.stack(outs["cp"]),
            jnp.stack(outs["ks"]), jnp.stack(outs["vs"]), jnp.stack(outs["cs"]))
```

```python
import functools
import math

import jax
import jax.numpy as jnp
from jax import lax
from jax.experimental import pallas as pl
from jax.experimental.pallas import tpu as pltpu

D_MODEL = 1024
D_ATTN = 512
D_CONV = 512
N_HEADS = 4
HEAD_DIM = 64
V_DIM = 128
CONV_WIDTH = 31
D_FF = 4096
D_PLE = 256
PAGE_SIZE = 128
EPS = 1e-6
N_QKV = 3 * D_ATTN
N_IN = N_QKV + 2 * D_CONV
NEG_BIG = -1e30
LOG2E = math.log2(math.e)
Q_SCALE = HEAD_DIM ** -0.5 * LOG2E
ALIBI_SLOPES = tuple(2.0 ** (-8.0 * (h + 1) / N_HEADS) * LOG2E for h in range(N_HEADS))

F32 = jnp.float32
BF16 = jnp.bfloat16

VMEM_LIMIT_BYTES = 56 * 1024 * 1024
PROMPT_ROW_TILE = 512


def _cparams(semantics):
    return pltpu.CompilerParams(dimension_semantics=semantics, vmem_limit_bytes=VMEM_LIMIT_BYTES)


def _rms(x, g):
    return x * lax.rsqrt(jnp.mean(x * x, axis=-1, keepdims=True) + EPS) * g


def _diff_lambda(lq1, lk1, lq2, lk2, lam_init):
    return (jnp.exp(jnp.sum(lq1 * lk1, axis=-1, keepdims=True))
            - jnp.exp(jnp.sum(lq2 * lk2, axis=-1, keepdims=True)) + lam_init)


N_INPROJ_OUT = 5


def _inproj_kernel(x_ref, g_ref, w_ref, *refs, tm):
    n_cast = (len(refs) - N_INPROJ_OUT) // 2
    q_ref, kt_ref, vi_ref, v16_ref, u_ref = refs[n_cast:n_cast + N_INPROJ_OUT]
    for src_ref, dst_ref in zip(refs[:n_cast], refs[n_cast + N_INPROJ_OUT:]):
        dst_ref[...] = src_ref[...].astype(BF16)
    xn = _rms(x_ref[...], g_ref[...]).astype(BF16)
    q = jnp.dot(xn, w_ref[:, 0:D_ATTN], preferred_element_type=F32)
    q_ref[...] = (q * Q_SCALE).astype(BF16)
    kt_ref[...] = jnp.dot(xn, w_ref[:, D_ATTN:2 * D_ATTN], preferred_element_type=F32).T
    v = jnp.dot(xn, w_ref[:, 2 * D_ATTN:N_QKV], preferred_element_type=F32)
    v16_ref[...] = v.astype(BF16)
    for h in range(N_HEADS):
        vi_ref[pl.ds(h, tm, stride=N_HEADS), :] = v[:, h * V_DIM:(h + 1) * V_DIM]
    a = jnp.dot(xn, w_ref[:, N_QKV:N_QKV + D_CONV], preferred_element_type=F32)
    gt = jnp.dot(xn, w_ref[:, N_QKV + D_CONV:], preferred_element_type=F32)
    u_ref[...] = a * jax.nn.sigmoid(gt)


def _inproj(x2d, g_pre_mix, w_in16, n_seq, tm, cast_weights=()):
    m = x2d.shape[0]
    seq = m // n_seq
    nt = seq // tm
    n_steps = m // tm
    cast_specs = [pl.BlockSpec((w.shape[0] // n_steps, w.shape[1]), lambda i: (i, 0)) for w in cast_weights]
    return pl.pallas_call(
        functools.partial(_inproj_kernel, tm=tm),
        grid=(n_steps,),
        in_specs=[
            pl.BlockSpec((tm, D_MODEL), lambda i: (i, 0)),
            pl.BlockSpec((1, D_MODEL), lambda i: (0, 0)),
            pl.BlockSpec((D_MODEL, N_IN), lambda i: (0, 0)),
        ] + cast_specs,
        out_specs=[
            pl.BlockSpec((tm, D_ATTN), lambda i: (i, 0)),
            pl.BlockSpec((None, D_ATTN, tm), lambda i: (i // nt, 0, i % nt)),
            pl.BlockSpec((tm * N_HEADS, V_DIM), lambda i: (i, 0)),
            pl.BlockSpec((tm, D_ATTN), lambda i: (i, 0)),
            pl.BlockSpec((tm, D_CONV), lambda i: (i, 0)),
        ] + cast_specs,
        out_shape=[
            jax.ShapeDtypeStruct((m, D_ATTN), BF16),
            jax.ShapeDtypeStruct((n_seq, D_ATTN, seq), F32),
            jax.ShapeDtypeStruct((m * N_HEADS, V_DIM), F32),
            jax.ShapeDtypeStruct((m, D_ATTN), BF16),
            jax.ShapeDtypeStruct((m, D_CONV), F32),
        ] + [jax.ShapeDtypeStruct(w.shape, BF16) for w in cast_weights],
        compiler_params=_cparams(("parallel",)),
        name="inproj",
    )(x2d, g_pre_mix, w_in16, *cast_weights)


ATTN_TILE = 1024
ATTN_KEY_TILE = 256
ATTN_ROW_BLOCK = 128
LANES = 128


def _attn_prompt_kernel(lq1_ref, lk1_ref, lq2_ref, lk2_ref, gsub_ref, q_ref, kt_ref, v_ref, o_ref,
                        qs_scr, m_scr, l_scr, acc_scr, *, lam_init):
    t, w, rb = ATTN_TILE, ATTN_KEY_TILE, ATTN_ROW_BLOCK
    n_rb = t // rb
    qi = pl.program_id(1)
    lane = lax.broadcasted_iota(jnp.int32, (rb, 2 * HEAD_DIM), 1)
    kcol = lax.broadcasted_iota(jnp.int32, (1, w), 1)

    for h in range(N_HEADS):
        hs = slice(h * V_DIM, (h + 1) * V_DIM)
        for g in range(n_rb):
            q = q_ref[g * rb:(g + 1) * rb, hs]
            zero = jnp.zeros_like(q)
            qs_scr[h, g] = jnp.concatenate([jnp.where(lane < HEAD_DIM, q, zero),
                                            jnp.where(lane >= HEAD_DIM, q, zero)], axis=0)
    m_scr[...] = jnp.full(m_scr.shape, NEG_BIG, F32)
    l_scr[...] = jnp.zeros(l_scr.shape, F32)
    acc_scr[...] = jnp.zeros(acc_scr.shape, F32)

    def tile(k0, d):
        koff = (kcol + (k0 - qi * t)).astype(F32)
        for h in range(N_HEADS):
            hs = slice(h * V_DIM, (h + 1) * V_DIM)
            kt = kt_ref[hs, pl.ds(k0, w)].astype(BF16)
            vv = v_ref[pl.ds(k0, w), hs]
            bias = ALIBI_SLOPES[h] * koff
            for g in range(n_rb):
                masked = False
                if d is not None:
                    if (g + 1) * rb <= d * w:
                        continue
                    masked = g * rb < (d + 1) * w - 1
                s = jnp.dot(qs_scr[h, g], kt, preferred_element_type=F32) + bias
                if masked:
                    row = lax.broadcasted_iota(jnp.int32, (2 * rb, w), 0)
                    row = jnp.where(row >= rb, row - rb, row) + (g * rb - d * w)
                    col = lax.broadcasted_iota(jnp.int32, (2 * rb, w), 1)
                    s = jnp.where(col <= row, s, NEG_BIG)
                m_prev = m_scr[h, g]
                m_new = jnp.maximum(m_prev, jnp.max(s, axis=-1, keepdims=True))
                alpha = jnp.exp2(m_prev - m_new)
                p = jnp.exp2(s - jnp.concatenate([m_new] * (w // LANES), axis=-1))
                l_scr[h, g] = alpha * l_scr[h, g] + jnp.sum(p, axis=-1, keepdims=True)
                acc_scr[h, g] = alpha * acc_scr[h, g] + jnp.dot(p.astype(BF16), vv,
                                                               preferred_element_type=F32)
                m_scr[h, g] = m_new

    def below_diagonal(kv, carry):
        tile(pl.multiple_of(kv * w, w), None)
        return carry

    lax.fori_loop(0, qi * (t // w), below_diagonal, 0)
    for d in range(t // w):
        tile(pl.multiple_of(qi * t + d * w, w), d)

    lam = _diff_lambda(lq1_ref[...], lk1_ref[...], lq2_ref[...], lk2_ref[...], lam_init)
    for h in range(N_HEADS):
        for g in range(n_rb):
            o = acc_scr[h, g] / l_scr[h, g]
            oh = o[:rb] - lam * o[rb:]
            y = _rms(oh, gsub_ref[...]) * (1.0 - lam_init)
            o_ref[g * rb:(g + 1) * rb, h * V_DIM:(h + 1) * V_DIM] = y.astype(o_ref.dtype)


def _attn_prompt(q16, kt, v16, lam_vecs, g_subln, batch, seq, lam_init):
    t, rb = ATTN_TILE, ATTN_ROW_BLOCK
    nq = seq // t
    n_rb = t // rb
    kern = functools.partial(_attn_prompt_kernel, lam_init=lam_init)
    vec = lambda n: pl.BlockSpec((1, n), lambda b, qi: (0, 0))
    return pl.pallas_call(
        kern,
        grid=(batch, nq),
        in_specs=[vec(HEAD_DIM)] * 4 + [
            vec(V_DIM),
            pl.BlockSpec((t, D_ATTN), lambda b, qi: (b * nq + qi, 0)),
            pl.BlockSpec((None, D_ATTN, seq), lambda b, qi: (b, 0, 0)),
            pl.BlockSpec((seq, D_ATTN), lambda b, qi: (b, 0)),
        ],
        out_specs=pl.BlockSpec((t, D_ATTN), lambda b, qi: (b * nq + qi, 0)),
        out_shape=jax.ShapeDtypeStruct((batch * seq, D_ATTN), BF16),
        scratch_shapes=[
            pltpu.VMEM((N_HEADS, n_rb, 2 * rb, 2 * HEAD_DIM), BF16),
            pltpu.VMEM((N_HEADS, n_rb, 2 * rb, LANES), F32),
            pltpu.VMEM((N_HEADS, n_rb, 2 * rb, LANES), F32),
            pltpu.VMEM((N_HEADS, n_rb, 2 * rb, V_DIM), F32),
        ],
        compiler_params=_cparams(("parallel", "arbitrary")),
        name="attn_prompt",
    )(*lam_vecs, g_subln, q16, kt, v16)


PAGED_CHAINS = 1


def _paged_rows(dec_seq):
    rph = 2 * dec_seq
    n_rows = N_HEADS * rph
    row = lax.broadcasted_iota(jnp.int32, (n_rows, 1), 0)
    head = row // rph
    comp = (row % rph) // dec_seq
    tok = row % dec_seq
    slope = jnp.zeros((n_rows, 1), F32)
    for h in range(N_HEADS):
        slope = jnp.where(head == h, ALIBI_SLOPES[h], slope)
    return rph, n_rows, head, comp, tok, slope


def _paged_init(q_ref, q2_scr, qbd_scr, m_scr, l_scr, acc_scr, *, dec_seq):
    rph, n_rows, head, comp, _, _ = _paged_rows(dec_seq)
    q = q_ref[...]
    q2_scr[0:dec_seq, :] = q
    q2_scr[dec_seq:rph, :] = q
    qrep = jnp.concatenate([q2_scr[...]] * N_HEADS, axis=0)
    col = lax.broadcasted_iota(jnp.int32, (n_rows, D_ATTN), 1)
    qbd_scr[...] = jnp.where(col // HEAD_DIM == head * 2 + comp, qrep, 0.0).astype(BF16)
    m_scr[...] = jnp.full(m_scr.shape, NEG_BIG, F32)
    l_scr[...] = jnp.zeros(l_scr.shape, F32)
    acc_scr[...] = jnp.zeros(acc_scr.shape, F32)


def _paged_update(c, s, v_refs, n_slots, m_scr, l_scr, acc_scr, rph):
    m_prev = m_scr[c]
    m_new = jnp.maximum(m_prev, jnp.max(s, axis=-1, keepdims=True))
    alpha = jnp.exp2(m_prev - m_new)
    p = jnp.exp2(s - m_new)
    l_scr[c] = alpha * l_scr[c] + jnp.sum(p, axis=-1, keepdims=True)
    p16 = p.astype(BF16)
    for h in range(N_HEADS):
        rs = slice(h * rph, (h + 1) * rph)
        vh = jnp.concatenate([r[pl.ds(h, n_slots, stride=N_HEADS), :] for r in v_refs], axis=0)
        pv = jnp.dot(p16[rs, :], vh.astype(BF16), preferred_element_type=F32)
        acc_scr[c, rs, :] = alpha[rs] * acc_scr[c, rs, :] + pv
    m_scr[c] = m_new


def _paged_pages(j, ktp_refs, vip_refs, qbd_scr, m_scr, l_scr, acc_scr, *, past_len, dec_seq):
    rph, _, _, _, _, slope = _paged_rows(dec_seq)
    pps = len(ktp_refs)
    ppc = pps // PAGED_CHAINS
    qbd = qbd_scr[...]
    kcol = lax.broadcasted_iota(jnp.int32, (1, ppc * PAGE_SIZE), 1)
    for c in range(PAGED_CHAINS):
        pages = range(c * ppc, (c + 1) * ppc)
        s = jnp.concatenate(
            [jnp.dot(qbd, ktp_refs[i][...].astype(BF16), preferred_element_type=F32) for i in pages], axis=-1)
        koff = (kcol + ((j * pps + c * ppc) * PAGE_SIZE - past_len)).astype(F32)
        _paged_update(c, s + slope * koff, [vip_refs[i] for i in pages], PAGE_SIZE, m_scr, l_scr, acc_scr, rph)


def _paged_finish(b, lam_refs, gsub_ref, ktn_ref, vin_ref, o_ref, qbd_scr, m_scr, l_scr, acc_scr, *,
                  dec_seq, lam_init):
    rph, n_rows, _, _, tok, slope = _paged_rows(dec_seq)
    n_new = ktn_ref.shape[1]
    sn = jnp.dot(qbd_scr[...], ktn_ref[...].astype(BF16), preferred_element_type=F32)
    col = lax.broadcasted_iota(jnp.int32, (n_rows, n_new), 1)
    ctok = col % dec_seq
    valid = jnp.logical_and(col // dec_seq == b, ctok <= tok)
    sn = jnp.where(valid, sn + slope * ctok.astype(F32), NEG_BIG)
    _paged_update(0, sn, [vin_ref], n_new, m_scr, l_scr, acc_scr, rph)

    m = m_scr[0]
    for c in range(1, PAGED_CHAINS):
        m = jnp.maximum(m, m_scr[c])
    l = jnp.zeros((n_rows, 1), F32)
    acc = jnp.zeros((n_rows, V_DIM), F32)
    for c in range(PAGED_CHAINS):
        w = jnp.exp2(m_scr[c] - m)
        l = l + w * l_scr[c]
        acc = acc + w * acc_scr[c]

    lam = _diff_lambda(*[r[...] for r in lam_refs], lam_init)
    o = acc / l
    for h in range(N_HEADS):
        oh = o[h * rph:h * rph + dec_seq] - lam * o[h * rph + dec_seq:(h + 1) * rph]
        o_ref[:, h * V_DIM:(h + 1) * V_DIM] = _rms(oh, gsub_ref[...]) * (1.0 - lam_init)


SUBLANES = 8
CONV_HALO = 32
CONV_ROW_CHUNK = 16


def _ln_swish(c, g, b):
    mu = jnp.mean(c, axis=-1, keepdims=True)
    d = c - mu
    var = jnp.mean(d * d, axis=-1, keepdims=True)
    y = d * lax.rsqrt(var + EPS) * g + b
    return y * jax.nn.sigmoid(y)


def _conv_prompt_kernel(prev_ref, cur_ref, w_ref, b_ref, g_ref, beta_ref, o_ref, ext_scr, sh_scr, *, tm):
    i = pl.program_id(1)
    n_ext = CONV_HALO + tm
    ext_scr[0:CONV_HALO, :] = jnp.where(i == 0, 0.0, prev_ref[...])
    ext_scr[CONV_HALO:n_ext, :] = cur_ref[...]
    ext = ext_scr[...]
    for r in range(1, SUBLANES):
        sh_scr[r] = pltpu.roll(ext, n_ext - r, axis=0)
    first = CONV_HALO - (CONV_WIDTH - 1)
    for c in range(0, tm, CONV_ROW_CHUNK):
        acc = jnp.broadcast_to(b_ref[...], (CONV_ROW_CHUNK, D_CONV))
        for j in range(CONV_WIDTH):
            r = (first + j) % SUBLANES
            a = c + first + j - r
            src = ext_scr[a:a + CONV_ROW_CHUNK, :] if r == 0 else sh_scr[r, a:a + CONV_ROW_CHUNK, :]
            acc = acc + jnp.concatenate([w_ref[j]] * (CONV_ROW_CHUNK // SUBLANES), axis=0) * src
        o_ref[c:c + CONV_ROW_CHUNK, :] = _ln_swish(acc, g_ref[...], beta_ref[...]).astype(o_ref.dtype)


def _conv_prompt(u2d, w_dw, b_dw, ln_g, ln_b, batch, seq, tm=PROMPT_ROW_TILE):
    nt = seq // tm
    halo_per_tile = tm // CONV_HALO
    kern = functools.partial(_conv_prompt_kernel, tm=tm)
    vec = pl.BlockSpec((1, D_CONV), lambda b, i: (0, 0))
    w_rep = jnp.broadcast_to(w_dw[:, None, :], (CONV_WIDTH, SUBLANES, D_CONV))
    return pl.pallas_call(
        kern,
        grid=(batch, nt),
        in_specs=[
            pl.BlockSpec((CONV_HALO, D_CONV),
                         lambda b, i: (jnp.maximum((b * nt + i) * halo_per_tile - 1, 0), 0)),
            pl.BlockSpec((tm, D_CONV), lambda b, i: (b * nt + i, 0)),
            pl.BlockSpec((CONV_WIDTH, SUBLANES, D_CONV), lambda b, i: (0, 0, 0)),
            vec, vec, vec,
        ],
        out_specs=pl.BlockSpec((tm, D_CONV), lambda b, i: (b * nt + i, 0)),
        out_shape=jax.ShapeDtypeStruct((batch * seq, D_CONV), BF16),
        scratch_shapes=[pltpu.VMEM((CONV_HALO + tm, D_CONV), F32),
                        pltpu.VMEM((SUBLANES, CONV_HALO + tm, D_CONV), F32)],
        compiler_params=_cparams(("parallel", "parallel")),
        name="conv_prompt",
    )(u2d, u2d, w_rep, b_dw, ln_g, ln_b)


def _conv_sample_kernel(state_ref, u_ref, w_ref, b_ref, g_ref, beta_ref, o_ref, st_ref, *, dec_batch, dec_seq):
    n_state = CONV_WIDTH - 1

    def ext(i):
        return state_ref[i] if i < n_state else u_ref[i - n_state]

    for t in range(dec_seq):
        acc = jnp.broadcast_to(b_ref[...], (dec_batch, D_CONV))
        for j in range(CONV_WIDTH):
            acc = acc + w_ref[j:j + 1, :] * ext(t + j)
        o_ref[t] = _ln_swish(acc, g_ref[...], beta_ref[...])
    for i in range(n_state):
        st_ref[i] = ext(i + dec_seq)


def _conv_sample(state_tm, u_s, w_dw, b_dw, ln_g, ln_b, dec_batch, dec_seq):
    kern = functools.partial(_conv_sample_kernel, dec_batch=dec_batch, dec_seq=dec_seq)
    u_tm = jnp.transpose(u_s.reshape(dec_batch, dec_seq, D_CONV), (1, 0, 2))
    y_tm, st_tm = pl.pallas_call(
        kern,
        out_shape=[
            jax.ShapeDtypeStruct((dec_seq, dec_batch, D_CONV), F32),
            jax.ShapeDtypeStruct(state_tm.shape, F32),
        ],
        compiler_params=pltpu.CompilerParams(vmem_limit_bytes=VMEM_LIMIT_BYTES),
        name="conv_sample",
    )(state_tm, u_tm, w_dw, b_dw, ln_g, ln_b)
    return jnp.transpose(y_tm, (1, 0, 2)).reshape(dec_batch * dec_seq, D_CONV), st_tm


def _post_first(ya_ref, yc_ref, x_ref, wo_ref, gpm_ref, gpf_ref, h_scr, hn_scr, acc_scr):
    y_mix = jnp.concatenate([ya_ref[...].astype(BF16), yc_ref[...].astype(BF16)], axis=-1)
    m = jnp.dot(y_mix, wo_ref[...], preferred_element_type=F32)
    h = x_ref[...] + _rms(m, gpm_ref[...])
    h_scr[...] = h
    hn_scr[...] = _rms(h, gpf_ref[...]).astype(BF16)
    acc_scr[...] = jnp.zeros(acc_scr.shape, F32)


def _post_ffn_chunk(w1_ref, w2_ref, hn_scr, acc_scr):
    a = jnp.dot(hn_scr[...], w1_ref[...], preferred_element_type=F32)
    a = jnp.square(jnp.maximum(a, 0.0)).astype(BF16)
    acc_scr[...] += jnp.dot(a, w2_ref[...], preferred_element_type=F32)


def _post_last(pe_ref, gpost_ref, wg_ref, wp_ref, o_ref, h_scr, acc_scr):
    h2 = h_scr[...] + _rms(acc_scr[...], gpost_ref[...])
    gate = jax.nn.sigmoid(jnp.dot(h2.astype(BF16), wg_ref[...], preferred_element_type=F32))
    pev = jnp.dot(pe_ref[...].astype(BF16), wp_ref[...], preferred_element_type=F32)
    o_ref[...] = h2 + gate * pev


def _post_kernel(ya_ref, yc_ref, x_ref, pe_ref, wo_ref, gpm_ref, gpf_ref, w1_ref, w2_ref, gpost_ref,
                 wg_ref, wp_ref, o_ref, h_scr, hn_scr, acc_scr):
    k = pl.program_id(1)

    @pl.when(k == 0)
    def _():
        _post_first(ya_ref, yc_ref, x_ref, wo_ref, gpm_ref, gpf_ref, h_scr, hn_scr, acc_scr)

    _post_ffn_chunk(w1_ref, w2_ref, hn_scr, acc_scr)

    @pl.when(k == pl.num_programs(1) - 1)
    def _():
        _post_last(pe_ref, gpost_ref, wg_ref, wp_ref, o_ref, h_scr, acc_scr)


N_POST_IN = 12
N_PAGED_IN = 10


def _page_copies(pt_ref, seq, chunk, slot, kt_pool_ref, vi_pool_ref, kbuf, vbuf, sem, pps):
    copies = []
    for p in range(pps):
        page = pt_ref[seq, chunk * pps + p]
        copies.append(pltpu.make_async_copy(kt_pool_ref.at[page], kbuf.at[slot, p], sem.at[slot, 0]))
        copies.append(pltpu.make_async_copy(vi_pool_ref.at[page], vbuf.at[slot, p], sem.at[slot, 1]))
    return copies


def _post_paged_kernel(pt_ref, *refs, pps, past_len, dec_seq, lam_init):
    (ya_ref, yc_ref, x_ref, pe_ref, wo_ref, gpm_ref, gpf_ref, w1_ref, w2_ref, gpost_ref, wg_ref,
     wp_ref) = refs[:N_POST_IN]
    lam_refs = refs[N_POST_IN:N_POST_IN + 4]
    gsub_ref, q_ref, ktn_ref, vin_ref, kt_pool_ref, vi_pool_ref = refs[N_POST_IN + 4:N_POST_IN + N_PAGED_IN]
    o_ref, oa_ref = refs[N_POST_IN + N_PAGED_IN:N_POST_IN + N_PAGED_IN + 2]
    (h_scr, hn_scr, acc_scr, q2_scr, qbd_scr, m_scr, l_scr, pacc_scr, kbuf, vbuf,
     sem) = refs[N_POST_IN + N_PAGED_IN + 2:]
    i = pl.program_id(0)
    k = pl.program_id(1)
    n_k = pl.num_programs(1)
    step = i * n_k + k
    slot = step % 2
    copies = functools.partial(_page_copies, pt_ref, kt_pool_ref=kt_pool_ref, vi_pool_ref=vi_pool_ref,
                               kbuf=kbuf, vbuf=vbuf, sem=sem, pps=pps)

    @pl.when(step == 0)
    def _():
        for c in copies(0, 0, 0):
            c.start()

    @pl.when(step + 1 < pl.num_programs(0) * n_k)
    def _():
        nxt = step + 1
        for c in copies(nxt // n_k, nxt % n_k, 1 - slot):
            c.start()

    @pl.when(k == 0)
    def _():
        _post_first(ya_ref, yc_ref, x_ref, wo_ref, gpm_ref, gpf_ref, h_scr, hn_scr, acc_scr)
        _paged_init(q_ref, q2_scr, qbd_scr, m_scr, l_scr, pacc_scr, dec_seq=dec_seq)

    for c in copies(i, k, slot):
        c.wait()
    _post_ffn_chunk(w1_ref, w2_ref, hn_scr, acc_scr)
    _paged_pages(k, [kbuf.at[slot, p] for p in range(pps)], [vbuf.at[slot, p] for p in range(pps)],
                 qbd_scr, m_scr, l_scr, pacc_scr, past_len=past_len, dec_seq=dec_seq)

    @pl.when(k == n_k - 1)
    def _():
        _post_last(pe_ref, gpost_ref, wg_ref, wp_ref, o_ref, h_scr, acc_scr)
        _paged_finish(i, lam_refs, gsub_ref, ktn_ref, vin_ref, oa_ref, qbd_scr, m_scr, l_scr, pacc_scr,
                      dec_seq=dec_seq, lam_init=lam_init)


POST_ROW_TILE = PROMPT_ROW_TILE
POST_FF_CHUNK = 1024


def _post_specs(tm, tf, imap):
    row_tile = lambda n: pl.BlockSpec((tm, n), imap(lambda i, k: (i, 0)))
    vec = pl.BlockSpec((1, D_MODEL), imap(lambda i, k: (0, 0)))
    resident = lambda r, c: pl.BlockSpec((r, c), imap(lambda i, k: (0, 0)), pipeline_mode=pl.Buffered(1))
    in_specs = [
        row_tile(D_ATTN), row_tile(D_CONV), row_tile(D_MODEL), row_tile(D_PLE),
        resident(D_MODEL, D_MODEL), vec, vec,
        pl.BlockSpec((D_MODEL, tf), imap(lambda i, k: (0, k))),
        pl.BlockSpec((tf, D_MODEL), imap(lambda i, k: (k, 0))),
        vec,
        resident(D_MODEL, D_MODEL), resident(D_PLE, D_MODEL),
    ]
    scratch = [pltpu.VMEM((tm, D_MODEL), F32), pltpu.VMEM((tm, D_MODEL), BF16), pltpu.VMEM((tm, D_MODEL), F32)]
    return in_specs, row_tile(D_MODEL), scratch


def _post(post_args, tm):
    m = post_args[2].shape[0]
    in_specs, out_spec, scratch = _post_specs(tm, POST_FF_CHUNK, lambda f: f)
    return pl.pallas_call(
        _post_kernel,
        grid=(m // tm, D_FF // POST_FF_CHUNK),
        in_specs=in_specs,
        out_specs=out_spec,
        out_shape=jax.ShapeDtypeStruct((m, D_MODEL), F32),
        scratch_shapes=scratch,
        compiler_params=_cparams(("parallel", "arbitrary")),
        name="post",
    )(*post_args)


def _post_paged(post_args, q_s, kt_new, vi_new, kt_pool, vi_pool, page_table, lam_vecs, g_subln,
                dec_batch, dec_seq, lam_init):
    tm, tf = POST_ROW_TILE, POST_FF_CHUNK
    m = post_args[2].shape[0]
    n_row_tiles, n_ff = m // tm, D_FF // tf
    n_pages = page_table.shape[1]
    assert n_row_tiles == dec_batch and n_pages % n_ff == 0
    pps = n_pages // n_ff
    assert pps % PAGED_CHAINS == 0
    past_len = n_pages * PAGE_SIZE
    rph = 2 * dec_seq
    n_rows = N_HEADS * rph
    n_new = dec_batch * dec_seq
    q3 = q_s.astype(F32).reshape(dec_batch, dec_seq, D_ATTN)
    with_pt = lambda f: (lambda i, k, pt: f(i, k))
    post_in, post_out, post_scratch = _post_specs(tm, tf, with_pt)
    vec = lambda n: pl.BlockSpec((1, n), lambda i, k, pt: (0, 0))

    grid_spec = pltpu.PrefetchScalarGridSpec(
        num_scalar_prefetch=1,
        grid=(n_row_tiles, n_ff),
        in_specs=post_in + [vec(HEAD_DIM)] * 4 + [
            vec(V_DIM),
            pl.BlockSpec((None, dec_seq, D_ATTN), lambda i, k, pt: (i, 0, 0)),
            pl.BlockSpec((D_ATTN, n_new), lambda i, k, pt: (0, 0)),
            pl.BlockSpec((n_new * N_HEADS, V_DIM), lambda i, k, pt: (0, 0)),
            pl.BlockSpec(memory_space=pl.ANY),
            pl.BlockSpec(memory_space=pl.ANY),
        ],
        out_specs=[post_out, pl.BlockSpec((None, dec_seq, D_ATTN), lambda i, k, pt: (i, 0, 0))],
        scratch_shapes=post_scratch + [
            pltpu.VMEM((rph, D_ATTN), F32),
            pltpu.VMEM((n_rows, D_ATTN), BF16),
            pltpu.VMEM((PAGED_CHAINS, n_rows, 1), F32),
            pltpu.VMEM((PAGED_CHAINS, n_rows, 1), F32),
            pltpu.VMEM((PAGED_CHAINS, n_rows, V_DIM), F32),
            pltpu.VMEM((2, pps, D_ATTN, PAGE_SIZE), F32),
            pltpu.VMEM((2, pps, PAGE_SIZE * N_HEADS, V_DIM), F32),
            pltpu.SemaphoreType.DMA((2, 2)),
        ],
    )
    kern = functools.partial(_post_paged_kernel, pps=pps, past_len=past_len, dec_seq=dec_seq, lam_init=lam_init)
    out, ya_s = pl.pallas_call(
        kern,
        grid_spec=grid_spec,
        out_shape=[jax.ShapeDtypeStruct((m, D_MODEL), F32),
                   jax.ShapeDtypeStruct((dec_batch, dec_seq, D_ATTN), F32)],
        compiler_params=_cparams(("arbitrary", "arbitrary")),
        name="post_paged",
    )(page_table, *post_args, *lam_vecs, g_subln, q3, kt_new, vi_new, kt_pool, vi_pool)
    return out, ya_s.reshape(dec_batch * dec_seq, D_ATTN)


def _row(v):
    return v.reshape(1, -1)


def kernel(x_prompt, x_sample, cache_k, cache_v, state_conv, page_table, p_prompt, p_sample,
           w_in, w_out, lambda_q1, lambda_k1, lambda_q2, lambda_k2, g_subln, w_dw, b_dw,
           ln_conv_g, ln_conv_b, g_pre_mix, g_post_mix, g_pre_ffn, g_post_ffn,
           w_ff1, w_ff2, w_ple, w_ple_gate):
    depth = w_in.shape[0]
    batch, seq, _ = x_prompt.shape
    dec_batch, dec_seq, _ = x_sample.shape
    n_pool = cache_k.shape[1]
    mp, ms = batch * seq, dec_batch * dec_seq
    assert 2 * dec_seq == 8 and cache_k.shape[2] == PAGE_SIZE

    hp = x_prompt.reshape(mp, D_MODEL)
    hs = x_sample.reshape(ms, D_MODEL)
    outs = {k: [] for k in ("kp", "vp", "cp", "ks", "vs", "cs")}
    for l in range(depth):
        lam_init = 0.8 - 0.6 * math.exp(-0.3 * l)
        w_in16 = w_in[l].astype(BF16)
        w_ple16 = w_ple[l].astype(BF16)
        lam_vecs = [_row(lambda_q1[l]), _row(lambda_k1[l]), _row(lambda_q2[l]), _row(lambda_k2[l])]
        gsub = _row(g_subln[l])
        conv_args = (w_dw[l], _row(b_dw[l]), _row(ln_conv_g[l]), _row(ln_conv_b[l]))
        kt_pool = jnp.transpose(cache_k[l], (0, 2, 3, 4, 1)).reshape(n_pool, D_ATTN, PAGE_SIZE)
        vi_pool = cache_v[l].reshape(n_pool, PAGE_SIZE * N_HEADS, V_DIM)

        q_p, kt_p, vi_p, v16_p, u_p, w_out16, w_ff1_16, w_ff2_16, w_gate16 = _inproj(
            hp, _row(g_pre_mix[l]), w_in16, n_seq=batch, tm=PROMPT_ROW_TILE,
            cast_weights=(w_out[l], w_ff1[l], w_ff2[l], w_ple_gate[l]))
        post_w = (w_out16, _row(g_post_mix[l]), _row(g_pre_ffn[l]), w_ff1_16, w_ff2_16, _row(g_post_ffn[l]),
                  w_gate16, w_ple16)
        ya_p = _attn_prompt(q_p, kt_p, v16_p, lam_vecs, gsub, batch, seq, lam_init)
        yc_p = _conv_prompt(u_p, *conv_args, batch, seq)
        q_s, kt_s, vi_s, _, u_s = _inproj(hs, _row(g_pre_mix[l]), w_in16, n_seq=1, tm=ms)
        yc_s, st_tm = _conv_sample(jnp.transpose(state_conv[l], (1, 0, 2)), u_s, *conv_args,
                                   dec_batch, dec_seq)

        hp, ya_s = _post_paged((ya_p, yc_p, hp, p_prompt[l].reshape(mp, D_PLE)) + post_w,
                               q_s, kt_s[0], vi_s, kt_pool, vi_pool, page_table, lam_vecs, gsub,
                               dec_batch, dec_seq, lam_init)
        hs = _post((ya_s, yc_s, hs, p_sample[l].reshape(ms, D_PLE)) + post_w, tm=ms)

        outs["kp"].append(jnp.transpose(kt_p.reshape(batch, N_HEADS, 2, HEAD_DIM, seq), (0, 4, 1, 2, 3)))
        outs["vp"].append(vi_p.reshape(batch, seq, N_HEADS, V_DIM))
        outs["cp"].append(u_p.reshape(batch, seq, D_CONV)[:, seq - (CONV_WIDTH - 1):])
        outs["ks"].append(jnp.transpose(kt_s.reshape(N_HEADS, 2, HEAD_DIM, dec_batch, dec_seq),
                                        (3, 4, 0, 1, 2)))
        outs["vs"].append(vi_s.reshape(dec_batch, dec_seq, N_HEADS, V_DIM))
        outs["cs"].append(jnp.transpose(st_tm, (1, 0, 2)))

    return (hp.reshape(batch, seq, D_MODEL), hs.reshape(dec_batch, dec_seq, D_MODEL),
            jnp.stack(outs["kp"]), jnp.stack(outs["vp"]), jnp.stack(outs["cp"]),
            jnp.stack(outs["ks"]), jnp.stack(outs["vs"]), jnp.stack(outs["cs"]))
```

```python
import functools
import math

import jax
import jax.numpy as jnp
from jax import lax
from jax.experimental import pallas as pl
from jax.experimental.pallas import tpu as pltpu

D_MODEL = 1024
D_ATTN = 512
D_CONV = 512
N_HEADS = 4
HEAD_DIM = 64
V_DIM = 128
CONV_WIDTH = 31
D_FF = 4096
D_PLE = 256
PAGE_SIZE = 128
EPS = 1e-6
N_QKV = 3 * D_ATTN
N_IN = N_QKV + 2 * D_CONV
NEG_BIG = -1e30
LOG2E = math.log2(math.e)
Q_SCALE = HEAD_DIM ** -0.5 * LOG2E
ALIBI_SLOPES = tuple(2.0 ** (-8.0 * (h + 1) / N_HEADS) * LOG2E for h in range(N_HEADS))

F32 = jnp.float32
BF16 = jnp.bfloat16

VMEM_LIMIT_BYTES = 56 * 1024 * 1024
PROMPT_ROW_TILE = 512


def _cparams(semantics):
    return pltpu.CompilerParams(dimension_semantics=semantics, vmem_limit_bytes=VMEM_LIMIT_BYTES)


def _rms(x, g):
    return x * lax.rsqrt(jnp.mean(x * x, axis=-1, keepdims=True) + EPS) * g


def _diff_lambda(lq1, lk1, lq2, lk2, lam_init):
    return (jnp.exp(jnp.sum(lq1 * lk1, axis=-1, keepdims=True))
            - jnp.exp(jnp.sum(lq2 * lk2, axis=-1, keepdims=True)) + lam_init)


N_INPROJ_OUT = 5


def _inproj_kernel(x_ref, g_ref, w_ref, *refs, tm):
    n_cast = (len(refs) - N_INPROJ_OUT) // 2
    q_ref, kt_ref, vi_ref, v16_ref, u_ref = refs[n_cast:n_cast + N_INPROJ_OUT]
    for src_ref, dst_ref in zip(refs[:n_cast], refs[n_cast + N_INPROJ_OUT:]):
        dst_ref[...] = src_ref[...].astype(BF16)
    xn = _rms(x_ref[...], g_ref[...]).astype(BF16)
    q = jnp.dot(xn, w_ref[:, 0:D_ATTN], preferred_element_type=F32)
    q_ref[...] = (q * Q_SCALE).astype(BF16)
    kt_ref[...] = jnp.dot(xn, w_ref[:, D_ATTN:2 * D_ATTN], preferred_element_type=F32).T
    v = jnp.dot(xn, w_ref[:, 2 * D_ATTN:N_QKV], preferred_element_type=F32)
    v16_ref[...] = v.astype(BF16)
    for h in range(N_HEADS):
        vi_ref[pl.ds(h, tm, stride=N_HEADS), :] = v[:, h * V_DIM:(h + 1) * V_DIM]
    a = jnp.dot(xn, w_ref[:, N_QKV:N_QKV + D_CONV], preferred_element_type=F32)
    gt = jnp.dot(xn, w_ref[:, N_QKV + D_CONV:], preferred_element_type=F32)
    u_ref[...] = a * jax.nn.sigmoid(gt)


def _inproj(x2d, g_pre_mix, w_in16, n_seq, tm, cast_weights=()):
    m = x2d.shape[0]
    seq = m // n_seq
    nt = seq // tm
    n_steps = m // tm
    cast_specs = [pl.BlockSpec((w.shape[0] // n_steps, w.shape[1]), lambda i: (i, 0)) for w in cast_weights]
    return pl.pallas_call(
        functools.partial(_inproj_kernel, tm=tm),
        grid=(n_steps,),
        in_specs=[
            pl.BlockSpec((tm, D_MODEL), lambda i: (i, 0)),
            pl.BlockSpec((1, D_MODEL), lambda i: (0, 0)),
            pl.BlockSpec((D_MODEL, N_IN), lambda i: (0, 0)),
        ] + cast_specs,
        out_specs=[
            pl.BlockSpec((tm, D_ATTN), lambda i: (i, 0)),
            pl.BlockSpec((None, D_ATTN, tm), lambda i: (i // nt, 0, i % nt)),
            pl.BlockSpec((tm * N_HEADS, V_DIM), lambda i: (i, 0)),
            pl.BlockSpec((tm, D_ATTN), lambda i: (i, 0)),
            pl.BlockSpec((tm, D_CONV), lambda i: (i, 0)),
        ] + cast_specs,
        out_shape=[
            jax.ShapeDtypeStruct((m, D_ATTN), BF16),
            jax.ShapeDtypeStruct((n_seq, D_ATTN, seq), F32),
            jax.ShapeDtypeStruct((m * N_HEADS, V_DIM), F32),
            jax.ShapeDtypeStruct((m, D_ATTN), BF16),
            jax.ShapeDtypeStruct((m, D_CONV), F32),
        ] + [jax.ShapeDtypeStruct(w.shape, BF16) for w in cast_weights],
        compiler_params=_cparams(("parallel",)),
        name="inproj",
    )(x2d, g_pre_mix, w_in16, *cast_weights)


ATTN_TILE = 1024
ATTN_KEY_TILE = 256
ATTN_ROW_BLOCK = 128
LANES = 128


def _attn_prompt_kernel(lq1_ref, lk1_ref, lq2_ref, lk2_ref, gsub_ref, q_ref, kt_ref, v_ref, o_ref,
                        qs_scr, m_scr, l_scr, acc_scr, *, lam_init):
    t, w, rb = ATTN_TILE, ATTN_KEY_TILE, ATTN_ROW_BLOCK
    n_rb = t // rb
    qi = pl.program_id(1)
    lane = lax.broadcasted_iota(jnp.int32, (rb, 2 * HEAD_DIM), 1)
    kcol = lax.broadcasted_iota(jnp.int32, (1, w), 1)

    for h in range(N_HEADS):
        hs = slice(h * V_DIM, (h + 1) * V_DIM)
        for g in range(n_rb):
            q = q_ref[g * rb:(g + 1) * rb, hs]
            zero = jnp.zeros_like(q)
            qs_scr[h, g] = jnp.concatenate([jnp.where(lane < HEAD_DIM, q, zero),
                                            jnp.where(lane >= HEAD_DIM, q, zero)], axis=0)
    m_scr[...] = jnp.full(m_scr.shape, NEG_BIG, F32)
    l_scr[...] = jnp.zeros(l_scr.shape, F32)
    acc_scr[...] = jnp.zeros(acc_scr.shape, F32)

    def tile(k0, d):
        koff = (kcol + (k0 - qi * t)).astype(F32)
        for h in range(N_HEADS):
            hs = slice(h * V_DIM, (h + 1) * V_DIM)
            kt = kt_ref[hs, pl.ds(k0, w)].astype(BF16)
            vv = v_ref[pl.ds(k0, w), hs]
            bias = ALIBI_SLOPES[h] * koff
            for g in range(n_rb):
                masked = False
                if d is not None:
                    if (g + 1) * rb <= d * w:
                        continue
                    masked = g * rb < (d + 1) * w - 1
                s = jnp.dot(qs_scr[h, g], kt, preferred_element_type=F32) + bias
                if masked:
                    row = lax.broadcasted_iota(jnp.int32, (2 * rb, w), 0)
                    row = jnp.where(row >= rb, row - rb, row) + (g * rb - d * w)
                    col = lax.broadcasted_iota(jnp.int32, (2 * rb, w), 1)
                    s = jnp.where(col <= row, s, NEG_BIG)
                m_prev = m_scr[h, g]
                m_new = jnp.maximum(m_prev, jnp.max(s, axis=-1, keepdims=True))
                alpha = jnp.exp2(m_prev - m_new)
                p = jnp.exp2(s - jnp.concatenate([m_new] * (w // LANES), axis=-1))
                l_scr[h, g] = alpha * l_scr[h, g] + jnp.sum(p, axis=-1, keepdims=True)
                acc_scr[h, g] = alpha * acc_scr[h, g] + jnp.dot(p.astype(BF16), vv,
                                                               preferred_element_type=F32)
                m_scr[h, g] = m_new

    def below_diagonal(kv, carry):
        tile(pl.multiple_of(kv * w, w), None)
        return carry

    lax.fori_loop(0, qi * (t // w), below_diagonal, 0)
    for d in range(t // w):
        tile(pl.multiple_of(qi * t + d * w, w), d)

    lam = _diff_lambda(lq1_ref[...], lk1_ref[...], lq2_ref[...], lk2_ref[...], lam_init)
    for h in range(N_HEADS):
        for g in range(n_rb):
            o = acc_scr[h, g] / l_scr[h, g]
            oh = o[:rb] - lam * o[rb:]
            y = _rms(oh, gsub_ref[...]) * (1.0 - lam_init)
            o_ref[g * rb:(g + 1) * rb, h * V_DIM:(h + 1) * V_DIM] = y.astype(o_ref.dtype)


def _attn_prompt(q16, kt, v16, lam_vecs, g_subln, batch, seq, lam_init):
    t, rb = ATTN_TILE, ATTN_ROW_BLOCK
    nq = seq // t
    n_rb = t // rb
    kern = functools.partial(_attn_prompt_kernel, lam_init=lam_init)
    vec = lambda n: pl.BlockSpec((1, n), lambda b, qi: (0, 0))
    return pl.pallas_call(
        kern,
        grid=(batch, nq),
        in_specs=[vec(HEAD_DIM)] * 4 + [
            vec(V_DIM),
            pl.BlockSpec((t, D_ATTN), lambda b, qi: (b * nq + qi, 0)),
            pl.BlockSpec((None, D_ATTN, seq), lambda b, qi: (b, 0, 0)),
            pl.BlockSpec((seq, D_ATTN), lambda b, qi: (b, 0)),
        ],
        out_specs=pl.BlockSpec((t, D_ATTN), lambda b, qi: (b * nq + qi, 0)),
        out_shape=jax.ShapeDtypeStruct((batch * seq, D_ATTN), BF16),
        scratch_shapes=[
            pltpu.VMEM((N_HEADS, n_rb, 2 * rb, 2 * HEAD_DIM), BF16),
            pltpu.VMEM((N_HEADS, n_rb, 2 * rb, LANES), F32),
            pltpu.VMEM((N_HEADS, n_rb, 2 * rb, LANES), F32),
            pltpu.VMEM((N_HEADS, n_rb, 2 * rb, V_DIM), F32),
        ],
        compiler_params=_cparams(("parallel", "arbitrary")),
        name="attn_prompt",
    )(*lam_vecs, g_subln, q16, kt, v16)


PAGED_CHAINS = 1


def _paged_rows(dec_seq):
    rph = 2 * dec_seq
    n_rows = N_HEADS * rph
    row = lax.broadcasted_iota(jnp.int32, (n_rows, 1), 0)
    head = row // rph
    comp = (row % rph) // dec_seq
    tok = row % dec_seq
    slope = jnp.zeros((n_rows, 1), F32)
    for h in range(N_HEADS):
        slope = jnp.where(head == h, ALIBI_SLOPES[h], slope)
    return rph, n_rows, head, comp, tok, slope


def _paged_init(q_ref, q2_scr, qbd_scr, m_scr, l_scr, acc_scr, *, dec_seq):
    rph, n_rows, head, comp, _, _ = _paged_rows(dec_seq)
    q = q_ref[...]
    q2_scr[0:dec_seq, :] = q
    q2_scr[dec_seq:rph, :] = q
    qrep = jnp.concatenate([q2_scr[...]] * N_HEADS, axis=0)
    col = lax.broadcasted_iota(jnp.int32, (n_rows, D_ATTN), 1)
    qbd_scr[...] = jnp.where(col // HEAD_DIM == head * 2 + comp, qrep, 0.0).astype(BF16)
    m_scr[...] = jnp.full(m_scr.shape, NEG_BIG, F32)
    l_scr[...] = jnp.zeros(l_scr.shape, F32)
    acc_scr[...] = jnp.zeros(acc_scr.shape, F32)


def _paged_update(c, s, v_refs, n_slots, m_scr, l_scr, acc_scr, rph):
    m_prev = m_scr[c]
    m_new = jnp.maximum(m_prev, jnp.max(s, axis=-1, keepdims=True))
    alpha = jnp.exp2(m_prev - m_new)
    p = jnp.exp2(s - m_new)
    l_scr[c] = alpha * l_scr[c] + jnp.sum(p, axis=-1, keepdims=True)
    p16 = p.astype(BF16)
    for h in range(N_HEADS):
        rs = slice(h * rph, (h + 1) * rph)
        vh = jnp.concatenate([r[pl.ds(h, n_slots, stride=N_HEADS), :] for r in v_refs], axis=0)
        pv = jnp.dot(p16[rs, :], vh.astype(BF16), preferred_element_type=F32)
        acc_scr[c, rs, :] = alpha[rs] * acc_scr[c, rs, :] + pv
    m_scr[c] = m_new


def _paged_pages(j, ktp_refs, vip_refs, qbd_scr, m_scr, l_scr, acc_scr, *, past_len, dec_seq):
    rph, _, _, _, _, slope = _paged_rows(dec_seq)
    pps = len(ktp_refs)
    ppc = pps // PAGED_CHAINS
    qbd = qbd_scr[...]
    kcol = lax.broadcasted_iota(jnp.int32, (1, ppc * PAGE_SIZE), 1)
    for c in range(PAGED_CHAINS):
        pages = range(c * ppc, (c + 1) * ppc)
        s = jnp.concatenate(
            [jnp.dot(qbd, ktp_refs[i][...].astype(BF16), preferred_element_type=F32) for i in pages], axis=-1)
        koff = (kcol + ((j * pps + c * ppc) * PAGE_SIZE - past_len)).astype(F32)
        _paged_update(c, s + slope * koff, [vip_refs[i] for i in pages], PAGE_SIZE, m_scr, l_scr, acc_scr, rph)


def _paged_finish(b, lam_refs, gsub_ref, ktn_ref, vin_ref, o_ref, qbd_scr, m_scr, l_scr, acc_scr, *,
                  dec_seq, lam_init):
    rph, n_rows, _, _, tok, slope = _paged_rows(dec_seq)
    n_new = ktn_ref.shape[1]
    sn = jnp.dot(qbd_scr[...], ktn_ref[...].astype(BF16), preferred_element_type=F32)
    col = lax.broadcasted_iota(jnp.int32, (n_rows, n_new), 1)
    ctok = col % dec_seq
    valid = jnp.logical_and(col // dec_seq == b, ctok <= tok)
    sn = jnp.where(valid, sn + slope * ctok.astype(F32), NEG_BIG)
    _paged_update(0, sn, [vin_ref], n_new, m_scr, l_scr, acc_scr, rph)

    m = m_scr[0]
    for c in range(1, PAGED_CHAINS):
        m = jnp.maximum(m, m_scr[c])
    l = jnp.zeros((n_rows, 1), F32)
    acc = jnp.zeros((n_rows, V_DIM), F32)
    for c in range(PAGED_CHAINS):
        w = jnp.exp2(m_scr[c] - m)
        l = l + w * l_scr[c]
        acc = acc + w * acc_scr[c]

    lam = _diff_lambda(*[r[...] for r in lam_refs], lam_init)
    o = acc / l
    for h in range(N_HEADS):
        oh = o[h * rph:h * rph + dec_seq] - lam * o[h * rph + dec_seq:(h + 1) * rph]
        o_ref[:, h * V_DIM:(h + 1) * V_DIM] = _rms(oh, gsub_ref[...]) * (1.0 - lam_init)


SUBLANES = 8
CONV_HALO = 32
CONV_ROW_CHUNK = 16


def _ln_swish(c, g, b):
    mu = jnp.mean(c, axis=-1, keepdims=True)
    d = c - mu
    var = jnp.mean(d * d, axis=-1, keepdims=True)
    y = d * lax.rsqrt(var + EPS) * g + b
    return y * jax.nn.sigmoid(y)


def _conv_prompt_kernel(prev_ref, cur_ref, w_ref, b_ref, g_ref, beta_ref, o_ref, ext_scr, sh_scr, *, tm):
    i = pl.program_id(1)
    n_ext = CONV_HALO + tm
    ext_scr[0:CONV_HALO, :] = jnp.where(i == 0, 0.0, prev_ref[...])
    ext_scr[CONV_HALO:n_ext, :] = cur_ref[...]
    ext = ext_scr[...]
    for r in range(1, SUBLANES):
        sh_scr[r] = pltpu.roll(ext, n_ext - r, axis=0)
    first = CONV_HALO - (CONV_WIDTH - 1)
    for c in range(0, tm, CONV_ROW_CHUNK):
        acc = jnp.broadcast_to(b_ref[...], (CONV_ROW_CHUNK, D_CONV))
        for j in range(CONV_WIDTH):
            r = (first + j) % SUBLANES
            a = c + first + j - r
            src = ext_scr[a:a + CONV_ROW_CHUNK, :] if r == 0 else sh_scr[r, a:a + CONV_ROW_CHUNK, :]
            acc = acc + jnp.concatenate([w_ref[j]] * (CONV_ROW_CHUNK // SUBLANES), axis=0) * src
        o_ref[c:c + CONV_ROW_CHUNK, :] = _ln_swish(acc, g_ref[...], beta_ref[...]).astype(o_ref.dtype)


def _conv_prompt(u2d, w_dw, b_dw, ln_g, ln_b, batch, seq, tm=PROMPT_ROW_TILE):
    nt = seq // tm
    halo_per_tile = tm // CONV_HALO
    kern = functools.partial(_conv_prompt_kernel, tm=tm)
    vec = pl.BlockSpec((1, D_CONV), lambda b, i: (0, 0))
    w_rep = jnp.broadcast_to(w_dw[:, None, :], (CONV_WIDTH, SUBLANES, D_CONV))
    return pl.pallas_call(
        kern,
        grid=(batch, nt),
        in_specs=[
            pl.BlockSpec((CONV_HALO, D_CONV),
                         lambda b, i: (jnp.maximum((b * nt + i) * halo_per_tile - 1, 0), 0)),
            pl.BlockSpec((tm, D_CONV), lambda b, i: (b * nt + i, 0)),
            pl.BlockSpec((CONV_WIDTH, SUBLANES, D_CONV), lambda b, i: (0, 0, 0)),
            vec, vec, vec,
        ],
        out_specs=pl.BlockSpec((tm, D_CONV), lambda b, i: (b * nt + i, 0)),
        out_shape=jax.ShapeDtypeStruct((batch * seq, D_CONV), BF16),
        scratch_shapes=[pltpu.VMEM((CONV_HALO + tm, D_CONV), F32),
                        pltpu.VMEM((SUBLANES, CONV_HALO + tm, D_CONV), F32)],
        compiler_params=_cparams(("parallel", "parallel")),
        name="conv_prompt",
    )(u2d, u2d, w_rep, b_dw, ln_g, ln_b)


def _conv_sample_kernel(state_ref, u_ref, w_ref, b_ref, g_ref, beta_ref, o_ref, st_ref, *, dec_batch, dec_seq):
    n_state = CONV_WIDTH - 1

    def ext(i):
        return state_ref[i] if i < n_state else u_ref[i - n_state]

    for t in range(dec_seq):
        acc = jnp.broadcast_to(b_ref[...], (dec_batch, D_CONV))
        for j in range(CONV_WIDTH):
            acc = acc + w_ref[j:j + 1, :] * ext(t + j)
        o_ref[t] = _ln_swish(acc, g_ref[...], beta_ref[...])
    for i in range(n_state):
        st_ref[i] = ext(i + dec_seq)


def _conv_sample(state_tm, u_s, w_dw, b_dw, ln_g, ln_b, dec_batch, dec_seq):
    kern = functools.partial(_conv_sample_kernel, dec_batch=dec_batch, dec_seq=dec_seq)
    u_tm = jnp.transpose(u_s.reshape(dec_batch, dec_seq, D_CONV), (1, 0, 2))
    y_tm, st_tm = pl.pallas_call(
        kern,
        out_shape=[
            jax.ShapeDtypeStruct((dec_seq, dec_batch, D_CONV), F32),
            jax.ShapeDtypeStruct(state_tm.shape, F32),
        ],
        compiler_params=pltpu.CompilerParams(vmem_limit_bytes=VMEM_LIMIT_BYTES),
        name="conv_sample",
    )(state_tm, u_tm, w_dw, b_dw, ln_g, ln_b)
    return jnp.transpose(y_tm, (1, 0, 2)).reshape(dec_batch * dec_seq, D_CONV), st_tm


def _post_first(ya_ref, yc_ref, x_ref, wo_ref, gpm_ref, gpf_ref, h_scr, hn_scr, acc_scr):
    y_mix = jnp.concatenate([ya_ref[...].astype(BF16), yc_ref[...].astype(BF16)], axis=-1)
    m = jnp.dot(y_mix, wo_ref[...], preferred_element_type=F32)
    h = x_ref[...] + _rms(m, gpm_ref[...])
    h_scr[...] = h
    hn_scr[...] = _rms(h, gpf_ref[...]).astype(BF16)
    acc_scr[...] = jnp.zeros(acc_scr.shape, F32)


def _post_ffn_chunk(w1_ref, w2_ref, hn_scr, acc_scr):
    a = jnp.dot(hn_scr[...], w1_ref[...], preferred_element_type=F32)
    a = jnp.square(jnp.maximum(a, 0.0)).astype(BF16)
    acc_scr[...] += jnp.dot(a, w2_ref[...], preferred_element_type=F32)


def _post_last(pe_ref, gpost_ref, wg_ref, wp_ref, o_ref, h_scr, acc_scr):
    h2 = h_scr[...] + _rms(acc_scr[...], gpost_ref[...])
    gate = jax.nn.sigmoid(jnp.dot(h2.astype(BF16), wg_ref[...], preferred_element_type=F32))
    pev = jnp.dot(pe_ref[...].astype(BF16), wp_ref[...], preferred_element_type=F32)
    o_ref[...] = h2 + gate * pev


def _post_kernel(ya_ref, yc_ref, x_ref, pe_ref, wo_ref, gpm_ref, gpf_ref, w1_ref, w2_ref, gpost_ref,
                 wg_ref, wp_ref, o_ref, h_scr, hn_scr, acc_scr):
    k = pl.program_id(1)

    @pl.when(k == 0)
    def _():
        _post_first(ya_ref, yc_ref, x_ref, wo_ref, gpm_ref, gpf_ref, h_scr, hn_scr, acc_scr)

    _post_ffn_chunk(w1_ref, w2_ref, hn_scr, acc_scr)

    @pl.when(k == pl.num_programs(1) - 1)
    def _():
        _post_last(pe_ref, gpost_ref, wg_ref, wp_ref, o_ref, h_scr, acc_scr)


N_POST_IN = 12
N_PAGED_IN = 10


def _page_copies(pt_ref, seq, chunk, slot, kt_pool_ref, vi_pool_ref, kbuf, vbuf, sem, pps):
    copies = []
    for p in range(pps):
        page = pt_ref[seq, chunk * pps + p]
        copies.append(pltpu.make_async_copy(kt_pool_ref.at[page], kbuf.at[slot, p], sem.at[slot, 0]))
        copies.append(pltpu.make_async_copy(vi_pool_ref.at[page], vbuf.at[slot, p], sem.at[slot, 1]))
    return copies


def _post_paged_kernel(pt_ref, *refs, pps, past_len, dec_seq, lam_init):
    (ya_ref, yc_ref, x_ref, pe_ref, wo_ref, gpm_ref, gpf_ref, w1_ref, w2_ref, gpost_ref, wg_ref,
     wp_ref) = refs[:N_POST_IN]
    lam_refs = refs[N_POST_IN:N_POST_IN + 4]
    gsub_ref, q_ref, ktn_ref, vin_ref, kt_pool_ref, vi_pool_ref = refs[N_POST_IN + 4:N_POST_IN + N_PAGED_IN]
    o_ref, oa_ref = refs[N_POST_IN + N_PAGED_IN:N_POST_IN + N_PAGED_IN + 2]
    (h_scr, hn_scr, acc_scr, q2_scr, qbd_scr, m_scr, l_scr, pacc_scr, kbuf, vbuf,
     sem) = refs[N_POST_IN + N_PAGED_IN + 2:]
    i = pl.program_id(0)
    k = pl.program_id(1)
    n_k = pl.num_programs(1)
    step = i * n_k + k
    slot = step % 2
    copies = functools.partial(_page_copies, pt_ref, kt_pool_ref=kt_pool_ref, vi_pool_ref=vi_pool_ref,
                               kbuf=kbuf, vbuf=vbuf, sem=sem, pps=pps)

    @pl.when(step == 0)
    def _():
        for c in copies(0, 0, 0):
            c.start()

    @pl.when(step + 1 < pl.num_programs(0) * n_k)
    def _():
        nxt = step + 1
        for c in copies(nxt // n_k, nxt % n_k, 1 - slot):
            c.start()

    @pl.when(k == 0)
    def _():
        _post_first(ya_ref, yc_ref, x_ref, wo_ref, gpm_ref, gpf_ref, h_scr, hn_scr, acc_scr)
        _paged_init(q_ref, q2_scr, qbd_scr, m_scr, l_scr, pacc_scr, dec_seq=dec_seq)

    _post_ffn_chunk(w1_ref, w2_ref, hn_scr, acc_scr)
    for c in copies(i, k, slot):
        c.wait()
    _paged_pages(k, [kbuf.at[slot, p] for p in range(pps)], [vbuf.at[slot, p] for p in range(pps)],
                 qbd_scr, m_scr, l_scr, pacc_scr, past_len=past_len, dec_seq=dec_seq)

    @pl.when(k == n_k - 1)
    def _():
        _post_last(pe_ref, gpost_ref, wg_ref, wp_ref, o_ref, h_scr, acc_scr)
        _paged_finish(i, lam_refs, gsub_ref, ktn_ref, vin_ref, oa_ref, qbd_scr, m_scr, l_scr, pacc_scr,
                      dec_seq=dec_seq, lam_init=lam_init)


POST_ROW_TILE = PROMPT_ROW_TILE
POST_FF_CHUNK = 1024


def _post_specs(tm, tf, imap):
    row_tile = lambda n: pl.BlockSpec((tm, n), imap(lambda i, k: (i, 0)))
    vec = pl.BlockSpec((1, D_MODEL), imap(lambda i, k: (0, 0)))
    resident = lambda r, c: pl.BlockSpec((r, c), imap(lambda i, k: (0, 0)), pipeline_mode=pl.Buffered(1))
    in_specs = [
        row_tile(D_ATTN), row_tile(D_CONV), row_tile(D_MODEL), row_tile(D_PLE),
        resident(D_MODEL, D_MODEL), vec, vec,
        pl.BlockSpec((D_MODEL, tf), imap(lambda i, k: (0, k))),
        pl.BlockSpec((tf, D_MODEL), imap(lambda i, k: (k, 0))),
        vec,
        resident(D_MODEL, D_MODEL), resident(D_PLE, D_MODEL),
    ]
    scratch = [pltpu.VMEM((tm, D_MODEL), F32), pltpu.VMEM((tm, D_MODEL), BF16), pltpu.VMEM((tm, D_MODEL), F32)]
    return in_specs, row_tile(D_MODEL), scratch


def _post(post_args, tm):
    m = post_args[2].shape[0]
    in_specs, out_spec, scratch = _post_specs(tm, POST_FF_CHUNK, lambda f: f)
    return pl.pallas_call(
        _post_kernel,
        grid=(m // tm, D_FF // POST_FF_CHUNK),
        in_specs=in_specs,
        out_specs=out_spec,
        out_shape=jax.ShapeDtypeStruct((m, D_MODEL), F32),
        scratch_shapes=scratch,
        compiler_params=_cparams(("parallel", "arbitrary")),
        name="post",
    )(*post_args)


def _post_paged(post_args, q_s, kt_new, vi_new, kt_pool, vi_pool, page_table, lam_vecs, g_subln,
                dec_batch, dec_seq, lam_init):
    tm, tf = POST_ROW_TILE, POST_FF_CHUNK
    m = post_args[2].shape[0]
    n_row_tiles, n_ff = m // tm, D_FF // tf
    n_pages = page_table.shape[1]
    assert n_row_tiles == dec_batch and n_pages % n_ff == 0
    pps = n_pages // n_ff
    assert pps % PAGED_CHAINS == 0
    past_len = n_pages * PAGE_SIZE
    rph = 2 * dec_seq
    n_rows = N_HEADS * rph
    n_new = dec_batch * dec_seq
    q3 = q_s.astype(F32).reshape(dec_batch, dec_seq, D_ATTN)
    with_pt = lambda f: (lambda i, k, pt: f(i, k))
    post_in, post_out, post_scratch = _post_specs(tm, tf, with_pt)
    vec = lambda n: pl.BlockSpec((1, n), lambda i, k, pt: (0, 0))

    grid_spec = pltpu.PrefetchScalarGridSpec(
        num_scalar_prefetch=1,
        grid=(n_row_tiles, n_ff),
        in_specs=post_in + [vec(HEAD_DIM)] * 4 + [
            vec(V_DIM),
            pl.BlockSpec((None, dec_seq, D_ATTN), lambda i, k, pt: (i, 0, 0)),
            pl.BlockSpec((D_ATTN, n_new), lambda i, k, pt: (0, 0)),
            pl.BlockSpec((n_new * N_HEADS, V_DIM), lambda i, k, pt: (0, 0)),
            pl.BlockSpec(memory_space=pl.ANY),
            pl.BlockSpec(memory_space=pl.ANY),
        ],
        out_specs=[post_out, pl.BlockSpec((None, dec_seq, D_ATTN), lambda i, k, pt: (i, 0, 0))],
        scratch_shapes=post_scratch + [
            pltpu.VMEM((rph, D_ATTN), F32),
            pltpu.VMEM((n_rows, D_ATTN), BF16),
            pltpu.VMEM((PAGED_CHAINS, n_rows, 1), F32),
            pltpu.VMEM((PAGED_CHAINS, n_rows, 1), F32),
            pltpu.VMEM((PAGED_CHAINS, n_rows, V_DIM), F32),
            pltpu.VMEM((2, pps, D_ATTN, PAGE_SIZE), F32),
            pltpu.VMEM((2, pps, PAGE_SIZE * N_HEADS, V_DIM), F32),
            pltpu.SemaphoreType.DMA((2, 2)),
        ],
    )
    kern = functools.partial(_post_paged_kernel, pps=pps, past_len=past_len, dec_seq=dec_seq, lam_init=lam_init)
    out, ya_s = pl.pallas_call(
        kern,
        grid_spec=grid_spec,
        out_shape=[jax.ShapeDtypeStruct((m, D_MODEL), F32),
                   jax.ShapeDtypeStruct((dec_batch, dec_seq, D_ATTN), F32)],
        compiler_params=_cparams(("arbitrary", "arbitrary")),
        name="post_paged",
    )(page_table, *post_args, *lam_vecs, g_subln, q3, kt_new, vi_new, kt_pool, vi_pool)
    return out, ya_s.reshape(dec_batch * dec_seq, D_ATTN)


def _row(v):
    return v.reshape(1, -1)


def kernel(x_prompt, x_sample, cache_k, cache_v, state_conv, page_table, p_prompt, p_sample,
           w_in, w_out, lambda_q1, lambda_k1, lambda_q2, lambda_k2, g_subln, w_dw, b_dw,
           ln_conv_g, ln_conv_b, g_pre_mix, g_post_mix, g_pre_ffn, g_post_ffn,
           w_ff1, w_ff2, w_ple, w_ple_gate):
    depth = w_in.shape[0]
    batch, seq, _ = x_prompt.shape
    dec_batch, dec_seq, _ = x_sample.shape
    n_pool = cache_k.shape[1]
    mp, ms = batch * seq, dec_batch * dec_seq
    assert 2 * dec_seq == 8 and cache_k.shape[2] == PAGE_SIZE

    hp = x_prompt.reshape(mp, D_MODEL)
    hs = x_sample.reshape(ms, D_MODEL)
    outs = {k: [] for k in ("kp", "vp", "cp", "ks", "vs", "cs")}
    for l in range(depth):
        lam_init = 0.8 - 0.6 * math.exp(-0.3 * l)
        w_in16 = w_in[l].astype(BF16)
        w_ple16 = w_ple[l].astype(BF16)
        lam_vecs = [_row(lambda_q1[l]), _row(lambda_k1[l]), _row(lambda_q2[l]), _row(lambda_k2[l])]
        gsub = _row(g_subln[l])
        conv_args = (w_dw[l], _row(b_dw[l]), _row(ln_conv_g[l]), _row(ln_conv_b[l]))
        kt_pool = jnp.transpose(cache_k[l], (0, 2, 3, 4, 1)).reshape(n_pool, D_ATTN, PAGE_SIZE)
        vi_pool = cache_v[l].reshape(n_pool, PAGE_SIZE * N_HEADS, V_DIM)

        q_p, kt_p, vi_p, v16_p, u_p, w_out16, w_ff1_16, w_ff2_16, w_gate16 = _inproj(
            hp, _row(g_pre_mix[l]), w_in16, n_seq=batch, tm=PROMPT_ROW_TILE,
            cast_weights=(w_out[l], w_ff1[l], w_ff2[l], w_ple_gate[l]))
        post_w = (w_out16, _row(g_post_mix[l]), _row(g_pre_ffn[l]), w_ff1_16, w_ff2_16, _row(g_post_ffn[l]),
                  w_gate16, w_ple16)
        ya_p = _attn_prompt(q_p, kt_p, v16_p, lam_vecs, gsub, batch, seq, lam_init)
        yc_p = _conv_prompt(u_p, *conv_args, batch, seq)
        q_s, kt_s, vi_s, _, u_s = _inproj(hs, _row(g_pre_mix[l]), w_in16, n_seq=1, tm=ms)
        yc_s, st_tm = _conv_sample(jnp.transpose(state_conv[l], (1, 0, 2)), u_s, *conv_args,
                                   dec_batch, dec_seq)

        hp, ya_s = _post_paged((ya_p, yc_p, hp, p_prompt[l].reshape(mp, D_PLE)) + post_w,
                               q_s, kt_s[0], vi_s, kt_pool, vi_pool, page_table, lam_vecs, gsub,
                               dec_batch, dec_seq, lam_init)
        hs = _post((ya_s, yc_s, hs, p_sample[l].reshape(ms, D_PLE)) + post_w, tm=ms)

        outs["kp"].append(jnp.transpose(kt_p.reshape(batch, N_HEADS, 2, HEAD_DIM, seq), (0, 4, 1, 2, 3)))
        outs["vp"].append(vi_p.reshape(batch, seq, N_HEADS, V_DIM))
        outs["cp"].append(u_p.reshape(batch, seq, D_CONV)[:, seq - (CONV_WIDTH - 1):])
        outs["ks"].append(jnp.transpose(kt_s.reshape(N_HEADS, 2, HEAD_DIM, dec_batch, dec_seq),
                                        (3, 4, 0, 1, 2)))
        outs["vs"].append(vi_s.reshape(dec_batch, dec_seq, N_HEADS, V_DIM))
        outs["cs"].append(jnp.transpose(st_tm, (1, 0, 2)))

    return (hp.reshape(batch, seq, D_MODEL), hs.reshape(dec_batch, dec_seq, D_MODEL),
            jnp.stack(outs["kp"]), jnp.stack(outs["vp"]), jnp.stack(outs["cp"]),
            jnp.stack(outs["ks"]), jnp.stack(outs["vs"]), jnp.stack(outs["cs"]))
```

```python
import functools
import math

import jax
import jax.numpy as jnp
from jax import lax
from jax.experimental import pallas as pl
from jax.experimental.pallas import tpu as pltpu

D_MODEL = 1024
D_ATTN = 512
D_CONV = 512
N_HEADS = 4
HEAD_DIM = 64
V_DIM = 128
CONV_WIDTH = 31
D_FF = 4096
D_PLE = 256
PAGE_SIZE = 128
EPS = 1e-6
N_QKV = 3 * D_ATTN
N_IN = N_QKV + 2 * D_CONV
NEG_BIG = -1e30
LOG2E = math.log2(math.e)
Q_SCALE = HEAD_DIM ** -0.5 * LOG2E
ALIBI_SLOPES = tuple(2.0 ** (-8.0 * (h + 1) / N_HEADS) * LOG2E for h in range(N_HEADS))

F32 = jnp.float32
BF16 = jnp.bfloat16

VMEM_LIMIT_BYTES = 56 * 1024 * 1024
PROMPT_ROW_TILE = 512


def _cparams(semantics):
    return pltpu.CompilerParams(dimension_semantics=semantics, vmem_limit_bytes=VMEM_LIMIT_BYTES)


def _rms(x, g):
    return x * lax.rsqrt(jnp.mean(x * x, axis=-1, keepdims=True) + EPS) * g


def _diff_lambda(lq1, lk1, lq2, lk2, lam_init):
    return (jnp.exp(jnp.sum(lq1 * lk1, axis=-1, keepdims=True))
            - jnp.exp(jnp.sum(lq2 * lk2, axis=-1, keepdims=True)) + lam_init)


N_INPROJ_OUT = 5


def _inproj_kernel(x_ref, g_ref, w_ref, *refs, tm):
    n_cast = (len(refs) - N_INPROJ_OUT) // 2
    q_ref, kt_ref, vi_ref, v16_ref, u_ref = refs[n_cast:n_cast + N_INPROJ_OUT]
    for src_ref, dst_ref in zip(refs[:n_cast], refs[n_cast + N_INPROJ_OUT:]):
        dst_ref[...] = src_ref[...].astype(BF16)
    xn = _rms(x_ref[...], g_ref[...]).astype(BF16)
    q = jnp.dot(xn, w_ref[:, 0:D_ATTN], preferred_element_type=F32)
    q_ref[...] = (q * Q_SCALE).astype(BF16)
    kt_ref[...] = jnp.dot(xn, w_ref[:, D_ATTN:2 * D_ATTN], preferred_element_type=F32).T
    v = jnp.dot(xn, w_ref[:, 2 * D_ATTN:N_QKV], preferred_element_type=F32)
    v16_ref[...] = v.astype(BF16)
    for h in range(N_HEADS):
        vi_ref[pl.ds(h, tm, stride=N_HEADS), :] = v[:, h * V_DIM:(h + 1) * V_DIM]
    a = jnp.dot(xn, w_ref[:, N_QKV:N_QKV + D_CONV], preferred_element_type=F32)
    gt = jnp.dot(xn, w_ref[:, N_QKV + D_CONV:], preferred_element_type=F32)
    u_ref[...] = a * jax.nn.sigmoid(gt)


def _inproj(x2d, g_pre_mix, w_in16, n_seq, tm, cast_weights=()):
    m = x2d.shape[0]
    seq = m // n_seq
    nt = seq // tm
    n_steps = m // tm
    cast_specs = [pl.BlockSpec((w.shape[0] // n_steps, w.shape[1]), lambda i: (i, 0)) for w in cast_weights]
    return pl.pallas_call(
        functools.partial(_inproj_kernel, tm=tm),
        grid=(n_steps,),
        in_specs=[
            pl.BlockSpec((tm, D_MODEL), lambda i: (i, 0)),
            pl.BlockSpec((1, D_MODEL), lambda i: (0, 0)),
            pl.BlockSpec((D_MODEL, N_IN), lambda i: (0, 0)),
        ] + cast_specs,
        out_specs=[
            pl.BlockSpec((tm, D_ATTN), lambda i: (i, 0)),
            pl.BlockSpec((None, D_ATTN, tm), lambda i: (i // nt, 0, i % nt)),
            pl.BlockSpec((tm * N_HEADS, V_DIM), lambda i: (i, 0)),
            pl.BlockSpec((tm, D_ATTN), lambda i: (i, 0)),
            pl.BlockSpec((tm, D_CONV), lambda i: (i, 0)),
        ] + cast_specs,
        out_shape=[
            jax.ShapeDtypeStruct((m, D_ATTN), BF16),
            jax.ShapeDtypeStruct((n_seq, D_ATTN, seq), F32),
            jax.ShapeDtypeStruct((m * N_HEADS, V_DIM), F32),
            jax.ShapeDtypeStruct((m, D_ATTN), BF16),
            jax.ShapeDtypeStruct((m, D_CONV), F32),
        ] + [jax.ShapeDtypeStruct(w.shape, BF16) for w in cast_weights],
        compiler_params=_cparams(("parallel",)),
        name="inproj",
    )(x2d, g_pre_mix, w_in16, *cast_weights)


ATTN_TILE = 1024
ATTN_KEY_TILE = 256
ATTN_ROW_BLOCK = 256
LANES = 128


def _attn_prompt_kernel(lq1_ref, lk1_ref, lq2_ref, lk2_ref, gsub_ref, q_ref, kt_ref, v_ref, o_ref,
                        qs_scr, m_scr, l_scr, acc_scr, *, lam_init):
    t, w, rb = ATTN_TILE, ATTN_KEY_TILE, ATTN_ROW_BLOCK
    n_rb = t // rb
    qi = pl.program_id(1)
    lane = lax.broadcasted_iota(jnp.int32, (rb, 2 * HEAD_DIM), 1)
    kcol = lax.broadcasted_iota(jnp.int32, (1, w), 1)

    for h in range(N_HEADS):
        hs = slice(h * V_DIM, (h + 1) * V_DIM)
        for g in range(n_rb):
            q = q_ref[g * rb:(g + 1) * rb, hs]
            zero = jnp.zeros_like(q)
            qs_scr[h, g] = jnp.concatenate([jnp.where(lane < HEAD_DIM, q, zero),
                                            jnp.where(lane >= HEAD_DIM, q, zero)], axis=0)
    m_scr[...] = jnp.full(m_scr.shape, NEG_BIG, F32)
    l_scr[...] = jnp.zeros(l_scr.shape, F32)
    acc_scr[...] = jnp.zeros(acc_scr.shape, F32)

    def tile(k0, d):
        koff = (kcol + (k0 - qi * t)).astype(F32)
        for h in range(N_HEADS):
            hs = slice(h * V_DIM, (h + 1) * V_DIM)
            kt = kt_ref[hs, pl.ds(k0, w)].astype(BF16)
            vv = v_ref[pl.ds(k0, w), hs]
            bias = ALIBI_SLOPES[h] * koff
            for g in range(n_rb):
                masked = False
                if d is not None:
                    if (g + 1) * rb <= d * w:
                        continue
                    masked = g * rb < (d + 1) * w - 1
                s = jnp.dot(qs_scr[h, g], kt, preferred_element_type=F32) + bias
                if masked:
                    row = lax.broadcasted_iota(jnp.int32, (2 * rb, w), 0)
                    row = jnp.where(row >= rb, row - rb, row) + (g * rb - d * w)
                    col = lax.broadcasted_iota(jnp.int32, (2 * rb, w), 1)
                    s = jnp.where(col <= row, s, NEG_BIG)
                m_prev = m_scr[h, g]
                m_new = jnp.maximum(m_prev, jnp.max(s, axis=-1, keepdims=True))
                alpha = jnp.exp2(m_prev - m_new)
                p = jnp.exp2(s - jnp.concatenate([m_new] * (w // LANES), axis=-1))
                l_scr[h, g] = alpha * l_scr[h, g] + jnp.sum(p, axis=-1, keepdims=True)
                acc_scr[h, g] = alpha * acc_scr[h, g] + jnp.dot(p.astype(BF16), vv,
                                                               preferred_element_type=F32)
                m_scr[h, g] = m_new

    def below_diagonal(kv, carry):
        tile(pl.multiple_of(kv * w, w), None)
        return carry

    lax.fori_loop(0, qi * (t // w), below_diagonal, 0)
    for d in range(t // w):
        tile(pl.multiple_of(qi * t + d * w, w), d)

    lam = _diff_lambda(lq1_ref[...], lk1_ref[...], lq2_ref[...], lk2_ref[...], lam_init)
    for h in range(N_HEADS):
        for g in range(n_rb):
            o = acc_scr[h, g] / l_scr[h, g]
            oh = o[:rb] - lam * o[rb:]
            y = _rms(oh, gsub_ref[...]) * (1.0 - lam_init)
            o_ref[g * rb:(g + 1) * rb, h * V_DIM:(h + 1) * V_DIM] = y.astype(o_ref.dtype)


def _attn_prompt(q16, kt, v16, lam_vecs, g_subln, batch, seq, lam_init):
    t, rb = ATTN_TILE, ATTN_ROW_BLOCK
    nq = seq // t
    n_rb = t // rb
    kern = functools.partial(_attn_prompt_kernel, lam_init=lam_init)
    vec = lambda n: pl.BlockSpec((1, n), lambda b, qi: (0, 0))
    return pl.pallas_call(
        kern,
        grid=(batch, nq),
        in_specs=[vec(HEAD_DIM)] * 4 + [
            vec(V_DIM),
            pl.BlockSpec((t, D_ATTN), lambda b, qi: (b * nq + qi, 0)),
            pl.BlockSpec((None, D_ATTN, seq), lambda b, qi: (b, 0, 0)),
            pl.BlockSpec((seq, D_ATTN), lambda b, qi: (b, 0)),
        ],
        out_specs=pl.BlockSpec((t, D_ATTN), lambda b, qi: (b * nq + qi, 0)),
        out_shape=jax.ShapeDtypeStruct((batch * seq, D_ATTN), BF16),
        scratch_shapes=[
            pltpu.VMEM((N_HEADS, n_rb, 2 * rb, 2 * HEAD_DIM), BF16),
            pltpu.VMEM((N_HEADS, n_rb, 2 * rb, LANES), F32),
            pltpu.VMEM((N_HEADS, n_rb, 2 * rb, LANES), F32),
            pltpu.VMEM((N_HEADS, n_rb, 2 * rb, V_DIM), F32),
        ],
        compiler_params=_cparams(("parallel", "arbitrary")),
        name="attn_prompt",
    )(*lam_vecs, g_subln, q16, kt, v16)


PAGED_CHAINS = 1


def _paged_rows(dec_seq):
    rph = 2 * dec_seq
    n_rows = N_HEADS * rph
    row = lax.broadcasted_iota(jnp.int32, (n_rows, 1), 0)
    head = row // rph
    comp = (row % rph) // dec_seq
    tok = row % dec_seq
    slope = jnp.zeros((n_rows, 1), F32)
    for h in range(N_HEADS):
        slope = jnp.where(head == h, ALIBI_SLOPES[h], slope)
    return rph, n_rows, head, comp, tok, slope


def _paged_init(q_ref, q2_scr, qbd_scr, m_scr, l_scr, acc_scr, *, dec_seq):
    rph, n_rows, head, comp, _, _ = _paged_rows(dec_seq)
    q = q_ref[...]
    q2_scr[0:dec_seq, :] = q
    q2_scr[dec_seq:rph, :] = q
    qrep = jnp.concatenate([q2_scr[...]] * N_HEADS, axis=0)
    col = lax.broadcasted_iota(jnp.int32, (n_rows, D_ATTN), 1)
    qbd_scr[...] = jnp.where(col // HEAD_DIM == head * 2 + comp, qrep, 0.0).astype(BF16)
    m_scr[...] = jnp.full(m_scr.shape, NEG_BIG, F32)
    l_scr[...] = jnp.zeros(l_scr.shape, F32)
    acc_scr[...] = jnp.zeros(acc_scr.shape, F32)


def _paged_update(c, s, v_refs, n_slots, m_scr, l_scr, acc_scr, rph):
    m_prev = m_scr[c]
    m_new = jnp.maximum(m_prev, jnp.max(s, axis=-1, keepdims=True))
    alpha = jnp.exp2(m_prev - m_new)
    p = jnp.exp2(s - m_new)
    l_scr[c] = alpha * l_scr[c] + jnp.sum(p, axis=-1, keepdims=True)
    p16 = p.astype(BF16)
    for h in range(N_HEADS):
        rs = slice(h * rph, (h + 1) * rph)
        vh = jnp.concatenate([r[pl.ds(h, n_slots, stride=N_HEADS), :] for r in v_refs], axis=0)
        pv = jnp.dot(p16[rs, :], vh.astype(BF16), preferred_element_type=F32)
        acc_scr[c, rs, :] = alpha[rs] * acc_scr[c, rs, :] + pv
    m_scr[c] = m_new


def _paged_pages(j, ktp_refs, vip_refs, qbd_scr, m_scr, l_scr, acc_scr, *, past_len, dec_seq):
    rph, _, _, _, _, slope = _paged_rows(dec_seq)
    pps = len(ktp_refs)
    ppc = pps // PAGED_CHAINS
    qbd = qbd_scr[...]
    kcol = lax.broadcasted_iota(jnp.int32, (1, ppc * PAGE_SIZE), 1)
    for c in range(PAGED_CHAINS):
        pages = range(c * ppc, (c + 1) * ppc)
        s = jnp.concatenate(
            [jnp.dot(qbd, ktp_refs[i][...].astype(BF16), preferred_element_type=F32) for i in pages], axis=-1)
        koff = (kcol + ((j * pps + c * ppc) * PAGE_SIZE - past_len)).astype(F32)
        _paged_update(c, s + slope * koff, [vip_refs[i] for i in pages], PAGE_SIZE, m_scr, l_scr, acc_scr, rph)


def _paged_finish(b, lam_refs, gsub_ref, ktn_ref, vin_ref, o_ref, qbd_scr, m_scr, l_scr, acc_scr, *,
                  dec_seq, lam_init):
    rph, n_rows, _, _, tok, slope = _paged_rows(dec_seq)
    n_new = ktn_ref.shape[1]
    sn = jnp.dot(qbd_scr[...], ktn_ref[...].astype(BF16), preferred_element_type=F32)
    col = lax.broadcasted_iota(jnp.int32, (n_rows, n_new), 1)
    ctok = col % dec_seq
    valid = jnp.logical_and(col // dec_seq == b, ctok <= tok)
    sn = jnp.where(valid, sn + slope * ctok.astype(F32), NEG_BIG)
    _paged_update(0, sn, [vin_ref], n_new, m_scr, l_scr, acc_scr, rph)

    m = m_scr[0]
    for c in range(1, PAGED_CHAINS):
        m = jnp.maximum(m, m_scr[c])
    l = jnp.zeros((n_rows, 1), F32)
    acc = jnp.zeros((n_rows, V_DIM), F32)
    for c in range(PAGED_CHAINS):
        w = jnp.exp2(m_scr[c] - m)
        l = l + w * l_scr[c]
        acc = acc + w * acc_scr[c]

    lam = _diff_lambda(*[r[...] for r in lam_refs], lam_init)
    o = acc / l
    for h in range(N_HEADS):
        oh = o[h * rph:h * rph + dec_seq] - lam * o[h * rph + dec_seq:(h + 1) * rph]
        o_ref[:, h * V_DIM:(h + 1) * V_DIM] = _rms(oh, gsub_ref[...]) * (1.0 - lam_init)


SUBLANES = 8
CONV_HALO = 32
CONV_ROW_CHUNK = 16


def _ln_swish(c, g, b):
    mu = jnp.mean(c, axis=-1, keepdims=True)
    d = c - mu
    var = jnp.mean(d * d, axis=-1, keepdims=True)
    y = d * lax.rsqrt(var + EPS) * g + b
    return y * jax.nn.sigmoid(y)


def _conv_prompt_kernel(prev_ref, cur_ref, w_ref, b_ref, g_ref, beta_ref, o_ref, ext_scr, sh_scr, *, tm):
    i = pl.program_id(1)
    n_ext = CONV_HALO + tm
    ext_scr[0:CONV_HALO, :] = jnp.where(i == 0, 0.0, prev_ref[...])
    ext_scr[CONV_HALO:n_ext, :] = cur_ref[...]
    ext = ext_scr[...]
    for r in range(1, SUBLANES):
        sh_scr[r] = pltpu.roll(ext, n_ext - r, axis=0)
    first = CONV_HALO - (CONV_WIDTH - 1)
    for c in range(0, tm, CONV_ROW_CHUNK):
        acc = jnp.broadcast_to(b_ref[...], (CONV_ROW_CHUNK, D_CONV))
        for j in range(CONV_WIDTH):
            r = (first + j) % SUBLANES
            a = c + first + j - r
            src = ext_scr[a:a + CONV_ROW_CHUNK, :] if r == 0 else sh_scr[r, a:a + CONV_ROW_CHUNK, :]
            acc = acc + jnp.concatenate([w_ref[j]] * (CONV_ROW_CHUNK // SUBLANES), axis=0) * src
        o_ref[c:c + CONV_ROW_CHUNK, :] = _ln_swish(acc, g_ref[...], beta_ref[...]).astype(o_ref.dtype)


def _conv_prompt(u2d, w_dw, b_dw, ln_g, ln_b, batch, seq, tm=PROMPT_ROW_TILE):
    nt = seq // tm
    halo_per_tile = tm // CONV_HALO
    kern = functools.partial(_conv_prompt_kernel, tm=tm)
    vec = pl.BlockSpec((1, D_CONV), lambda b, i: (0, 0))
    w_rep = jnp.broadcast_to(w_dw[:, None, :], (CONV_WIDTH, SUBLANES, D_CONV))
    return pl.pallas_call(
        kern,
        grid=(batch, nt),
        in_specs=[
            pl.BlockSpec((CONV_HALO, D_CONV),
                         lambda b, i: (jnp.maximum((b * nt + i) * halo_per_tile - 1, 0), 0)),
            pl.BlockSpec((tm, D_CONV), lambda b, i: (b * nt + i, 0)),
            pl.BlockSpec((CONV_WIDTH, SUBLANES, D_CONV), lambda b, i: (0, 0, 0)),
            vec, vec, vec,
        ],
        out_specs=pl.BlockSpec((tm, D_CONV), lambda b, i: (b * nt + i, 0)),
        out_shape=jax.ShapeDtypeStruct((batch * seq, D_CONV), BF16),
        scratch_shapes=[pltpu.VMEM((CONV_HALO + tm, D_CONV), F32),
                        pltpu.VMEM((SUBLANES, CONV_HALO + tm, D_CONV), F32)],
        compiler_params=_cparams(("parallel", "parallel")),
        name="conv_prompt",
    )(u2d, u2d, w_rep, b_dw, ln_g, ln_b)


def _conv_sample_kernel(state_ref, u_ref, w_ref, b_ref, g_ref, beta_ref, o_ref, st_ref, *, dec_batch, dec_seq):
    n_state = CONV_WIDTH - 1

    def ext(i):
        return state_ref[i] if i < n_state else u_ref[i - n_state]

    for t in range(dec_seq):
        acc = jnp.broadcast_to(b_ref[...], (dec_batch, D_CONV))
        for j in range(CONV_WIDTH):
            acc = acc + w_ref[j:j + 1, :] * ext(t + j)
        o_ref[t] = _ln_swish(acc, g_ref[...], beta_ref[...])
    for i in range(n_state):
        st_ref[i] = ext(i + dec_seq)


def _conv_sample(state_tm, u_s, w_dw, b_dw, ln_g, ln_b, dec_batch, dec_seq):
    kern = functools.partial(_conv_sample_kernel, dec_batch=dec_batch, dec_seq=dec_seq)
    u_tm = jnp.transpose(u_s.reshape(dec_batch, dec_seq, D_CONV), (1, 0, 2))
    y_tm, st_tm = pl.pallas_call(
        kern,
        out_shape=[
            jax.ShapeDtypeStruct((dec_seq, dec_batch, D_CONV), F32),
            jax.ShapeDtypeStruct(state_tm.shape, F32),
        ],
        compiler_params=pltpu.CompilerParams(vmem_limit_bytes=VMEM_LIMIT_BYTES),
        name="conv_sample",
    )(state_tm, u_tm, w_dw, b_dw, ln_g, ln_b)
    return jnp.transpose(y_tm, (1, 0, 2)).reshape(dec_batch * dec_seq, D_CONV), st_tm


def _post_first(ya_ref, yc_ref, x_ref, wo_ref, gpm_ref, gpf_ref, h_scr, hn_scr, acc_scr):
    y_mix = jnp.concatenate([ya_ref[...].astype(BF16), yc_ref[...].astype(BF16)], axis=-1)
    m = jnp.dot(y_mix, wo_ref[...], preferred_element_type=F32)
    h = x_ref[...] + _rms(m, gpm_ref[...])
    h_scr[...] = h
    hn_scr[...] = _rms(h, gpf_ref[...]).astype(BF16)
    acc_scr[...] = jnp.zeros(acc_scr.shape, F32)


def _post_ffn_chunk(w1_ref, w2_ref, hn_scr, acc_scr):
    a = jnp.dot(hn_scr[...], w1_ref[...], preferred_element_type=F32)
    a = jnp.square(jnp.maximum(a, 0.0)).astype(BF16)
    acc_scr[...] += jnp.dot(a, w2_ref[...], preferred_element_type=F32)


def _post_last(pe_ref, gpost_ref, wg_ref, wp_ref, o_ref, h_scr, acc_scr):
    h2 = h_scr[...] + _rms(acc_scr[...], gpost_ref[...])
    gate = jax.nn.sigmoid(jnp.dot(h2.astype(BF16), wg_ref[...], preferred_element_type=F32))
    pev = jnp.dot(pe_ref[...].astype(BF16), wp_ref[...], preferred_element_type=F32)
    o_ref[...] = h2 + gate * pev


def _post_kernel(ya_ref, yc_ref, x_ref, pe_ref, wo_ref, gpm_ref, gpf_ref, w1_ref, w2_ref, gpost_ref,
                 wg_ref, wp_ref, o_ref, h_scr, hn_scr, acc_scr):
    k = pl.program_id(1)

    @pl.when(k == 0)
    def _():
        _post_first(ya_ref, yc_ref, x_ref, wo_ref, gpm_ref, gpf_ref, h_scr, hn_scr, acc_scr)

    _post_ffn_chunk(w1_ref, w2_ref, hn_scr, acc_scr)

    @pl.when(k == pl.num_programs(1) - 1)
    def _():
        _post_last(pe_ref, gpost_ref, wg_ref, wp_ref, o_ref, h_scr, acc_scr)


N_POST_IN = 12
N_PAGED_IN = 10


def _page_copies(pt_ref, seq, chunk, slot, kt_pool_ref, vi_pool_ref, kbuf, vbuf, sem, pps):
    copies = []
    for p in range(pps):
        page = pt_ref[seq, chunk * pps + p]
        copies.append(pltpu.make_async_copy(kt_pool_ref.at[page], kbuf.at[slot, p], sem.at[slot, 0]))
        copies.append(pltpu.make_async_copy(vi_pool_ref.at[page], vbuf.at[slot, p], sem.at[slot, 1]))
    return copies


def _post_paged_kernel(pt_ref, *refs, pps, past_len, dec_seq, lam_init):
    (ya_ref, yc_ref, x_ref, pe_ref, wo_ref, gpm_ref, gpf_ref, w1_ref, w2_ref, gpost_ref, wg_ref,
     wp_ref) = refs[:N_POST_IN]
    lam_refs = refs[N_POST_IN:N_POST_IN + 4]
    gsub_ref, q_ref, ktn_ref, vin_ref, kt_pool_ref, vi_pool_ref = refs[N_POST_IN + 4:N_POST_IN + N_PAGED_IN]
    o_ref, oa_ref = refs[N_POST_IN + N_PAGED_IN:N_POST_IN + N_PAGED_IN + 2]
    (h_scr, hn_scr, acc_scr, q2_scr, qbd_scr, m_scr, l_scr, pacc_scr, kbuf, vbuf,
     sem) = refs[N_POST_IN + N_PAGED_IN + 2:]
    i = pl.program_id(0)
    k = pl.program_id(1)
    n_k = pl.num_programs(1)
    step = i * n_k + k
    slot = step % 2
    copies = functools.partial(_page_copies, pt_ref, kt_pool_ref=kt_pool_ref, vi_pool_ref=vi_pool_ref,
                               kbuf=kbuf, vbuf=vbuf, sem=sem, pps=pps)

    @pl.when(step == 0)
    def _():
        for c in copies(0, 0, 0):
            c.start()

    @pl.when(step + 1 < pl.num_programs(0) * n_k)
    def _():
        nxt = step + 1
        for c in copies(nxt // n_k, nxt % n_k, 1 - slot):
            c.start()

    @pl.when(k == 0)
    def _():
        _post_first(ya_ref, yc_ref, x_ref, wo_ref, gpm_ref, gpf_ref, h_scr, hn_scr, acc_scr)
        _paged_init(q_ref, q2_scr, qbd_scr, m_scr, l_scr, pacc_scr, dec_seq=dec_seq)

    _post_ffn_chunk(w1_ref, w2_ref, hn_scr, acc_scr)
    for c in copies(i, k, slot):
        c.wait()
    _paged_pages(k, [kbuf.at[slot, p] for p in range(pps)], [vbuf.at[slot, p] for p in range(pps)],
                 qbd_scr, m_scr, l_scr, pacc_scr, past_len=past_len, dec_seq=dec_seq)

    @pl.when(k == n_k - 1)
    def _():
        _post_last(pe_ref, gpost_ref, wg_ref, wp_ref, o_ref, h_scr, acc_scr)
        _paged_finish(i, lam_refs, gsub_ref, ktn_ref, vin_ref, oa_ref, qbd_scr, m_scr, l_scr, pacc_scr,
                      dec_seq=dec_seq, lam_init=lam_init)


POST_ROW_TILE = PROMPT_ROW_TILE
POST_FF_CHUNK = 1024


def _post_specs(tm, tf, imap):
    row_tile = lambda n: pl.BlockSpec((tm, n), imap(lambda i, k: (i, 0)))
    vec = pl.BlockSpec((1, D_MODEL), imap(lambda i, k: (0, 0)))
    resident = lambda r, c: pl.BlockSpec((r, c), imap(lambda i, k: (0, 0)), pipeline_mode=pl.Buffered(1))
    in_specs = [
        row_tile(D_ATTN), row_tile(D_CONV), row_tile(D_MODEL), row_tile(D_PLE),
        resident(D_MODEL, D_MODEL), vec, vec,
        pl.BlockSpec((D_MODEL, tf), imap(lambda i, k: (0, k))),
        pl.BlockSpec((tf, D_MODEL), imap(lambda i, k: (k, 0))),
        vec,
        resident(D_MODEL, D_MODEL), resident(D_PLE, D_MODEL),
    ]
    scratch = [pltpu.VMEM((tm, D_MODEL), F32), pltpu.VMEM((tm, D_MODEL), BF16), pltpu.VMEM((tm, D_MODEL), F32)]
    return in_specs, row_tile(D_MODEL), scratch


def _post(post_args, tm):
    m = post_args[2].shape[0]
    in_specs, out_spec, scratch = _post_specs(tm, POST_FF_CHUNK, lambda f: f)
    return pl.pallas_call(
        _post_kernel,
        grid=(m // tm, D_FF // POST_FF_CHUNK),
        in_specs=in_specs,
        out_specs=out_spec,
        out_shape=jax.ShapeDtypeStruct((m, D_MODEL), F32),
        scratch_shapes=scratch,
        compiler_params=_cparams(("parallel", "arbitrary")),
        name="post",
    )(*post_args)


def _post_paged(post_args, q_s, kt_new, vi_new, kt_pool, vi_pool, page_table, lam_vecs, g_subln,
                dec_batch, dec_seq, lam_init):
    tm, tf = POST_ROW_TILE, POST_FF_CHUNK
    m = post_args[2].shape[0]
    n_row_tiles, n_ff = m // tm, D_FF // tf
    n_pages = page_table.shape[1]
    assert n_row_tiles == dec_batch and n_pages % n_ff == 0
    pps = n_pages // n_ff
    assert pps % PAGED_CHAINS == 0
    past_len = n_pages * PAGE_SIZE
    rph = 2 * dec_seq
    n_rows = N_HEADS * rph
    n_new = dec_batch * dec_seq
    q3 = q_s.astype(F32).reshape(dec_batch, dec_seq, D_ATTN)
    with_pt = lambda f: (lambda i, k, pt: f(i, k))
    post_in, post_out, post_scratch = _post_specs(tm, tf, with_pt)
    vec = lambda n: pl.BlockSpec((1, n), lambda i, k, pt: (0, 0))

    grid_spec = pltpu.PrefetchScalarGridSpec(
        num_scalar_prefetch=1,
        grid=(n_row_tiles, n_ff),
        in_specs=post_in + [vec(HEAD_DIM)] * 4 + [
            vec(V_DIM),
            pl.BlockSpec((None, dec_seq, D_ATTN), lambda i, k, pt: (i, 0, 0)),
            pl.BlockSpec((D_ATTN, n_new), lambda i, k, pt: (0, 0)),
            pl.BlockSpec((n_new * N_HEADS, V_DIM), lambda i, k, pt: (0, 0)),
            pl.BlockSpec(memory_space=pl.ANY),
            pl.BlockSpec(memory_space=pl.ANY),
        ],
        out_specs=[post_out, pl.BlockSpec((None, dec_seq, D_ATTN), lambda i, k, pt: (i, 0, 0))],
        scratch_shapes=post_scratch + [
            pltpu.VMEM((rph, D_ATTN), F32),
            pltpu.VMEM((n_rows, D_ATTN), BF16),
            pltpu.VMEM((PAGED_CHAINS, n_rows, 1), F32),
            pltpu.VMEM((PAGED_CHAINS, n_rows, 1), F32),
            pltpu.VMEM((PAGED_CHAINS, n_rows, V_DIM), F32),
            pltpu.VMEM((2, pps, D_ATTN, PAGE_SIZE), F32),
            pltpu.VMEM((2, pps, PAGE_SIZE * N_HEADS, V_DIM), F32),
            pltpu.SemaphoreType.DMA((2, 2)),
        ],
    )
    kern = functools.partial(_post_paged_kernel, pps=pps, past_len=past_len, dec_seq=dec_seq, lam_init=lam_init)
    out, ya_s = pl.pallas_call(
        kern,
        grid_spec=grid_spec,
        out_shape=[jax.ShapeDtypeStruct((m, D_MODEL), F32),
                   jax.ShapeDtypeStruct((dec_batch, dec_seq, D_ATTN), F32)],
        compiler_params=_cparams(("arbitrary", "arbitrary")),
        name="post_paged",
    )(page_table, *post_args, *lam_vecs, g_subln, q3, kt_new, vi_new, kt_pool, vi_pool)
    return out, ya_s.reshape(dec_batch * dec_seq, D_ATTN)


def _row(v):
    return v.reshape(1, -1)


def kernel(x_prompt, x_sample, cache_k, cache_v, state_conv, page_table, p_prompt, p_sample,
           w_in, w_out, lambda_q1, lambda_k1, lambda_q2, lambda_k2, g_subln, w_dw, b_dw,
           ln_conv_g, ln_conv_b, g_pre_mix, g_post_mix, g_pre_ffn, g_post_ffn,
           w_ff1, w_ff2, w_ple, w_ple_gate):
    depth = w_in.shape[0]
    batch, seq, _ = x_prompt.shape
    dec_batch, dec_seq, _ = x_sample.shape
    n_pool = cache_k.shape[1]
    mp, ms = batch * seq, dec_batch * dec_seq
    assert 2 * dec_seq == 8 and cache_k.shape[2] == PAGE_SIZE

    hp = x_prompt.reshape(mp, D_MODEL)
    hs = x_sample.reshape(ms, D_MODEL)
    outs = {k: [] for k in ("kp", "vp", "cp", "ks", "vs", "cs")}
    for l in range(depth):
        lam_init = 0.8 - 0.6 * math.exp(-0.3 * l)
        w_in16 = w_in[l].astype(BF16)
        w_ple16 = w_ple[l].astype(BF16)
        lam_vecs = [_row(lambda_q1[l]), _row(lambda_k1[l]), _row(lambda_q2[l]), _row(lambda_k2[l])]
        gsub = _row(g_subln[l])
        conv_args = (w_dw[l], _row(b_dw[l]), _row(ln_conv_g[l]), _row(ln_conv_b[l]))
        kt_pool = jnp.transpose(cache_k[l], (0, 2, 3, 4, 1)).reshape(n_pool, D_ATTN, PAGE_SIZE)
        vi_pool = cache_v[l].reshape(n_pool, PAGE_SIZE * N_HEADS, V_DIM)

        q_p, kt_p, vi_p, v16_p, u_p, w_out16, w_ff1_16, w_ff2_16, w_gate16 = _inproj(
            hp, _row(g_pre_mix[l]), w_in16, n_seq=batch, tm=PROMPT_ROW_TILE,
            cast_weights=(w_out[l], w_ff1[l], w_ff2[l], w_ple_gate[l]))
        post_w = (w_out16, _row(g_post_mix[l]), _row(g_pre_ffn[l]), w_ff1_16, w_ff2_16, _row(g_post_ffn[l]),
                  w_gate16, w_ple16)
        ya_p = _attn_prompt(q_p, kt_p, v16_p, lam_vecs, gsub, batch, seq, lam_init)
        yc_p = _conv_prompt(u_p, *conv_args, batch, seq)
        q_s, kt_s, vi_s, _, u_s = _inproj(hs, _row(g_pre_mix[l]), w_in16, n_seq=1, tm=ms)
        yc_s, st_tm = _conv_sample(jnp.transpose(state_conv[l], (1, 0, 2)), u_s, *conv_args,
                                   dec_batch, dec_seq)

        hp, ya_s = _post_paged((ya_p, yc_p, hp, p_prompt[l].reshape(mp, D_PLE)) + post_w,
                               q_s, kt_s[0], vi_s, kt_pool, vi_pool, page_table, lam_vecs, gsub,
                               dec_batch, dec_seq, lam_init)
        hs = _post((ya_s, yc_s, hs, p_sample[l].reshape(ms, D_PLE)) + post_w, tm=ms)

        outs["kp"].append(jnp.transpose(kt_p.reshape(batch, N_HEADS, 2, HEAD_DIM, seq), (0, 4, 1, 2, 3)))
        outs["vp"].append(vi_p.reshape(batch, seq, N_HEADS, V_DIM))
        outs["cp"].append(u_p.reshape(batch, seq, D_CONV)[:, seq - (CONV_WIDTH - 1):])
        outs["ks"].append(jnp.transpose(kt_s.reshape(N_HEADS, 2, HEAD_DIM, dec_batch, dec_seq),
                                        (3, 4, 0, 1, 2)))
        outs["vs"].append(vi_s.reshape(dec_batch, dec_seq, N_HEADS, V_DIM))
        outs["cs"].append(jnp.transpose(st_tm, (1, 0, 2)))

    return (hp.reshape(batch, seq, D_MODEL), hs.reshape(dec_batch, dec_seq, D_MODEL),
            jnp.stack(outs["kp"]), jnp.stack(outs["vp"]), jnp.stack(outs["cp"]),
            jnp.stack(outs["ks"]), jnp.stack(outs["vs"]), jnp.stack(outs["cs"]))
```

```python
import functools
import math

import jax
import jax.numpy as jnp
from jax import lax
from jax.experimental import pallas as pl
from jax.experimental.pallas import tpu as pltpu

D_MODEL = 1024
D_ATTN = 512
D_CONV = 512
N_HEADS = 4
HEAD_DIM = 64
V_DIM = 128
CONV_WIDTH = 31
D_FF = 4096
D_PLE = 256
PAGE_SIZE = 128
EPS = 1e-6
N_QKV = 3 * D_ATTN
N_IN = N_QKV + 2 * D_CONV
NEG_BIG = -1e30
LOG2E = math.log2(math.e)
Q_SCALE = HEAD_DIM ** -0.5 * LOG2E
ALIBI_SLOPES = tuple(2.0 ** (-8.0 * (h + 1) / N_HEADS) * LOG2E for h in range(N_HEADS))

F32 = jnp.float32
BF16 = jnp.bfloat16

VMEM_LIMIT_BYTES = 56 * 1024 * 1024
PROMPT_ROW_TILE = 1024


def _cparams(semantics):
    return pltpu.CompilerParams(dimension_semantics=semantics, vmem_limit_bytes=VMEM_LIMIT_BYTES)


def _rms(x, g):
    return x * lax.rsqrt(jnp.mean(x * x, axis=-1, keepdims=True) + EPS) * g


def _diff_lambda(lq1, lk1, lq2, lk2, lam_init):
    return (jnp.exp(jnp.sum(lq1 * lk1, axis=-1, keepdims=True))
            - jnp.exp(jnp.sum(lq2 * lk2, axis=-1, keepdims=True)) + lam_init)


N_INPROJ_OUT = 5


def _inproj_kernel(x_ref, g_ref, w_ref, *refs, tm):
    n_cast = (len(refs) - N_INPROJ_OUT) // 2
    q_ref, kt_ref, vi_ref, v16_ref, u_ref = refs[n_cast:n_cast + N_INPROJ_OUT]
    for src_ref, dst_ref in zip(refs[:n_cast], refs[n_cast + N_INPROJ_OUT:]):
        dst_ref[...] = src_ref[...].astype(BF16)
    xn = _rms(x_ref[...], g_ref[...]).astype(BF16)
    q = jnp.dot(xn, w_ref[:, 0:D_ATTN], preferred_element_type=F32)
    q_ref[...] = (q * Q_SCALE).astype(BF16)
    kt_ref[...] = jnp.dot(xn, w_ref[:, D_ATTN:2 * D_ATTN], preferred_element_type=F32).T
    v = jnp.dot(xn, w_ref[:, 2 * D_ATTN:N_QKV], preferred_element_type=F32)
    v16_ref[...] = v.astype(BF16)
    for h in range(N_HEADS):
        vi_ref[pl.ds(h, tm, stride=N_HEADS), :] = v[:, h * V_DIM:(h + 1) * V_DIM]
    a = jnp.dot(xn, w_ref[:, N_QKV:N_QKV + D_CONV], preferred_element_type=F32)
    gt = jnp.dot(xn, w_ref[:, N_QKV + D_CONV:], preferred_element_type=F32)
    u_ref[...] = a * jax.nn.sigmoid(gt)


def _inproj(x2d, g_pre_mix, w_in16, n_seq, tm, cast_weights=()):
    m = x2d.shape[0]
    seq = m // n_seq
    nt = seq // tm
    n_steps = m // tm
    cast_specs = [pl.BlockSpec((w.shape[0] // n_steps, w.shape[1]), lambda i: (i, 0)) for w in cast_weights]
    return pl.pallas_call(
        functools.partial(_inproj_kernel, tm=tm),
        grid=(n_steps,),
        in_specs=[
            pl.BlockSpec((tm, D_MODEL), lambda i: (i, 0)),
            pl.BlockSpec((1, D_MODEL), lambda i: (0, 0)),
            pl.BlockSpec((D_MODEL, N_IN), lambda i: (0, 0)),
        ] + cast_specs,
        out_specs=[
            pl.BlockSpec((tm, D_ATTN), lambda i: (i, 0)),
            pl.BlockSpec((None, D_ATTN, tm), lambda i: (i // nt, 0, i % nt)),
            pl.BlockSpec((tm * N_HEADS, V_DIM), lambda i: (i, 0)),
            pl.BlockSpec((tm, D_ATTN), lambda i: (i, 0)),
            pl.BlockSpec((tm, D_CONV), lambda i: (i, 0)),
        ] + cast_specs,
        out_shape=[
            jax.ShapeDtypeStruct((m, D_ATTN), BF16),
            jax.ShapeDtypeStruct((n_seq, D_ATTN, seq), F32),
            jax.ShapeDtypeStruct((m * N_HEADS, V_DIM), F32),
            jax.ShapeDtypeStruct((m, D_ATTN), BF16),
            jax.ShapeDtypeStruct((m, D_CONV), F32),
        ] + [jax.ShapeDtypeStruct(w.shape, BF16) for w in cast_weights],
        compiler_params=_cparams(("parallel",)),
        name="inproj",
    )(x2d, g_pre_mix, w_in16, *cast_weights)


ATTN_TILE = 1024
ATTN_KEY_TILE = 256
ATTN_ROW_BLOCK = 256
LANES = 128


def _attn_prompt_kernel(lq1_ref, lk1_ref, lq2_ref, lk2_ref, gsub_ref, q_ref, kt_ref, v_ref, o_ref,
                        qs_scr, m_scr, l_scr, acc_scr, *, lam_init):
    t, w, rb = ATTN_TILE, ATTN_KEY_TILE, ATTN_ROW_BLOCK
    n_rb = t // rb
    qi = pl.program_id(1)
    lane = lax.broadcasted_iota(jnp.int32, (rb, 2 * HEAD_DIM), 1)
    kcol = lax.broadcasted_iota(jnp.int32, (1, w), 1)

    for h in range(N_HEADS):
        hs = slice(h * V_DIM, (h + 1) * V_DIM)
        for g in range(n_rb):
            q = q_ref[g * rb:(g + 1) * rb, hs]
            zero = jnp.zeros_like(q)
            qs_scr[h, g] = jnp.concatenate([jnp.where(lane < HEAD_DIM, q, zero),
                                            jnp.where(lane >= HEAD_DIM, q, zero)], axis=0)
    m_scr[...] = jnp.full(m_scr.shape, NEG_BIG, F32)
    l_scr[...] = jnp.zeros(l_scr.shape, F32)
    acc_scr[...] = jnp.zeros(acc_scr.shape, F32)

    def tile(k0, d):
        koff = (kcol + (k0 - qi * t)).astype(F32)
        for h in range(N_HEADS):
            hs = slice(h * V_DIM, (h + 1) * V_DIM)
            kt = kt_ref[hs, pl.ds(k0, w)].astype(BF16)
            vv = v_ref[pl.ds(k0, w), hs]
            bias = ALIBI_SLOPES[h] * koff
            for g in range(n_rb):
                masked = False
                if d is not None:
                    if (g + 1) * rb <= d * w:
                        continue
                    masked = g * rb < (d + 1) * w - 1
                s = jnp.dot(qs_scr[h, g], kt, preferred_element_type=F32) + bias
                if masked:
                    row = lax.broadcasted_iota(jnp.int32, (2 * rb, w), 0)
                    row = jnp.where(row >= rb, row - rb, row) + (g * rb - d * w)
                    col = lax.broadcasted_iota(jnp.int32, (2 * rb, w), 1)
                    s = jnp.where(col <= row, s, NEG_BIG)
                m_prev = m_scr[h, g]
                m_new = jnp.maximum(m_prev, jnp.max(s, axis=-1, keepdims=True))
                alpha = jnp.exp2(m_prev - m_new)
                p = jnp.exp2(s - jnp.concatenate([m_new] * (w // LANES), axis=-1))
                l_scr[h, g] = alpha * l_scr[h, g] + jnp.sum(p, axis=-1, keepdims=True)
                acc_scr[h, g] = alpha * acc_scr[h, g] + jnp.dot(p.astype(BF16), vv,
                                                               preferred_element_type=F32)
                m_scr[h, g] = m_new

    def below_diagonal(kv, carry):
        tile(pl.multiple_of(kv * w, w), None)
        return carry

    lax.fori_loop(0, qi * (t // w), below_diagonal, 0)
    for d in range(t // w):
        tile(pl.multiple_of(qi * t + d * w, w), d)

    lam = _diff_lambda(lq1_ref[...], lk1_ref[...], lq2_ref[...], lk2_ref[...], lam_init)
    for h in range(N_HEADS):
        for g in range(n_rb):
            o = acc_scr[h, g] / l_scr[h, g]
            oh = o[:rb] - lam * o[rb:]
            y = _rms(oh, gsub_ref[...]) * (1.0 - lam_init)
            o_ref[g * rb:(g + 1) * rb, h * V_DIM:(h + 1) * V_DIM] = y.astype(o_ref.dtype)


def _attn_prompt(q16, kt, v16, lam_vecs, g_subln, batch, seq, lam_init):
    t, rb = ATTN_TILE, ATTN_ROW_BLOCK
    nq = seq // t
    n_rb = t // rb
    kern = functools.partial(_attn_prompt_kernel, lam_init=lam_init)
    vec = lambda n: pl.BlockSpec((1, n), lambda b, qi: (0, 0))
    return pl.pallas_call(
        kern,
        grid=(batch, nq),
        in_specs=[vec(HEAD_DIM)] * 4 + [
            vec(V_DIM),
            pl.BlockSpec((t, D_ATTN), lambda b, qi: (b * nq + qi, 0)),
            pl.BlockSpec((None, D_ATTN, seq), lambda b, qi: (b, 0, 0)),
            pl.BlockSpec((seq, D_ATTN), lambda b, qi: (b, 0)),
        ],
        out_specs=pl.BlockSpec((t, D_ATTN), lambda b, qi: (b * nq + qi, 0)),
        out_shape=jax.ShapeDtypeStruct((batch * seq, D_ATTN), BF16),
        scratch_shapes=[
            pltpu.VMEM((N_HEADS, n_rb, 2 * rb, 2 * HEAD_DIM), BF16),
            pltpu.VMEM((N_HEADS, n_rb, 2 * rb, LANES), F32),
            pltpu.VMEM((N_HEADS, n_rb, 2 * rb, LANES), F32),
            pltpu.VMEM((N_HEADS, n_rb, 2 * rb, V_DIM), F32),
        ],
        compiler_params=_cparams(("parallel", "arbitrary")),
        name="attn_prompt",
    )(*lam_vecs, g_subln, q16, kt, v16)


PAGED_CHAINS = 1


def _paged_rows(dec_seq):
    rph = 2 * dec_seq
    n_rows = N_HEADS * rph
    row = lax.broadcasted_iota(jnp.int32, (n_rows, 1), 0)
    head = row // rph
    comp = (row % rph) // dec_seq
    tok = row % dec_seq
    slope = jnp.zeros((n_rows, 1), F32)
    for h in range(N_HEADS):
        slope = jnp.where(head == h, ALIBI_SLOPES[h], slope)
    return rph, n_rows, head, comp, tok, slope


def _paged_init(q_ref, q2_scr, qbd_scr, m_scr, l_scr, acc_scr, *, dec_seq):
    rph, n_rows, head, comp, _, _ = _paged_rows(dec_seq)
    q = q_ref[...]
    q2_scr[0:dec_seq, :] = q
    q2_scr[dec_seq:rph, :] = q
    qrep = jnp.concatenate([q2_scr[...]] * N_HEADS, axis=0)
    col = lax.broadcasted_iota(jnp.int32, (n_rows, D_ATTN), 1)
    qbd_scr[...] = jnp.where(col // HEAD_DIM == head * 2 + comp, qrep, 0.0).astype(BF16)
    m_scr[...] = jnp.full(m_scr.shape, NEG_BIG, F32)
    l_scr[...] = jnp.zeros(l_scr.shape, F32)
    acc_scr[...] = jnp.zeros(acc_scr.shape, F32)


def _paged_update(c, s, v_refs, n_slots, m_scr, l_scr, acc_scr, rph):
    m_prev = m_scr[c]
    m_new = jnp.maximum(m_prev, jnp.max(s, axis=-1, keepdims=True))
    alpha = jnp.exp2(m_prev - m_new)
    p = jnp.exp2(s - m_new)
    l_scr[c] = alpha * l_scr[c] + jnp.sum(p, axis=-1, keepdims=True)
    p16 = p.astype(BF16)
    for h in range(N_HEADS):
        rs = slice(h * rph, (h + 1) * rph)
        vh = jnp.concatenate([r[pl.ds(h, n_slots, stride=N_HEADS), :] for r in v_refs], axis=0)
        pv = jnp.dot(p16[rs, :], vh.astype(BF16), preferred_element_type=F32)
        acc_scr[c, rs, :] = alpha[rs] * acc_scr[c, rs, :] + pv
    m_scr[c] = m_new


def _paged_pages(j, ktp_refs, vip_refs, qbd_scr, m_scr, l_scr, acc_scr, *, past_len, dec_seq):
    rph, _, _, _, _, slope = _paged_rows(dec_seq)
    pps = len(ktp_refs)
    ppc = pps // PAGED_CHAINS
    qbd = qbd_scr[...]
    kcol = lax.broadcasted_iota(jnp.int32, (1, ppc * PAGE_SIZE), 1)
    for c in range(PAGED_CHAINS):
        pages = range(c * ppc, (c + 1) * ppc)
        s = jnp.concatenate(
            [jnp.dot(qbd, ktp_refs[i][...].astype(BF16), preferred_element_type=F32) for i in pages], axis=-1)
        koff = (kcol + ((j * pps + c * ppc) * PAGE_SIZE - past_len)).astype(F32)
        _paged_update(c, s + slope * koff, [vip_refs[i] for i in pages], PAGE_SIZE, m_scr, l_scr, acc_scr, rph)


def _paged_finish(b, lam_refs, gsub_ref, ktn_ref, vin_ref, o_ref, qbd_scr, m_scr, l_scr, acc_scr, *,
                  dec_seq, lam_init):
    rph, n_rows, _, _, tok, slope = _paged_rows(dec_seq)
    n_new = ktn_ref.shape[1]
    sn = jnp.dot(qbd_scr[...], ktn_ref[...].astype(BF16), preferred_element_type=F32)
    col = lax.broadcasted_iota(jnp.int32, (n_rows, n_new), 1)
    ctok = col % dec_seq
    valid = jnp.logical_and(col // dec_seq == b, ctok <= tok)
    sn = jnp.where(valid, sn + slope * ctok.astype(F32), NEG_BIG)
    _paged_update(0, sn, [vin_ref], n_new, m_scr, l_scr, acc_scr, rph)

    m = m_scr[0]
    for c in range(1, PAGED_CHAINS):
        m = jnp.maximum(m, m_scr[c])
    l = jnp.zeros((n_rows, 1), F32)
    acc = jnp.zeros((n_rows, V_DIM), F32)
    for c in range(PAGED_CHAINS):
        w = jnp.exp2(m_scr[c] - m)
        l = l + w * l_scr[c]
        acc = acc + w * acc_scr[c]

    lam = _diff_lambda(*[r[...] for r in lam_refs], lam_init)
    o = acc / l
    for h in range(N_HEADS):
        oh = o[h * rph:h * rph + dec_seq] - lam * o[h * rph + dec_seq:(h + 1) * rph]
        o_ref[:, h * V_DIM:(h + 1) * V_DIM] = _rms(oh, gsub_ref[...]) * (1.0 - lam_init)


SUBLANES = 8
CONV_HALO = 32
CONV_ROW_CHUNK = 16


def _ln_swish(c, g, b):
    mu = jnp.mean(c, axis=-1, keepdims=True)
    d = c - mu
    var = jnp.mean(d * d, axis=-1, keepdims=True)
    y = d * lax.rsqrt(var + EPS) * g + b
    return y * jax.nn.sigmoid(y)


def _conv_prompt_kernel(prev_ref, cur_ref, w_ref, b_ref, g_ref, beta_ref, o_ref, ext_scr, sh_scr, *, tm):
    i = pl.program_id(1)
    n_ext = CONV_HALO + tm
    ext_scr[0:CONV_HALO, :] = jnp.where(i == 0, 0.0, prev_ref[...])
    ext_scr[CONV_HALO:n_ext, :] = cur_ref[...]
    ext = ext_scr[...]
    for r in range(1, SUBLANES):
        sh_scr[r] = pltpu.roll(ext, n_ext - r, axis=0)
    first = CONV_HALO - (CONV_WIDTH - 1)
    for c in range(0, tm, CONV_ROW_CHUNK):
        acc = jnp.broadcast_to(b_ref[...], (CONV_ROW_CHUNK, D_CONV))
        for j in range(CONV_WIDTH):
            r = (first + j) % SUBLANES
            a = c + first + j - r
            src = ext_scr[a:a + CONV_ROW_CHUNK, :] if r == 0 else sh_scr[r, a:a + CONV_ROW_CHUNK, :]
            acc = acc + jnp.concatenate([w_ref[j]] * (CONV_ROW_CHUNK // SUBLANES), axis=0) * src
        o_ref[c:c + CONV_ROW_CHUNK, :] = _ln_swish(acc, g_ref[...], beta_ref[...]).astype(o_ref.dtype)


def _conv_prompt(u2d, w_dw, b_dw, ln_g, ln_b, batch, seq, tm=PROMPT_ROW_TILE):
    nt = seq // tm
    halo_per_tile = tm // CONV_HALO
    kern = functools.partial(_conv_prompt_kernel, tm=tm)
    vec = pl.BlockSpec((1, D_CONV), lambda b, i: (0, 0))
    w_rep = jnp.broadcast_to(w_dw[:, None, :], (CONV_WIDTH, SUBLANES, D_CONV))
    return pl.pallas_call(
        kern,
        grid=(batch, nt),
        in_specs=[
            pl.BlockSpec((CONV_HALO, D_CONV),
                         lambda b, i: (jnp.maximum((b * nt + i) * halo_per_tile - 1, 0), 0)),
            pl.BlockSpec((tm, D_CONV), lambda b, i: (b * nt + i, 0)),
            pl.BlockSpec((CONV_WIDTH, SUBLANES, D_CONV), lambda b, i: (0, 0, 0)),
            vec, vec, vec,
        ],
        out_specs=pl.BlockSpec((tm, D_CONV), lambda b, i: (b * nt + i, 0)),
        out_shape=jax.ShapeDtypeStruct((batch * seq, D_CONV), BF16),
        scratch_shapes=[pltpu.VMEM((CONV_HALO + tm, D_CONV), F32),
                        pltpu.VMEM((SUBLANES, CONV_HALO + tm, D_CONV), F32)],
        compiler_params=_cparams(("parallel", "parallel")),
        name="conv_prompt",
    )(u2d, u2d, w_rep, b_dw, ln_g, ln_b)


def _conv_sample_kernel(state_ref, u_ref, w_ref, b_ref, g_ref, beta_ref, o_ref, st_ref, *, dec_batch, dec_seq):
    n_state = CONV_WIDTH - 1

    def ext(i):
        return state_ref[i] if i < n_state else u_ref[i - n_state]

    for t in range(dec_seq):
        acc = jnp.broadcast_to(b_ref[...], (dec_batch, D_CONV))
        for j in range(CONV_WIDTH):
            acc = acc + w_ref[j:j + 1, :] * ext(t + j)
        o_ref[t] = _ln_swish(acc, g_ref[...], beta_ref[...])
    for i in range(n_state):
        st_ref[i] = ext(i + dec_seq)


def _conv_sample(state_tm, u_s, w_dw, b_dw, ln_g, ln_b, dec_batch, dec_seq):
    kern = functools.partial(_conv_sample_kernel, dec_batch=dec_batch, dec_seq=dec_seq)
    u_tm = jnp.transpose(u_s.reshape(dec_batch, dec_seq, D_CONV), (1, 0, 2))
    y_tm, st_tm = pl.pallas_call(
        kern,
        out_shape=[
            jax.ShapeDtypeStruct((dec_seq, dec_batch, D_CONV), F32),
            jax.ShapeDtypeStruct(state_tm.shape, F32),
        ],
        compiler_params=pltpu.CompilerParams(vmem_limit_bytes=VMEM_LIMIT_BYTES),
        name="conv_sample",
    )(state_tm, u_tm, w_dw, b_dw, ln_g, ln_b)
    return jnp.transpose(y_tm, (1, 0, 2)).reshape(dec_batch * dec_seq, D_CONV), st_tm


def _post_first(ya_ref, yc_ref, x_ref, wo_ref, gpm_ref, gpf_ref, h_scr, hn_scr, acc_scr):
    y_mix = jnp.concatenate([ya_ref[...].astype(BF16), yc_ref[...].astype(BF16)], axis=-1)
    m = jnp.dot(y_mix, wo_ref[...], preferred_element_type=F32)
    h = x_ref[...] + _rms(m, gpm_ref[...])
    h_scr[...] = h
    hn_scr[...] = _rms(h, gpf_ref[...]).astype(BF16)
    acc_scr[...] = jnp.zeros(acc_scr.shape, F32)


def _post_ffn_chunk(w1_ref, w2_ref, hn_scr, acc_scr):
    a = jnp.dot(hn_scr[...], w1_ref[...], preferred_element_type=F32)
    a = jnp.square(jnp.maximum(a, 0.0)).astype(BF16)
    acc_scr[...] += jnp.dot(a, w2_ref[...], preferred_element_type=F32)


def _post_last(pe_ref, gpost_ref, wg_ref, wp_ref, o_ref, h_scr, acc_scr):
    h2 = h_scr[...] + _rms(acc_scr[...], gpost_ref[...])
    gate = jax.nn.sigmoid(jnp.dot(h2.astype(BF16), wg_ref[...], preferred_element_type=F32))
    pev = jnp.dot(pe_ref[...].astype(BF16), wp_ref[...], preferred_element_type=F32)
    o_ref[...] = h2 + gate * pev


def _post_kernel(ya_ref, yc_ref, x_ref, pe_ref, wo_ref, gpm_ref, gpf_ref, w1_ref, w2_ref, gpost_ref,
                 wg_ref, wp_ref, o_ref, h_scr, hn_scr, acc_scr):
    k = pl.program_id(1)

    @pl.when(k == 0)
    def _():
        _post_first(ya_ref, yc_ref, x_ref, wo_ref, gpm_ref, gpf_ref, h_scr, hn_scr, acc_scr)

    _post_ffn_chunk(w1_ref, w2_ref, hn_scr, acc_scr)

    @pl.when(k == pl.num_programs(1) - 1)
    def _():
        _post_last(pe_ref, gpost_ref, wg_ref, wp_ref, o_ref, h_scr, acc_scr)


N_POST_IN = 12
N_PAGED_IN = 10


def _page_copies(pt_ref, seq, chunk, slot, kt_pool_ref, vi_pool_ref, kbuf, vbuf, sem, pps):
    copies = []
    for p in range(pps):
        page = pt_ref[seq, chunk * pps + p]
        copies.append(pltpu.make_async_copy(kt_pool_ref.at[page], kbuf.at[slot, p], sem.at[slot, 0]))
        copies.append(pltpu.make_async_copy(vi_pool_ref.at[page], vbuf.at[slot, p], sem.at[slot, 1]))
    return copies


def _post_paged_kernel(pt_ref, *refs, pps, past_len, dec_seq, lam_init):
    (ya_ref, yc_ref, x_ref, pe_ref, wo_ref, gpm_ref, gpf_ref, w1_ref, w2_ref, gpost_ref, wg_ref,
     wp_ref) = refs[:N_POST_IN]
    lam_refs = refs[N_POST_IN:N_POST_IN + 4]
    gsub_ref, q_ref, ktn_ref, vin_ref, kt_pool_ref, vi_pool_ref = refs[N_POST_IN + 4:N_POST_IN + N_PAGED_IN]
    o_ref, oa_ref = refs[N_POST_IN + N_PAGED_IN:N_POST_IN + N_PAGED_IN + 2]
    (h_scr, hn_scr, acc_scr, q2_scr, qbd_scr, m_scr, l_scr, pacc_scr, kbuf, vbuf,
     sem) = refs[N_POST_IN + N_PAGED_IN + 2:]
    i = pl.program_id(0)
    k = pl.program_id(1)
    n_k = pl.num_programs(1)
    step = i * n_k + k
    slot = step % 2
    copies = functools.partial(_page_copies, pt_ref, kt_pool_ref=kt_pool_ref, vi_pool_ref=vi_pool_ref,
                               kbuf=kbuf, vbuf=vbuf, sem=sem, pps=pps)

    @pl.when(step == 0)
    def _():
        for c in copies(0, 0, 0):
            c.start()

    @pl.when(step + 1 < pl.num_programs(0) * n_k)
    def _():
        nxt = step + 1
        for c in copies(nxt // n_k, nxt % n_k, 1 - slot):
            c.start()

    @pl.when(k == 0)
    def _():
        _post_first(ya_ref, yc_ref, x_ref, wo_ref, gpm_ref, gpf_ref, h_scr, hn_scr, acc_scr)
        _paged_init(q_ref, q2_scr, qbd_scr, m_scr, l_scr, pacc_scr, dec_seq=dec_seq)

    _post_ffn_chunk(w1_ref, w2_ref, hn_scr, acc_scr)
    for c in copies(i, k, slot):
        c.wait()
    _paged_pages(k, [kbuf.at[slot, p] for p in range(pps)], [vbuf.at[slot, p] for p in range(pps)],
                 qbd_scr, m_scr, l_scr, pacc_scr, past_len=past_len, dec_seq=dec_seq)

    @pl.when(k == n_k - 1)
    def _():
        _post_last(pe_ref, gpost_ref, wg_ref, wp_ref, o_ref, h_scr, acc_scr)
        _paged_finish(i, lam_refs, gsub_ref, ktn_ref, vin_ref, oa_ref, qbd_scr, m_scr, l_scr, pacc_scr,
                      dec_seq=dec_seq, lam_init=lam_init)


POST_ROW_TILE = 512
POST_FF_CHUNK = 1024


def _post_specs(tm, tf, imap):
    row_tile = lambda n: pl.BlockSpec((tm, n), imap(lambda i, k: (i, 0)))
    vec = pl.BlockSpec((1, D_MODEL), imap(lambda i, k: (0, 0)))
    resident = lambda r, c: pl.BlockSpec((r, c), imap(lambda i, k: (0, 0)), pipeline_mode=pl.Buffered(1))
    in_specs = [
        row_tile(D_ATTN), row_tile(D_CONV), row_tile(D_MODEL), row_tile(D_PLE),
        resident(D_MODEL, D_MODEL), vec, vec,
        pl.BlockSpec((D_MODEL, tf), imap(lambda i, k: (0, k))),
        pl.BlockSpec((tf, D_MODEL), imap(lambda i, k: (k, 0))),
        vec,
        resident(D_MODEL, D_MODEL), resident(D_PLE, D_MODEL),
    ]
    scratch = [pltpu.VMEM((tm, D_MODEL), F32), pltpu.VMEM((tm, D_MODEL), BF16), pltpu.VMEM((tm, D_MODEL), F32)]
    return in_specs, row_tile(D_MODEL), scratch


def _post(post_args, tm):
    m = post_args[2].shape[0]
    in_specs, out_spec, scratch = _post_specs(tm, POST_FF_CHUNK, lambda f: f)
    return pl.pallas_call(
        _post_kernel,
        grid=(m // tm, D_FF // POST_FF_CHUNK),
        in_specs=in_specs,
        out_specs=out_spec,
        out_shape=jax.ShapeDtypeStruct((m, D_MODEL), F32),
        scratch_shapes=scratch,
        compiler_params=_cparams(("parallel", "arbitrary")),
        name="post",
    )(*post_args)


def _post_paged(post_args, q_s, kt_new, vi_new, kt_pool, vi_pool, page_table, lam_vecs, g_subln,
                dec_batch, dec_seq, lam_init):
    tm, tf = POST_ROW_TILE, POST_FF_CHUNK
    m = post_args[2].shape[0]
    n_row_tiles, n_ff = m // tm, D_FF // tf
    n_pages = page_table.shape[1]
    assert n_row_tiles == dec_batch and n_pages % n_ff == 0
    pps = n_pages // n_ff
    assert pps % PAGED_CHAINS == 0
    past_len = n_pages * PAGE_SIZE
    rph = 2 * dec_seq
    n_rows = N_HEADS * rph
    n_new = dec_batch * dec_seq
    q3 = q_s.astype(F32).reshape(dec_batch, dec_seq, D_ATTN)
    with_pt = lambda f: (lambda i, k, pt: f(i, k))
    post_in, post_out, post_scratch = _post_specs(tm, tf, with_pt)
    vec = lambda n: pl.BlockSpec((1, n), lambda i, k, pt: (0, 0))

    grid_spec = pltpu.PrefetchScalarGridSpec(
        num_scalar_prefetch=1,
        grid=(n_row_tiles, n_ff),
        in_specs=post_in + [vec(HEAD_DIM)] * 4 + [
            vec(V_DIM),
            pl.BlockSpec((None, dec_seq, D_ATTN), lambda i, k, pt: (i, 0, 0)),
            pl.BlockSpec((D_ATTN, n_new), lambda i, k, pt: (0, 0)),
            pl.BlockSpec((n_new * N_HEADS, V_DIM), lambda i, k, pt: (0, 0)),
            pl.BlockSpec(memory_space=pl.ANY),
            pl.BlockSpec(memory_space=pl.ANY),
        ],
        out_specs=[post_out, pl.BlockSpec((None, dec_seq, D_ATTN), lambda i, k, pt: (i, 0, 0))],
        scratch_shapes=post_scratch + [
            pltpu.VMEM((rph, D_ATTN), F32),
            pltpu.VMEM((n_rows, D_ATTN), BF16),
            pltpu.VMEM((PAGED_CHAINS, n_rows, 1), F32),
            pltpu.VMEM((PAGED_CHAINS, n_rows, 1), F32),
            pltpu.VMEM((PAGED_CHAINS, n_rows, V_DIM), F32),
            pltpu.VMEM((2, pps, D_ATTN, PAGE_SIZE), F32),
            pltpu.VMEM((2, pps, PAGE_SIZE * N_HEADS, V_DIM), F32),
            pltpu.SemaphoreType.DMA((2, 2)),
        ],
    )
    kern = functools.partial(_post_paged_kernel, pps=pps, past_len=past_len, dec_seq=dec_seq, lam_init=lam_init)
    out, ya_s = pl.pallas_call(
        kern,
        grid_spec=grid_spec,
        out_shape=[jax.ShapeDtypeStruct((m, D_MODEL), F32),
                   jax.ShapeDtypeStruct((dec_batch, dec_seq, D_ATTN), F32)],
        compiler_params=_cparams(("arbitrary", "arbitrary")),
        name="post_paged",
    )(page_table, *post_args, *lam_vecs, g_subln, q3, kt_new, vi_new, kt_pool, vi_pool)
    return out, ya_s.reshape(dec_batch * dec_seq, D_ATTN)


def _row(v):
    return v.reshape(1, -1)


def kernel(x_prompt, x_sample, cache_k, cache_v, state_conv, page_table, p_prompt, p_sample,
           w_in, w_out, lambda_q1, lambda_k1, lambda_q2, lambda_k2, g_subln, w_dw, b_dw,
           ln_conv_g, ln_conv_b, g_pre_mix, g_post_mix, g_pre_ffn, g_post_ffn,
           w_ff1, w_ff2, w_ple, w_ple_gate):
    depth = w_in.shape[0]
    batch, seq, _ = x_prompt.shape
    dec_batch, dec_seq, _ = x_sample.shape
    n_pool = cache_k.shape[1]
    mp, ms = batch * seq, dec_batch * dec_seq
    assert 2 * dec_seq == 8 and cache_k.shape[2] == PAGE_SIZE

    hp = x_prompt.reshape(mp, D_MODEL)
    hs = x_sample.reshape(ms, D_MODEL)
    outs = {k: [] for k in ("kp", "vp", "cp", "ks", "vs", "cs")}
    for l in range(depth):
        lam_init = 0.8 - 0.6 * math.exp(-0.3 * l)
        w_in16 = w_in[l].astype(BF16)
        w_ple16 = w_ple[l].astype(BF16)
        lam_vecs = [_row(lambda_q1[l]), _row(lambda_k1[l]), _row(lambda_q2[l]), _row(lambda_k2[l])]
        gsub = _row(g_subln[l])
        conv_args = (w_dw[l], _row(b_dw[l]), _row(ln_conv_g[l]), _row(ln_conv_b[l]))
        kt_pool = jnp.transpose(cache_k[l], (0, 2, 3, 4, 1)).reshape(n_pool, D_ATTN, PAGE_SIZE)
        vi_pool = cache_v[l].reshape(n_pool, PAGE_SIZE * N_HEADS, V_DIM)

        q_p, kt_p, vi_p, v16_p, u_p, w_out16, w_ff1_16, w_ff2_16, w_gate16 = _inproj(
            hp, _row(g_pre_mix[l]), w_in16, n_seq=batch, tm=PROMPT_ROW_TILE,
            cast_weights=(w_out[l], w_ff1[l], w_ff2[l], w_ple_gate[l]))
        post_w = (w_out16, _row(g_post_mix[l]), _row(g_pre_ffn[l]), w_ff1_16, w_ff2_16, _row(g_post_ffn[l]),
                  w_gate16, w_ple16)
        ya_p = _attn_prompt(q_p, kt_p, v16_p, lam_vecs, gsub, batch, seq, lam_init)
        yc_p = _conv_prompt(u_p, *conv_args, batch, seq)
        q_s, kt_s, vi_s, _, u_s = _inproj(hs, _row(g_pre_mix[l]), w_in16, n_seq=1, tm=ms)
        yc_s, st_tm = _conv_sample(jnp.transpose(state_conv[l], (1, 0, 2)), u_s, *conv_args,
                                   dec_batch, dec_seq)

        hp, ya_s = _post_paged((ya_p, yc_p, hp, p_prompt[l].reshape(mp, D_PLE)) + post_w,
                               q_s, kt_s[0], vi_s, kt_pool, vi_pool, page_table, lam_vecs, gsub,
                               dec_batch, dec_seq, lam_init)
        hs = _post((ya_s, yc_s, hs, p_sample[l].reshape(ms, D_PLE)) + post_w, tm=ms)

        outs["kp"].append(jnp.transpose(kt_p.reshape(batch, N_HEADS, 2, HEAD_DIM, seq), (0, 4, 1, 2, 3)))
        outs["vp"].append(vi_p.reshape(batch, seq, N_HEADS, V_DIM))
        outs["cp"].append(u_p.reshape(batch, seq, D_CONV)[:, seq - (CONV_WIDTH - 1):])
        outs["ks"].append(jnp.transpose(kt_s.reshape(N_HEADS, 2, HEAD_DIM, dec_batch, dec_seq),
                                        (3, 4, 0, 1, 2)))
        outs["vs"].append(vi_s.reshape(dec_batch, dec_seq, N_HEADS, V_DIM))
        outs["cs"].append(jnp.transpose(st_tm, (1, 0, 2)))

    return (hp.reshape(batch, seq, D_MODEL), hs.reshape(dec_batch, dec_seq, D_MODEL),
            jnp.stack(outs["kp"]), jnp.stack(outs["vp"]), jnp.stack(outs["cp"]),
            jnp.stack(outs["ks"]), jnp.stack(outs["vs"]), jnp.stack(outs["cs"]))
```
